```python
import jax
import jax.numpy as jnp
from jax import lax
import numpy as np

D_MODEL = 1024
BATCH = 16
SEQ = 256
DEPTH = 2
DEC_BATCH = 4
DEC_SEQ = 4096
PAST_LEN = 512

GRID_W = 64
N_EVEN = (DEPTH + 1) // 2
N_ODD = DEPTH // 2
RET_HEADS = 4
RET_DK = 64
RET_DV = 128
GDN_HEADS = 4
GDN_DK = 128
GDN_DV = 128
CONV_W = 3
CHUNK = 64
ATT_HEADS = 16
ATT_KV_HEADS = 4
ATT_HD = 64
ATT_GROUP = ATT_HEADS // ATT_KV_HEADS
Q_BLOCK = 128
ROPE_THETA = 10000.0
ROPE_PAIRS = ATT_HD // 4
FFN_HIDDEN = 2816
N_MOD = 9
EPS = 1e-6

RET_QK = RET_HEADS * RET_DK
RET_V = RET_HEADS * RET_DV
GDN_QK = GDN_HEADS * GDN_DK
GDN_V = GDN_HEADS * GDN_DV
EVEN_SPLITS = (RET_QK, RET_QK, RET_V, RET_V, GDN_QK, GDN_QK, GDN_V, GDN_V, 2 * GDN_HEADS, 2 * GDN_HEADS)
EVEN_IN = 2 * RET_QK + 2 * RET_V + 2 * GDN_QK + 2 * GDN_V + 4 * GDN_HEADS
EVEN_OUT = RET_V + GDN_V
ATT_Q = ATT_HEADS * ATT_HD
ATT_KV = ATT_KV_HEADS * ATT_HD
ODD_IN = ATT_Q + 2 * ATT_KV
ODD_OUT = ATT_Q

kernel_name = 'hybrid_prefix_dit_retention_gdn_gqa_step'


def _split(x, sizes):
    out, start = [], 0
    for s in sizes:
        out.append(x[..., start:start + s])
        start += s
    return out


def rmsnorm(x, w):
    xf = x.astype(jnp.float32)
    y = xf * lax.rsqrt(jnp.mean(xf * xf, axis=-1, keepdims=True) + EPS)
    return (y * w.astype(jnp.float32)).astype(x.dtype)


def l2norm(x):
    xf = x.astype(jnp.float32)
    return (xf * lax.rsqrt(jnp.sum(xf * xf, axis=-1, keepdims=True) + EPS)).astype(x.dtype)


def modulation(cond, w, b):
    m = jax.nn.silu(cond) @ w + b
    return m.reshape(cond.shape[0], N_MOD, D_MODEL)


def ada_norm(x, w, m, j):
    shift, scale = m[:, None, 3 * j], m[:, None, 3 * j + 1]
    return rmsnorm(x, w) * (1 + scale) + shift


def ffn_sublayer(x, m, j, w_norm, w_in, w_out):
    h = ada_norm(x, w_norm, m, j)
    a, b = _split(h @ w_in, (FFN_HIDDEN, FFN_HIDDEN))
    return x + 0.5 * m[:, None, 3 * j + 2] * ((jax.nn.silu(a) * b) @ w_out)


def _chunks(x):
    b, h, t = x.shape[:3]
    return x.reshape((b, h, t // CHUNK, CHUNK) + x.shape[3:])


def _decay_matrix(gc):
    causal = jnp.tril(jnp.ones((CHUNK, CHUNK), dtype=bool))
    diff = gc[..., :, None] - gc[..., None, :]
    return jnp.exp(jnp.where(causal, diff, -jnp.inf))


def chunk_decay_linear_attn(q, k, v, g, s0):
    out_dtype = q.dtype
    b, h, t, dv = v.shape
    qc, kc, vc = (_chunks(a.astype(jnp.float32)) for a in (q, k, v))
    gc = jnp.cumsum(_chunks(g.astype(jnp.float32)), axis=-1)
    decay = _decay_matrix(gc)
    scores = jnp.einsum('bhncd,bhnsd->bhncs', qc, kc) * decay
    o_intra = jnp.einsum('bhncs,bhnsv->bhncv', scores, vc)
    q_dec = qc * jnp.exp(gc)[..., None]
    k_dec = kc * jnp.exp(gc[..., -1:] - gc)[..., None]
    chunk_decay = jnp.exp(gc[..., -1])

    def step(state, xs):
        qd, kd, vv, cd = xs
        o = jnp.einsum('bhcd,bhdv->bhcv', qd, state)
        state = state * cd[..., None, None] + jnp.einsum('bhcd,bhcv->bhdv', kd, vv)
        return state, o

    xs = tuple(jnp.moveaxis(a, 2, 0) for a in (q_dec, k_dec, vc, chunk_decay))
    s_fin, o_inter = lax.scan(step, s0.astype(jnp.float32), xs)
    o = o_intra + jnp.moveaxis(o_inter, 0, 2)
    return o.reshape(b, h, t, dv).astype(out_dtype), s_fin


def chunk_gated_delta_rule(q, k, v, g, beta, s0):
    out_dtype = q.dtype
    b, h, t, dv = v.shape
    q, k, v = (a.astype(jnp.float32) for a in (q, k, v))
    beta = beta.astype(jnp.float32)[..., None]
    qc, kc = _chunks(q), _chunks(k)
    kbc, vbc = _chunks(k * beta), _chunks(v * beta)
    gc = jnp.cumsum(_chunks(g.astype(jnp.float32)), axis=-1)
    decay = _decay_matrix(gc)
    strict = jnp.tril(jnp.ones((CHUNK, CHUNK), dtype=bool), -1)
    m = jnp.where(strict, jnp.einsum('bhncd,bhnsd->bhncs', kbc, kc) * decay, 0.0)
    a_mat = m + jnp.eye(CHUNK, dtype=jnp.float32)
    rhs = jnp.concatenate([vbc, kbc * jnp.exp(gc)[..., None]], axis=-1)
    sol = lax.linalg.triangular_solve(a_mat, rhs, left_side=True, lower=True, unit_diagonal=True)
    u, w = sol[..., :dv], sol[..., dv:]
    attn = jnp.einsum('bhncd,bhnsd->bhncs', qc, kc) * decay
    q_dec = qc * jnp.exp(gc)[..., None]
    k_dec = kc * jnp.exp(gc[..., -1:] - gc)[..., None]
    chunk_decay = jnp.exp(gc[..., -1])

    def step(state, xs):
        qd, kd, uu, ww, at, cd = xs
        v_new = uu - jnp.einsum('bhcd,bhdv->bhcv', ww, state)
        o = jnp.einsum('bhcd,bhdv->bhcv', qd, state) + jnp.einsum('bhcs,bhsv->bhcv', at, v_new)
        state = state * cd[..., None, None] + jnp.einsum('bhcd,bhcv->bhdv', kd, v_new)
        return state, o

    xs = tuple(jnp.moveaxis(a, 2, 0) for a in (q_dec, k_dec, u, w, attn, chunk_decay))
    s_fin, o = lax.scan(step, s0.astype(jnp.float32), xs)
    o = jnp.moveaxis(o, 0, 2)
    return o.reshape(b, h, t, dv).astype(out_dtype), s_fin


def short_conv(x, w):
    return lax.conv_general_dilated(
        x, w[:, None, :].astype(x.dtype), window_strides=(1,),
        padding=[(CONV_W // 2, CONV_W // 2)], dimension_numbers=('NWC', 'WIO', 'NWC'),
        feature_group_count=x.shape[-1])


def even_mixer(h, w_in, w_out, decay_logit, ret_norm_w, conv_w, a_log, dt_bias, gdn_norm_w, s0_ret, s0_gdn):
    b, t, _ = h.shape
    rq, rk, rv, rg, gq, gk, gv, gg, ga, gb = _split(h @ w_in, EVEN_SPLITS)

    def heads(x, n):
        return x.reshape(b, t, n, -1).transpose(0, 2, 1, 3)

    def flip(x):
        return jnp.flip(x, axis=2)

    rq_h = heads(rq, RET_HEADS)
    rk_h = heads(rk, RET_HEADS) * RET_DK ** -0.5
    rv_h = heads(rv, RET_HEADS)
    log_gamma = jax.nn.log_sigmoid(decay_logit.astype(jnp.float32))
    g_f = jnp.broadcast_to(log_gamma[0][None, :, None], (b, RET_HEADS, t))
    g_b = jnp.broadcast_to(log_gamma[1][None, :, None], (b, RET_HEADS, t))
    o_f, sr_f = chunk_decay_linear_attn(rq_h, rk_h, rv_h, g_f, s0_ret[:, 0])
    o_b, sr_b = chunk_decay_linear_attn(flip(rq_h), flip(rk_h), flip(rv_h), g_b, s0_ret[:, 1])
    ret = rmsnorm(o_f + flip(o_b), ret_norm_w).transpose(0, 2, 1, 3).reshape(b, t, RET_V)
    ret = ret * jax.nn.silu(rg)

    qkv = jax.nn.silu(short_conv(jnp.concatenate([gq, gk, gv], axis=-1), conv_w))
    cq, ck, cv = _split(qkv, (GDN_QK, GDN_QK, GDN_V))
    q = l2norm(heads(cq, GDN_HEADS)) * GDN_DK ** -0.5
    k = l2norm(heads(ck, GDN_HEADS))
    v = heads(cv, GDN_HEADS)
    ga = ga.reshape(b, t, 2, GDN_HEADS).astype(jnp.float32)
    gb = gb.reshape(b, t, 2, GDN_HEADS).astype(jnp.float32)
    g = -jnp.exp(a_log.astype(jnp.float32)) * jax.nn.softplus(ga + dt_bias.astype(jnp.float32))
    g = g.transpose(2, 0, 3, 1)
    beta = jax.nn.sigmoid(gb).transpose(2, 0, 3, 1)
    od_f, sg_f = chunk_gated_delta_rule(q, k, v, g[0], beta[0], s0_gdn[:, 0])
    od_b, sg_b = chunk_gated_delta_rule(flip(q), flip(k), flip(v), jnp.flip(g[1], axis=-1),
                                        jnp.flip(beta[1], axis=-1), s0_gdn[:, 1])
    gdn = rmsnorm(od_f + flip(od_b), gdn_norm_w).transpose(0, 2, 1, 3).reshape(b, t, GDN_V)
    gdn = gdn * jax.nn.silu(gg)

    out = jnp.concatenate([ret, gdn], axis=-1) @ w_out
    return out, jnp.stack([sr_f, sr_b], axis=1), jnp.stack([sg_f, sg_b], axis=1)


def _rotate(x, ang):
    n = ang.shape[-1]
    cos = jnp.cos(ang)[None, :, None, :]
    sin = jnp.sin(ang)[None, :, None, :]
    xf = x.astype(jnp.float32)
    x1, x2 = xf[..., :n], xf[..., n:]
    return jnp.concatenate([x1 * cos - x2 * sin, x2 * cos + x1 * sin], axis=-1)


def axial_rope(x):
    t = x.shape[1]
    rows = t // GRID_W
    row = jnp.repeat(jnp.arange(rows, dtype=jnp.float32), GRID_W)
    col = (jnp.arange(rows * GRID_W) % GRID_W).astype(jnp.float32)
    inv = ROPE_THETA ** (-jnp.arange(ROPE_PAIRS, dtype=jnp.float32) / ROPE_PAIRS)
    half = ATT_HD // 2
    xr = _rotate(x[..., :half], row[:, None] * inv)
    xc = _rotate(x[..., half:], col[:, None] * inv)
    return jnp.concatenate([xr, xc], axis=-1).astype(x.dtype)


def block_attention(q, k, v):
    b, tq = q.shape[:2]
    nb = tq // Q_BLOCK
    qb = q.reshape(b, nb, Q_BLOCK, ATT_KV_HEADS, ATT_GROUP, ATT_HD).transpose(1, 0, 2, 3, 4, 5)
    scale = ATT_HD ** -0.5

    def one_block(qi):
        s = jnp.einsum('bqhgd,bkhd->bhgqk', qi, k, preferred_element_type=jnp.float32) * scale
        p = jax.nn.softmax(s, axis=-1).astype(v.dtype)
        return jnp.einsum('bhgqk,bkhd->bqhgd', p, v)

    o = lax.map(one_block, qb)
    return o.transpose(1, 0, 2, 3, 4, 5).reshape(b, tq, ATT_HEADS * ATT_HD)


def odd_mixer(h, w_in, w_out, q_norm_w, k_norm_w, ctx_k=None, ctx_v=None):
    b, t, _ = h.shape
    q, k, v = _split(h @ w_in, (ATT_Q, ATT_KV, ATT_KV))
    q = rmsnorm(q.reshape(b, t, ATT_HEADS, ATT_HD), q_norm_w)
    k = rmsnorm(k.reshape(b, t, ATT_KV_HEADS, ATT_HD), k_norm_w)
    v = v.reshape(b, t, ATT_KV_HEADS, ATT_HD)
    if ctx_k is None:
        o = block_attention(q, k, v)
    else:
        q, k = axial_rope(q), axial_rope(k)
        o = block_attention(q, jnp.concatenate([ctx_k, k], axis=1), jnp.concatenate([ctx_v, v], axis=1))
    return o @ w_out, k, v


def setup_inputs(seed: int = 0) -> dict:
    key = jax.random.key(seed)
    ks = jax.random.split(key, 26)
    f32 = jnp.float32

    def nrm(k, shape, s):
        return jax.random.normal(k, shape, f32) * s

    gamma = 1.0 - 2.0 ** (-5.0 - jnp.arange(RET_HEADS, dtype=f32))
    dt = jnp.exp(jax.random.uniform(ks[19], (N_EVEN, 2, GDN_HEADS), f32, np.log(1e-3), np.log(1e-1)))
    return {
        'x_prompt': nrm(ks[0], (BATCH, SEQ, D_MODEL), 1.0),
        'x_sample': nrm(ks[1], (DEC_BATCH, DEC_SEQ, D_MODEL), 1.0),
        'state_ret': nrm(ks[2], (DEC_BATCH, N_EVEN, 2, RET_HEADS, RET_DK, RET_DV), 0.5),
        'state_gdn': nrm(ks[3], (DEC_BATCH, N_EVEN, 2, GDN_HEADS, GDN_DK, GDN_DV), 0.5),
        'cache_k': nrm(ks[4], (DEC_BATCH, N_ODD, PAST_LEN, ATT_KV_HEADS, ATT_HD), 1.0),
        'cache_v': nrm(ks[5], (DEC_BATCH, N_ODD, PAST_LEN, ATT_KV_HEADS, ATT_HD), 1.0),
        'c': nrm(ks[6], (DEC_BATCH, D_MODEL), 1.0),
        'c_ctx': nrm(ks[7], (D_MODEL,), 1.0),
        'mod_w': nrm(ks[8], (DEPTH, D_MODEL, N_MOD * D_MODEL), 0.02),
        'mod_b': nrm(ks[9], (DEPTH, N_MOD * D_MODEL), 0.02),
        'norm_w': 1.0 + nrm(ks[10], (DEPTH, 3, D_MODEL), 0.02),
        'ffn_w_in': nrm(ks[11], (DEPTH, 2, D_MODEL, 2 * FFN_HIDDEN), D_MODEL ** -0.5),
        'ffn_w_out': nrm(ks[12], (DEPTH, 2, FFN_HIDDEN, D_MODEL), FFN_HIDDEN ** -0.5),
        'even_w_in': nrm(ks[13], (N_EVEN, D_MODEL, EVEN_IN), D_MODEL ** -0.5),
        'even_w_out': nrm(ks[14], (N_EVEN, EVEN_OUT, D_MODEL), EVEN_OUT ** -0.5),
        'ret_decay_logit': jnp.log(gamma / (1.0 - gamma)) + nrm(ks[15], (N_EVEN, 2, RET_HEADS), 0.1),
        'ret_norm_w': 1.0 + nrm(ks[16], (N_EVEN, RET_DV), 0.02),
        'gdn_conv_w': nrm(ks[17], (N_EVEN, CONV_W, 2 * GDN_QK + GDN_V), CONV_W ** -0.5),
        'gdn_A_log': jnp.log(jax.random.uniform(ks[18], (N_EVEN, 2, GDN_HEADS), f32, 1.0, 16.0)),
        'gdn_dt_bias': dt + jnp.log(-jnp.expm1(-dt)),
        'gdn_norm_w': 1.0 + nrm(ks[20], (N_EVEN, GDN_DV), 0.02),
        'odd_w_in': nrm(ks[21], (N_ODD, D_MODEL, ODD_IN), D_MODEL ** -0.5),
        'odd_w_out': nrm(ks[22], (N_ODD, ODD_OUT, D_MODEL), ODD_OUT ** -0.5),
        'q_norm_w': 1.0 + nrm(ks[23], (N_ODD, ATT_HD), 0.02),
        'k_norm_w': 1.0 + nrm(ks[24], (N_ODD, ATT_HD), 0.02),
        'final_norm_w': 1.0 + nrm(ks[25], (D_MODEL,), 0.02),
    }


def reference(x_prompt, x_sample, state_ret, state_gdn, cache_k, cache_v, c, c_ctx,
              mod_w, mod_b, norm_w, ffn_w_in, ffn_w_out, even_w_in, even_w_out,
              ret_decay_logit, ret_norm_w, gdn_conv_w, gdn_A_log, gdn_dt_bias, gdn_norm_w,
              odd_w_in, odd_w_out, q_norm_w, k_norm_w, final_norm_w):
    xp, xs = x_prompt, x_sample
    bp = xp.shape[0]
    zero_ret = jnp.zeros((bp, 2, RET_HEADS, RET_DK, RET_DV), jnp.float32)
    zero_gdn = jnp.zeros((bp, 2, GDN_HEADS, GDN_DK, GDN_DV), jnp.float32)
    new_ret, new_gdn, new_k, new_v = [], [], [], []
    for l in range(DEPTH):
        mp = modulation(c_ctx[None], mod_w[l], mod_b[l])
        ms = modulation(c, mod_w[l], mod_b[l])
        xp = ffn_sublayer(xp, mp, 0, norm_w[l, 0], ffn_w_in[l, 0], ffn_w_out[l, 0])
        xs = ffn_sublayer(xs, ms, 0, norm_w[l, 0], ffn_w_in[l, 0], ffn_w_out[l, 0])
        hp = ada_norm(xp, norm_w[l, 1], mp, 1)
        hs = ada_norm(xs, norm_w[l, 1], ms, 1)
        if l % 2 == 0:
            e = l // 2
            prm = (even_w_in[e], even_w_out[e], ret_decay_logit[e], ret_norm_w[e], gdn_conv_w[e],
                   gdn_A_log[e], gdn_dt_bias[e], gdn_norm_w[e])
            op, sr, sg = even_mixer(hp, *prm, zero_ret, zero_gdn)
            os_, _, _ = even_mixer(hs, *prm, state_ret[:, e], state_gdn[:, e])
            new_ret.append(sr)
            new_gdn.append(sg)
        else:
            o = l // 2
            prm = (odd_w_in[o], odd_w_out[o], q_norm_w[o], k_norm_w[o])
            op, kp, vp = odd_mixer(hp, *prm)
            os_, _, _ = odd_mixer(hs, *prm, cache_k[:, o], cache_v[:, o])
            new_k.append(kp)
            new_v.append(vp)
        xp = xp + mp[:, None, 5] * op
        xs = xs + ms[:, None, 5] * os_
        xp = ffn_sublayer(xp, mp, 2, norm_w[l, 2], ffn_w_in[l, 1], ffn_w_out[l, 1])
        xs = ffn_sublayer(xs, ms, 2, norm_w[l, 2], ffn_w_in[l, 1], ffn_w_out[l, 1])
    y_prompt = rmsnorm(xp, final_norm_w)
    y_sample = rmsnorm(xs, final_norm_w)
    new_state_ret = jnp.stack(new_ret, axis=1)
    new_state_gdn = jnp.stack(new_gdn, axis=1)
    new_cache_k = jnp.stack(new_k, axis=1)
    new_cache_v = jnp.stack(new_v, axis=1)
    return (y_prompt, y_sample, new_state_ret, new_state_gdn, new_cache_k, new_cache_v)
```

```python
import functools

import numpy as np
import jax
import jax.numpy as jnp
from jax import lax
from jax.experimental import pallas as pl
from jax.experimental.pallas import tpu as pltpu

F32 = jnp.float32
BF16 = jnp.bfloat16

D_MODEL = 1024
GRID_W = 64
RET_HEADS = 4
RET_DK = 64
RET_DV = 128
GDN_HEADS = 4
GDN_DK = 128
GDN_DV = 128
CHUNK = 64
ATT_HEADS = 16
ATT_KV_HEADS = 4
ATT_HD = 64
ATT_GROUP = ATT_HEADS // ATT_KV_HEADS
ROPE_THETA = 10000.0
ROPE_PAIRS = ATT_HD // 4
FFN_HIDDEN = 2816
N_MOD = 9
EPS = 1e-6

RET_QK = RET_HEADS * RET_DK
RET_V = RET_HEADS * RET_DV
GDN_QK = GDN_HEADS * GDN_DK
GDN_V = GDN_HEADS * GDN_DV
ATT_Q = ATT_HEADS * ATT_HD
ATT_KV = ATT_KV_HEADS * ATT_HD

LANE = 128
SUBLANE = 8
V7X_VMEM_BYTES = 64 * 1024 * 1024
VMEM_LIMIT = V7X_VMEM_BYTES - 8 * 1024 * 1024

TOKEN_TILE = 512
CONV_TILE = 256
ATT_Q_TILE = 256
FFN_CHUNKS = 2
HEAD_PAD = LANE


def _params(*sem):
    return pltpu.CompilerParams(dimension_semantics=sem, vmem_limit_bytes=VMEM_LIMIT)


def _resident(shape):
    nd = len(shape)
    return pl.BlockSpec(shape, lambda *_: (0,) * nd, pipeline_mode=pl.Buffered(1))


def _dot(a, b):
    return jnp.dot(a, b, preferred_element_type=F32)


def _dot_nt(a, b):
    return lax.dot_general(a, b, (((1,), (1,)), ((), ())), preferred_element_type=F32)


def _dot_tn(a, b):
    return lax.dot_general(a, b, (((0,), (0,)), ((), ())), preferred_element_type=F32)


def _split2(a):
    hi = a.astype(BF16)
    lo = (a - hi.astype(F32)).astype(BF16)
    return hi, lo


def _dot3(a, b):
    ah, al = _split2(a)
    bh, bl = _split2(b)
    return _dot(ah, bh) + _dot(ah, bl) + _dot(al, bh)


def _silu(x):
    return x * jax.nn.sigmoid(x)


def _softplus(x):
    return jnp.maximum(x, 0.0) + jnp.log(1.0 + jnp.exp(-jnp.abs(x)))


def _adaln(x, nw, shift, scale):
    ms = jnp.mean(x * x, axis=-1, keepdims=True)
    return (x * lax.rsqrt(ms + EPS)) * nw * (1.0 + scale) + shift


def _head_rmsnorm(x, w, n_heads):
    parts = []
    for h in range(n_heads):
        xs = x[:, h * LANE:(h + 1) * LANE]
        ms = jnp.mean(xs * xs, axis=-1, keepdims=True)
        parts.append(xs * lax.rsqrt(ms + EPS) * w)
    return jnp.concatenate(parts, axis=1)


def _mod_kernel(cond_ref, w_ref, b_ref, o_ref):
    c = cond_ref[...]
    o_ref[0] = _dot(_silu(c).astype(BF16), w_ref[0].astype(BF16)) + b_ref[0]


def _modulation(cond, mod_w, mod_b):
    depth, d, n = mod_w.shape
    r = cond.shape[0]
    tn = n // 8
    out = pl.pallas_call(
        _mod_kernel,
        grid=(depth, n // tn),
        in_specs=[
            pl.BlockSpec((r, d), lambda l, j: (0, 0)),
            pl.BlockSpec((1, d, tn), lambda l, j: (l, 0, j)),
            pl.BlockSpec((1, 1, tn), lambda l, j: (l, 0, j)),
        ],
        out_specs=pl.BlockSpec((1, r, tn), lambda l, j: (l, 0, j)),
        out_shape=jax.ShapeDtypeStruct((depth, r, n), F32),
        compiler_params=_params("parallel", "parallel"),
        name="modulation",
    )(cond, mod_w, mod_b.reshape(depth, 1, n))
    return out.reshape(depth, r, N_MOD, d)


class _Layout:
    def __init__(self, batch, seq, dec_batch, dec_seq, past):
        self.batch, self.seq, self.dec_batch, self.dec_seq, self.past = batch, seq, dec_batch, dec_seq, past
        self.n_p = batch * seq
        self.n_s = dec_batch * dec_seq
        self.n = self.n_p + self.n_s
        self.tm = min(TOKEN_TILE, self.n_p, dec_seq)
        assert self.n_p % self.tm == 0 and dec_seq % self.tm == 0
        self.tiles = self.n // self.tm
        self.p_tiles = self.n_p // self.tm
        self.tiles_per_seq = dec_seq // self.tm

    def group(self, i):
        return jnp.where(i < self.p_tiles, 0, 1 + (i - self.p_tiles) // self.tiles_per_seq)

    def tok(self, width, tm=None):
        tm = tm or self.tm
        return pl.BlockSpec((tm, width), lambda i: (i, 0))

    def mod(self):
        return pl.BlockSpec((None, N_MOD, D_MODEL), lambda i: (self.group(i), 0, 0))


def _ffn_kernel(x_ref, mod_ref, nw_ref, win_ref, wout_ref, *rest, j, final):
    o_ref = rest[-1]
    x = x_ref[...]
    h = _adaln(x, nw_ref[...], mod_ref[3 * j:3 * j + 1, :], mod_ref[3 * j + 1:3 * j + 2, :]).astype(BF16)
    th = FFN_HIDDEN // FFN_CHUNKS
    acc = None
    for c in range(FFN_CHUNKS):
        a = _dot(h, win_ref[:, c * th:(c + 1) * th])
        b = _dot(h, win_ref[:, FFN_HIDDEN + c * th:FFN_HIDDEN + (c + 1) * th])
        y = _dot((_silu(a) * b).astype(BF16), wout_ref[c * th:(c + 1) * th, :])
        acc = y if acc is None else acc + y
    out = x + (0.5 * mod_ref[3 * j + 2:3 * j + 3, :]) * acc
    if final:
        fw = rest[0][...]
        ms = jnp.mean(out * out, axis=-1, keepdims=True)
        out = out * lax.rsqrt(ms + EPS) * fw
    o_ref[...] = out


def _ffn(lay, x, mods, nw, w_in, w_out, j, final_w=None):
    final = final_w is not None
    ins = [x, mods, nw.reshape(1, D_MODEL), w_in, w_out]
    specs = [lay.tok(D_MODEL), lay.mod(), _resident((1, D_MODEL)), _resident(w_in.shape), _resident(w_out.shape)]
    if final:
        ins.append(final_w.reshape(1, D_MODEL))
        specs.append(_resident((1, D_MODEL)))
    return pl.pallas_call(
        functools.partial(_ffn_kernel, j=j, final=final),
        grid=(lay.tiles,),
        in_specs=specs,
        out_specs=lay.tok(D_MODEL),
        out_shape=jax.ShapeDtypeStruct((lay.n, D_MODEL), F32),
        compiler_params=_params("parallel"),
        name="ffn_final" if final else "ffn",
    )(*ins)


EVEN_WIDTHS = (RET_HEADS * HEAD_PAD, RET_HEADS * HEAD_PAD, RET_V, RET_V, 2 * GDN_QK + GDN_V, GDN_V, LANE)


def _proj_kernel(x_ref, mod_ref, nw_ref, w_ref, *o_refs, j, widths):
    h = _adaln(x_ref[...], nw_ref[...], mod_ref[3 * j:3 * j + 1, :], mod_ref[3 * j + 1:3 * j + 2, :]).astype(BF16)
    off = 0
    for o_ref, wd in zip(o_refs, widths):
        o_ref[...] = _dot(h, w_ref[:, off:off + wd])
        off += wd


def _even_proj(lay, x, mods, nw, w):
    return pl.pallas_call(
        functools.partial(_proj_kernel, j=1, widths=EVEN_WIDTHS),
        grid=(lay.tiles,),
        in_specs=[lay.tok(D_MODEL), lay.mod(), _resident((1, D_MODEL)), _resident(w.shape)],
        out_specs=[lay.tok(wd) for wd in EVEN_WIDTHS],
        out_shape=[jax.ShapeDtypeStruct((lay.n, wd), F32) for wd in EVEN_WIDTHS],
        compiler_params=_params("parallel"),
        name="even_proj",
    )(x, mods, nw.reshape(1, D_MODEL), w)


def _even_weight(w):
    d = w.shape[0]
    o = 0
    rq = w[:, o:o + RET_QK]; o += RET_QK
    rk = w[:, o:o + RET_QK]; o += RET_QK
    rest = w[:, o:o + 2 * RET_V + 2 * GDN_QK + 2 * GDN_V]; o += 2 * RET_V + 2 * GDN_QK + 2 * GDN_V
    gab = w[:, o:]

    def pad_heads(m):
        m = m.reshape(d, RET_HEADS, RET_DK)
        return jnp.pad(m, ((0, 0), (0, 0), (0, HEAD_PAD - RET_DK))).reshape(d, RET_HEADS * HEAD_PAD)

    gab = jnp.pad(gab, ((0, 0), (0, LANE - gab.shape[1])))
    return jnp.concatenate([pad_heads(rq), pad_heads(rk * RET_DK ** -0.5), rest, gab], axis=1).astype(BF16)


def _conv_kernel(x_ref, prev_ref, next_ref, cw_ref, gab_ref, alog_ref, dt_ref,
                 cq_ref, ck_ref, cv_ref, gb_ref, gt_ref, *, n_p, seq, dec_seq):
    tc = x_ref.shape[0]
    start = pl.program_id(0) * tc
    in_prompt = start < n_p
    rel = jnp.where(in_prompt, start, start - n_p)
    slen = jnp.where(in_prompt, seq, dec_seq)
    pos = lax.rem(rel, slen)
    x = x_ref[...]
    prev_row = jnp.where(pos == 0, 0.0, prev_ref[SUBLANE - 1:SUBLANE, :])
    next_row = jnp.where(pos + tc == slen, 0.0, next_ref[0:1, :])
    row = lax.broadcasted_iota(jnp.int32, x.shape, 0)
    xm1 = jnp.where(row == 0, prev_row, pltpu.roll(x, 1, 0))
    xp1 = jnp.where(row == tc - 1, next_row, pltpu.roll(x, tc - 1, 0))
    y = _silu(xm1 * cw_ref[0:1, :] + x * cw_ref[1:2, :] + xp1 * cw_ref[2:3, :])
    for h in range(GDN_HEADS):
        q = y[:, h * LANE:(h + 1) * LANE]
        k = y[:, GDN_QK + h * LANE:GDN_QK + (h + 1) * LANE]
        cq_ref[:, h * LANE:(h + 1) * LANE] = (
            q * lax.rsqrt(jnp.sum(q * q, axis=-1, keepdims=True) + EPS) * GDN_DK ** -0.5)
        ck_ref[:, h * LANE:(h + 1) * LANE] = k * lax.rsqrt(jnp.sum(k * k, axis=-1, keepdims=True) + EPS)
    cv_ref[...] = y[:, 2 * GDN_QK:]
    g = gab_ref[...]
    lane = lax.broadcasted_iota(jnp.int32, g.shape, 1)
    gb = jnp.where(lane < 2 * GDN_HEADS, -jnp.exp(alog_ref[...]) * _softplus(g + dt_ref[...]), jax.nn.sigmoid(g))
    gb_ref[...] = gb
    gbt = gb.T
    for c in range(tc // CHUNK):
        gt_ref[c] = gbt[0:4 * GDN_HEADS, c * CHUNK:(c + 1) * CHUNK]


def _even_conv(lay, gqkv, gab, conv_w, a_log, dt_bias):
    tc = min(CONV_TILE, lay.seq, lay.dec_seq)
    assert lay.seq % tc == 0 and lay.dec_seq % tc == 0 and tc % LANE == 0
    width = gqkv.shape[1]
    nblk8 = lay.n // SUBLANE
    r = tc // SUBLANE

    def pad_row(v):
        v = v.reshape(1, -1).astype(F32)
        return jnp.pad(v, ((0, 0), (0, LANE - v.shape[1])))

    return pl.pallas_call(
        functools.partial(_conv_kernel, n_p=lay.n_p, seq=lay.seq, dec_seq=lay.dec_seq),
        grid=(lay.n // tc,),
        in_specs=[
            pl.BlockSpec((tc, width), lambda i: (i, 0)),
            pl.BlockSpec((SUBLANE, width), lambda i: (jnp.maximum(i * r - 1, 0), 0)),
            pl.BlockSpec((SUBLANE, width), lambda i: (jnp.minimum((i + 1) * r, nblk8 - 1), 0)),
            _resident(conv_w.shape),
            pl.BlockSpec((tc, LANE), lambda i: (i, 0)),
            _resident((1, LANE)),
            _resident((1, LANE)),
        ],
        out_specs=[
            pl.BlockSpec((tc, GDN_QK), lambda i: (i, 0)),
            pl.BlockSpec((tc, GDN_QK), lambda i: (i, 0)),
            pl.BlockSpec((tc, GDN_V), lambda i: (i, 0)),
            pl.BlockSpec((tc, LANE), lambda i: (i, 0)),
            pl.BlockSpec((tc // CHUNK, 4 * GDN_HEADS, CHUNK), lambda i: (i, 0, 0)),
        ],
        out_shape=[
            jax.ShapeDtypeStruct((lay.n, GDN_QK), F32),
            jax.ShapeDtypeStruct((lay.n, GDN_QK), F32),
            jax.ShapeDtypeStruct((lay.n, GDN_V), F32),
            jax.ShapeDtypeStruct((lay.n, LANE), F32),
            jax.ShapeDtypeStruct((lay.n // CHUNK, 4 * GDN_HEADS, CHUNK), F32),
        ],
        compiler_params=_params("parallel"),
        name="even_conv",
    )(gqkv, gqkv, gqkv, conv_w, gab, pad_row(a_log), pad_row(dt_bias))


N_HEADS = 4
STACK = N_HEADS * CHUNK


def _scan_kernel(fblk, bblk, first, last, s0idx, soidx, logit_ref, *refs):
    del fblk, bblk, s0idx, soidx
    views = (refs[0:8], refs[8:16])
    s0r_ref, s0g_ref = refs[16], refs[17]
    o_refs = ((refs[18], refs[20]), (refs[19], refs[21]))
    sro_ref, sgo_ref = refs[22], refs[23]
    sr_ref, sg_ref = refs[24], refs[25]
    t = pl.program_id(0)

    @pl.when(first[t] == 1)
    def _():
        sr_ref[...] = s0r_ref[...]
        sg_ref[...] = s0g_ref[...]

    @pl.when(first[t] == 2)
    def _():
        sr_ref[...] = jnp.zeros(sr_ref.shape, F32)
        sg_ref[...] = jnp.zeros(sg_ref.shape, F32)

    r4 = lax.broadcasted_iota(jnp.int32, (STACK, STACK), 0)
    c4 = lax.broadcasted_iota(jnp.int32, (STACK, STACK), 1)
    same_head = (r4 >> 6) == (c4 >> 6)
    diag = r4 == c4
    eye = jnp.where(diag, 1.0, 0.0)
    rc = lax.broadcasted_iota(jnp.int32, (CHUNK, CHUNK), 0)
    cc = lax.broadcasted_iota(jnp.int32, (CHUNK, CHUNK), 1)
    rpos = (lax.broadcasted_iota(jnp.int32, (STACK, LANE), 0) & (CHUNK - 1)).astype(F32)

    def stack(ref):
        return jnp.concatenate([ref[:, h * LANE:(h + 1) * LANE] for h in range(N_HEADS)], axis=0)

    def rows(x, h):
        return x[h * CHUNK:(h + 1) * CHUNK]

    def neg_inf_outside(keep, x):
        return jnp.where(same_head, jnp.where(keep, x, -jnp.inf), -jnp.inf)

    for d in range(2):
        rq_ref, rk_ref, rv_ref, cq_ref, ck_ref, cv_ref, gb_ref, gt_ref = views[d]
        tri4 = (r4 >= c4) if d == 0 else (r4 <= c4)
        dist4 = ((r4 - c4) if d == 0 else (c4 - r4)).astype(F32)
        spos = rpos if d == 0 else (CHUNK - 1.0) - rpos

        lg4 = jnp.concatenate(
            [-_softplus(-jnp.full((CHUNK, STACK), logit_ref[d * N_HEADS + h], F32)) for h in range(N_HEADS)], axis=0)
        dec4 = jnp.exp(neg_inf_outside(tri4, lg4 * dist4))
        lgc = lg4[:, :LANE]
        q4, k4, v4 = stack(rq_ref), stack(rk_ref), stack(rv_ref)
        v4b = v4.astype(BF16)
        scores = _dot_nt(q4.astype(BF16), k4.astype(BF16)) * dec4
        intra = _dot(scores.astype(BF16), v4b)
        qd = (q4 * jnp.exp(lgc * (spos + 1.0))).astype(BF16)
        kd = (k4 * jnp.exp(lgc * ((CHUNK - 1.0) - spos))).astype(BF16)
        for h in range(N_HEADS):
            s = sr_ref[d, h]
            o_refs[d][0][:, h * LANE:(h + 1) * LANE] = rows(intra, h) + _dot(rows(qd, h), s.astype(BF16))
            cd = jnp.exp(lgc[h * CHUNK:h * CHUNK + 1, :] * float(CHUNK))
            sr_ref[d, h] = s * cd + _dot_tn(rows(kd, h), rows(v4b, h))

        gb = gb_ref[...]
        gt = gt_ref[...]
        low = jnp.where((rc >= cc) if d == 0 else (rc <= cc), 1.0, 0.0).astype(BF16)
        upp = jnp.where((rc <= cc) if d == 0 else (rc >= cc), 1.0, 0.0).astype(BF16)
        gh, gl = _split2(gb)
        gll = (gb - gh.astype(F32) - gl.astype(F32)).astype(BF16)
        gc_col = _dot(low, gh) + _dot(low, gl) + _dot(low, gll)
        th, tl = _split2(gt)
        tll = (gt - th.astype(F32) - tl.astype(F32)).astype(BF16)
        gc_row = _dot(th, upp) + _dot(tl, upp) + _dot(tll, upp)
        last_row = CHUNK - 1 if d == 0 else 0
        idx = [d * N_HEADS + h for h in range(N_HEADS)]
        gcol = jnp.concatenate([gc_col[:, i:i + 1] for i in idx], axis=0)
        grow = jnp.concatenate([gc_row[i:i + 1, :] for i in idx], axis=1)
        glast = jnp.concatenate(
            [jnp.broadcast_to(gc_col[last_row:last_row + 1, i:i + 1], (CHUNK, 1)) for i in idx], axis=0)
        beta = jnp.concatenate([gb[:, 2 * N_HEADS + i:2 * N_HEADS + i + 1] for i in idx], axis=0)
        dmat = jnp.exp(neg_inf_outside(tri4, gcol - grow))
        eg = jnp.exp(gcol)
        q4, k4, v4 = stack(cq_ref), stack(ck_ref), stack(cv_ref)
        k4b = k4.astype(BF16)
        kb = k4 * beta
        m = jnp.where(diag, 0.0, _dot_nt(kb.astype(BF16), k4b) * dmat)
        p = -m
        tinv = eye + p
        for _ in range(5):
            p = _dot3(p, p)
            tinv = tinv + _dot3(tinv, p)
        uw = _dot3(tinv, jnp.concatenate([v4 * beta, kb * eg], axis=1))
        u, w = uw[:, :GDN_DV], uw[:, GDN_DV:]
        attn = (_dot_nt(q4.astype(BF16), k4b) * dmat).astype(BF16)
        qd = q4 * eg
        kd = (k4 * jnp.exp(glast - gcol)).astype(BF16)
        ws = []
        for h in range(N_HEADS):
            wq = jnp.concatenate([rows(w, h), rows(qd, h)], axis=0).astype(BF16)
            ws.append(_dot(wq, sg_ref[d, h].astype(BF16)))
        v_new = u - jnp.concatenate([x[:CHUNK] for x in ws], axis=0)
        v_newb = v_new.astype(BF16)
        o4 = jnp.concatenate([x[CHUNK:] for x in ws], axis=0) + _dot(attn, v_newb)
        for h in range(N_HEADS):
            o_refs[d][1][:, h * LANE:(h + 1) * LANE] = rows(o4, h)
            cd = jnp.exp(glast[h * CHUNK:h * CHUNK + 1, :])
            sg_ref[d, h] = sg_ref[d, h] * cd + _dot_tn(rows(kd, h), rows(v_newb, h))

    @pl.when(last[t] == 1)
    def _():
        sro_ref[...] = sr_ref[...]
        sgo_ref[...] = sg_ref[...]


def _scan_tables(lay):
    fblk, bblk, first, last, s0idx, soidx = [], [], [], [], [], []
    cp, cs = lay.seq // CHUNK, lay.dec_seq // CHUNK
    for b in range(lay.dec_batch):
        base = lay.n_p // CHUNK + b * cs
        for s in range(cs):
            fblk.append(base + s); bblk.append(base + cs - 1 - s)
            first.append(1 if s == 0 else 0); last.append(0); s0idx.append(b); soidx.append(0)
    for b in range(lay.batch):
        base = b * cp
        for s in range(cp):
            fblk.append(base + s); bblk.append(base + cp - 1 - s)
            first.append(2 if s == 0 else 0); last.append(1 if s == cp - 1 else 0)
            s0idx.append(lay.dec_batch - 1); soidx.append(b)
    return [jnp.asarray(np.asarray(a, np.int32)) for a in (fblk, bblk, first, last, s0idx, soidx)]


def _even_scan(lay, rq, rk, rv, cq, ck, cv, gb, gt, logit, s0_ret, s0_gdn):
    tables = _scan_tables(lay)
    steps = int(tables[0].shape[0])
    width = N_HEADS * LANE

    def view(which):
        def tok(wd):
            return pl.BlockSpec((CHUNK, wd), lambda t, *tb: (tb[which][t], 0))
        return [tok(width)] * 6 + [tok(LANE), pl.BlockSpec((None, 4 * GDN_HEADS, CHUNK), lambda t, *tb: (tb[which][t], 0, 0))]

    state_shape = (2, N_HEADS, LANE, LANE)
    s0_spec = pl.BlockSpec((None,) + state_shape, lambda t, *tb: (tb[4][t], 0, 0, 0, 0))
    so_spec = pl.BlockSpec((None,) + state_shape, lambda t, *tb: (tb[5][t], 0, 0, 0, 0))
    out_f = pl.BlockSpec((CHUNK, width), lambda t, *tb: (tb[0][t], 0))
    out_b = pl.BlockSpec((CHUNK, width), lambda t, *tb: (tb[1][t], 0))
    o_sds = jax.ShapeDtypeStruct((lay.n, width), F32)
    so_sds = jax.ShapeDtypeStruct((lay.batch,) + state_shape, F32)
    args = [rq, rk, rv, cq, ck, cv, gb, gt]
    return pl.pallas_call(
        _scan_kernel,
        grid_spec=pltpu.PrefetchScalarGridSpec(
            num_scalar_prefetch=6,
            grid=(steps,),
            in_specs=[pl.BlockSpec(memory_space=pltpu.SMEM)] + view(0) + view(1) + [s0_spec, s0_spec],
            out_specs=[out_f, out_b, out_f, out_b, so_spec, so_spec],
            scratch_shapes=[pltpu.VMEM(state_shape, F32), pltpu.VMEM(state_shape, F32)],
        ),
        out_shape=[o_sds, o_sds, o_sds, o_sds, so_sds, so_sds],
        compiler_params=_params("arbitrary"),
        name="even_scan",
    )(*tables, logit.reshape(-1).astype(F32), *args, *args, s0_ret, s0_gdn)


def _even_out_kernel(x_ref, mod_ref, rf_ref, rb_ref, gf_ref, gb_ref, rg_ref, gg_ref, rnw_ref, gnw_ref, w_ref, o_ref):
    ret = _head_rmsnorm(rf_ref[...] + rb_ref[...], rnw_ref[...], RET_HEADS) * _silu(rg_ref[...])
    gdn = _head_rmsnorm(gf_ref[...] + gb_ref[...], gnw_ref[...], GDN_HEADS) * _silu(gg_ref[...])
    y = _dot(jnp.concatenate([ret, gdn], axis=1).astype(BF16), w_ref[...])
    o_ref[...] = x_ref[...] + mod_ref[5:6, :] * y


def _even_out(lay, x, mods, orf, orb, ogf, ogb, rg, gg, rnw, gnw, w):
    return pl.pallas_call(
        _even_out_kernel,
        grid=(lay.tiles,),
        in_specs=[lay.tok(D_MODEL), lay.mod()] + [lay.tok(RET_V)] * 6
                 + [_resident((1, LANE)), _resident((1, LANE)), _resident(w.shape)],
        out_specs=lay.tok(D_MODEL),
        out_shape=jax.ShapeDtypeStruct((lay.n, D_MODEL), F32),
        compiler_params=_params("parallel"),
        name="even_out",
    )(x, mods, orf, orb, ogf, ogb, rg, gg, rnw.reshape(1, LANE), gnw.reshape(1, LANE), w)


def _odd_proj_kernel(x_ref, mod_ref, nw_ref, w_ref, qnw_ref, knw_ref, cos_ref, sin_ref,
                     q_ref, kd_ref, vr_ref, ks_ref, vs_ref, *, p_tiles):
    h = _adaln(x_ref[...], nw_ref[...], mod_ref[3:4, :], mod_ref[4:5, :]).astype(BF16)
    tm = h.shape[0]
    lane = lax.broadcasted_iota(jnp.int32, (tm, LANE), 1)
    lo = lane < ATT_HD
    first16 = (lane & (2 * ROPE_PAIRS - 1)) < ROPE_PAIRS
    cos, sin = cos_ref[...], sin_ref[...]

    def norm_rope(xs, w2):
        sq = xs * xs
        ms_lo = jnp.sum(jnp.where(lo, sq, 0.0), axis=-1, keepdims=True) * (1.0 / ATT_HD)
        ms_hi = jnp.sum(jnp.where(lo, 0.0, sq), axis=-1, keepdims=True) * (1.0 / ATT_HD)
        xn = xs * jnp.where(lo, lax.rsqrt(ms_lo + EPS), lax.rsqrt(ms_hi + EPS)) * w2
        partner = jnp.where(first16, pltpu.roll(xn, LANE - ROPE_PAIRS, 1), pltpu.roll(xn, ROPE_PAIRS, 1))
        return xn * cos + partner * sin

    def both_halves(xs, j):
        swapped = pltpu.roll(xs, ATT_HD, 1)
        return jnp.where(lo, xs, swapped) if j % 2 == 0 else jnp.where(lo, swapped, xs)

    for p in range(ATT_Q // LANE):
        qs = norm_rope(_dot(h, w_ref[:, p * LANE:(p + 1) * LANE]), qnw_ref[...])
        q_ref[:, p * LANE:(p + 1) * LANE] = (qs * ATT_HD ** -0.5).astype(BF16)
    is_prompt = pl.program_id(0) < p_tiles
    for p in range(ATT_KV // LANE):
        ks = norm_rope(_dot(h, w_ref[:, ATT_Q + p * LANE:ATT_Q + (p + 1) * LANE]), knw_ref[...])
        vs = _dot(h, w_ref[:, ATT_Q + ATT_KV + p * LANE:ATT_Q + ATT_KV + (p + 1) * LANE])
        for j in (2 * p, 2 * p + 1):
            kd_ref[j] = both_halves(ks, j).astype(BF16)
            vd = both_halves(vs, j).astype(BF16)
            vr_ref[j] = jnp.concatenate([vd, vd], axis=1)

        @pl.when(is_prompt)
        def _():
            ks_ref[:, p * LANE:(p + 1) * LANE] = ks
            vs_ref[:, p * LANE:(p + 1) * LANE] = vs


def _rope_tables(lay):
    t = jnp.arange(lay.dec_seq)
    inv = ROPE_THETA ** (-jnp.arange(ROPE_PAIRS, dtype=F32) / ROPE_PAIRS)
    ar = (t // GRID_W).astype(F32)[:, None] * inv
    ac = (t % GRID_W).astype(F32)[:, None] * inv
    cos = jnp.concatenate([jnp.cos(ar), jnp.cos(ar), jnp.cos(ac), jnp.cos(ac)], axis=1)
    sin = jnp.concatenate([-jnp.sin(ar), jnp.sin(ar), -jnp.sin(ac), jnp.sin(ac)], axis=1)
    cos = jnp.concatenate([jnp.tile(cos, (1, 2)), jnp.ones((lay.tm, LANE), F32)], axis=0)
    sin = jnp.concatenate([jnp.tile(sin, (1, 2)), jnp.zeros((lay.tm, LANE), F32)], axis=0)
    return cos, sin


def _odd_proj(lay, x, mods, nw, w, qnw, knw):
    cos, sin = _rope_tables(lay)

    def table_block(i):
        return jnp.where(i < lay.p_tiles, lay.tiles_per_seq, lax.rem(jnp.maximum(i - lay.p_tiles, 0), lay.tiles_per_seq))

    tab = pl.BlockSpec((lay.tm, LANE), lambda i: (table_block(i), 0))
    std = pl.BlockSpec((lay.tm, ATT_KV), lambda i: (jnp.minimum(i, lay.p_tiles - 1), 0))
    tm = lay.tm
    return pl.pallas_call(
        functools.partial(_odd_proj_kernel, p_tiles=lay.p_tiles),
        grid=(lay.tiles,),
        in_specs=[lay.tok(D_MODEL), lay.mod(), _resident((1, D_MODEL)), _resident(w.shape),
                  _resident((1, LANE)), _resident((1, LANE)), tab, tab],
        out_specs=[
            lay.tok(ATT_Q),
            pl.BlockSpec((ATT_KV_HEADS, tm, LANE), lambda i: (0, i, 0)),
            pl.BlockSpec((ATT_KV_HEADS, tm, 2 * LANE), lambda i: (0, i, 0)),
            std, std,
        ],
        out_shape=[
            jax.ShapeDtypeStruct((lay.n, ATT_Q), BF16),
            jax.ShapeDtypeStruct((ATT_KV_HEADS, lay.n, LANE), BF16),
            jax.ShapeDtypeStruct((ATT_KV_HEADS, lay.n, 2 * LANE), BF16),
            jax.ShapeDtypeStruct((lay.n_p, ATT_KV), F32),
            jax.ShapeDtypeStruct((lay.n_p, ATT_KV), F32),
        ],
        compiler_params=_params("arbitrary"),
        name="odd_proj",
    )(x, mods, nw.reshape(1, D_MODEL), w, jnp.tile(qnw.reshape(1, ATT_HD), (1, 2)),
      jnp.tile(knw.reshape(1, ATT_HD), (1, 2)), cos, sin)


def _attn_kernel(*refs, cached):
    if cached:
        q_ref, kn_ref, vn_ref, kc_ref, vc_ref, _, o_ref = refs
    else:
        q_ref, kn_ref, vn_ref, o_ref = refs
    tq = q_ref.shape[0]
    lo = lax.broadcasted_iota(jnp.int32, (tq, LANE), 1) < ATT_HD
    head_of_lane = lax.broadcasted_iota(jnp.int32, (tq, ATT_GROUP * ATT_HD), 1) >> 6
    kn, vn = kn_ref[...], vn_ref[...]
    acc = jnp.zeros((tq, ATT_GROUP * ATT_HD), F32)
    for g in range(ATT_GROUP):
        q2 = q_ref[:, (g // 2) * LANE:(g // 2 + 1) * LANE]
        qm = jnp.where(lo, q2, jnp.zeros_like(q2)) if g % 2 == 0 else jnp.where(lo, jnp.zeros_like(q2), q2)
        s_n = _dot_nt(qm, kn)
        m = jnp.max(s_n, axis=-1, keepdims=True)
        if cached:
            s_c = _dot_nt(qm, kc_ref[...])
            m = jnp.maximum(m, jnp.max(s_c, axis=-1, keepdims=True))
        p_n = jnp.exp(s_n - m)
        l = jnp.sum(p_n, axis=-1, keepdims=True)
        o = _dot(p_n.astype(BF16), vn)
        if cached:
            p_c = jnp.exp(s_c - m)
            l = l + jnp.sum(p_c, axis=-1, keepdims=True)
            o = o + _dot(p_c.astype(BF16), vc_ref[...])
        acc = jnp.where(head_of_lane == g, o / l, acc)
    o_ref[...] = acc


def _attention(lay, q, kd, vr, kc, vc):
    gw = ATT_GROUP * ATT_HD
    o_sds = jax.ShapeDtypeStruct((lay.n, ATT_Q), F32)

    tq = min(ATT_Q_TILE, lay.seq)
    nq = lay.seq // tq
    out = pl.pallas_call(
        functools.partial(_attn_kernel, cached=False),
        grid=(lay.batch, ATT_KV_HEADS, nq),
        in_specs=[
            pl.BlockSpec((tq, gw), lambda b, j, i: (b * nq + i, j)),
            pl.BlockSpec((None, lay.seq, LANE), lambda b, j, i: (j, b, 0)),
            pl.BlockSpec((None, lay.seq, 2 * LANE), lambda b, j, i: (j, b, 0)),
        ],
        out_specs=pl.BlockSpec((tq, gw), lambda b, j, i: (b * nq + i, j)),
        out_shape=o_sds,
        compiler_params=_params("parallel", "parallel", "parallel"),
        name="attn_context",
    )(q, kd, vr)

    tq = min(ATT_Q_TILE, lay.dec_seq)
    nq = lay.dec_seq // tq
    qbase = lay.n_p // tq
    kbase = lay.n_p // lay.dec_seq
    assert lay.n_p % lay.dec_seq == 0
    return pl.pallas_call(
        functools.partial(_attn_kernel, cached=True),
        grid=(lay.dec_batch, ATT_KV_HEADS, nq),
        in_specs=[
            pl.BlockSpec((tq, gw), lambda b, j, i: (qbase + b * nq + i, j)),
            pl.BlockSpec((None, lay.dec_seq, LANE), lambda b, j, i: (j, kbase + b, 0)),
            pl.BlockSpec((None, lay.dec_seq, 2 * LANE), lambda b, j, i: (j, kbase + b, 0)),
            pl.BlockSpec((None, None, lay.past, LANE), lambda b, j, i: (b, j, 0, 0)),
            pl.BlockSpec((None, None, lay.past, 2 * LANE), lambda b, j, i: (b, j, 0, 0)),
            pl.BlockSpec(memory_space=pl.ANY),
        ],
        out_specs=pl.BlockSpec((tq, gw), lambda b, j, i: (qbase + b * nq + i, j)),
        out_shape=o_sds,
        input_output_aliases={5: 0},
        compiler_params=_params("parallel", "parallel", "parallel"),
        name="attn_latent",
    )(q, kd, vr, kc, vc, out)


def _odd_out_kernel(x_ref, mod_ref, a_ref, w_ref, o_ref):
    o_ref[...] = x_ref[...] + mod_ref[5:6, :] * _dot(a_ref[...].astype(BF16), w_ref[...])


def _odd_out(lay, x, mods, a, w):
    return pl.pallas_call(
        _odd_out_kernel,
        grid=(lay.tiles,),
        in_specs=[lay.tok(D_MODEL), lay.mod(), lay.tok(ATT_Q), _resident(w.shape)],
        out_specs=lay.tok(D_MODEL),
        out_shape=jax.ShapeDtypeStruct((lay.n, D_MODEL), F32),
        compiler_params=_params("parallel"),
        name="odd_out",
    )(x, mods, a, w)


def _pad_state(s):
    return jnp.pad(s, [(0, 0)] * (s.ndim - 2) + [(0, LANE - s.shape[-2]), (0, 0)])


def _dup_heads(c, reps):
    return jnp.tile(c.transpose(0, 2, 1, 3), (1, 1, 1, reps)).astype(BF16)


def kernel(x_prompt, x_sample, state_ret, state_gdn, cache_k, cache_v, c, c_ctx,
           mod_w, mod_b, norm_w, ffn_w_in, ffn_w_out, even_w_in, even_w_out,
           ret_decay_logit, ret_norm_w, gdn_conv_w, gdn_A_log, gdn_dt_bias, gdn_norm_w,
           odd_w_in, odd_w_out, q_norm_w, k_norm_w, final_norm_w):
    batch, seq, d = x_prompt.shape
    dec_batch, dec_seq, _ = x_sample.shape
    depth = mod_w.shape[0]
    lay = _Layout(batch, seq, dec_batch, dec_seq, cache_k.shape[2])

    x = jnp.concatenate([x_prompt.reshape(lay.n_p, d), x_sample.reshape(lay.n_s, d)], axis=0)
    n_cond = -(-(1 + dec_batch) // (2 * SUBLANE)) * (2 * SUBLANE)
    cond = jnp.zeros((n_cond, d), F32).at[0].set(c_ctx).at[1:1 + dec_batch].set(c)
    mods = _modulation(cond, mod_w, mod_b)

    new_ret, new_gdn, new_k, new_v = [], [], [], []
    for l in range(depth):
        m = mods[l]
        last = l == depth - 1
        x = _ffn(lay, x, m, norm_w[l, 0], ffn_w_in[l, 0].astype(BF16), ffn_w_out[l, 0].astype(BF16), 0)
        if l % 2 == 0:
            e = l // 2
            rq, rk, rv, rg, gqkv, gg, gab = _even_proj(lay, x, m, norm_w[l, 1], _even_weight(even_w_in[e]))
            cq, ck, cv, gb, gt = _even_conv(lay, gqkv, gab, gdn_conv_w[e], gdn_A_log[e], gdn_dt_bias[e])
            orf, orb, ogf, ogb, sr, sg = _even_scan(
                lay, rq, rk, rv, cq, ck, cv, gb, gt, ret_decay_logit[e],
                _pad_state(state_ret[:, e]), state_gdn[:, e])
            new_ret.append(sr[:, :, :, :RET_DK, :])
            new_gdn.append(sg)
            x = _even_out(lay, x, m, orf, orb, ogf, ogb, rg, gg, ret_norm_w[e], gdn_norm_w[e],
                          even_w_out[e].astype(BF16))
        else:
            o = l // 2
            q, kd, vr, ks, vs = _odd_proj(lay, x, m, norm_w[l, 1], odd_w_in[o].astype(BF16), q_norm_w[o], k_norm_w[o])
            a = _attention(lay, q, kd, vr, _dup_heads(cache_k[:, o], 2), _dup_heads(cache_v[:, o], 4))
            new_k.append(ks.reshape(batch, seq, ATT_KV_HEADS, ATT_HD))
            new_v.append(vs.reshape(batch, seq, ATT_KV_HEADS, ATT_HD))
            x = _odd_out(lay, x, m, a, odd_w_out[o].astype(BF16))
        x = _ffn(lay, x, m, norm_w[l, 2], ffn_w_in[l, 1].astype(BF16), ffn_w_out[l, 1].astype(BF16), 2,
                 final_w=final_norm_w if last else None)

    y_prompt = x[:lay.n_p].reshape(batch, seq, d)
    y_sample = x[lay.n_p:].reshape(dec_batch, dec_seq, d)
    return (y_prompt, y_sample, jnp.stack(new_ret, axis=1), jnp.stack(new_gdn, axis=1),
            jnp.stack(new_k, axis=1), jnp.stack(new_v, axis=1))
```

```python
import functools

import numpy as np
import jax
import jax.numpy as jnp
from jax import lax
from jax.experimental import pallas as pl
from jax.experimental.pallas import tpu as pltpu

F32 = jnp.float32
BF16 = jnp.bfloat16

D_MODEL = 1024
GRID_W = 64
RET_HEADS = 4
RET_DK = 64
RET_DV = 128
GDN_HEADS = 4
GDN_DK = 128
GDN_DV = 128
CHUNK = 64
ATT_HEADS = 16
ATT_KV_HEADS = 4
ATT_HD = 64
ATT_GROUP = ATT_HEADS // ATT_KV_HEADS
ROPE_THETA = 10000.0
ROPE_PAIRS = ATT_HD // 4
FFN_HIDDEN = 2816
N_MOD = 9
EPS = 1e-6

RET_QK = RET_HEADS * RET_DK
RET_V = RET_HEADS * RET_DV
GDN_QK = GDN_HEADS * GDN_DK
GDN_V = GDN_HEADS * GDN_DV
ATT_Q = ATT_HEADS * ATT_HD
ATT_KV = ATT_KV_HEADS * ATT_HD

LANE = 128
SUBLANE = 8
V7X_VMEM_BYTES = 64 * 1024 * 1024
VMEM_LIMIT = V7X_VMEM_BYTES - 8 * 1024 * 1024

TOKEN_TILE = 512
CONV_TILE = 256
ATT_Q_TILE = 256
KEY_CHUNK = 256
Q_SCALE = ATT_HD ** -0.5 * float(np.log2(np.e))
FFN_CHUNKS = 2
HEAD_PAD = LANE
PREP_CHUNKS = 2


def _params(*sem):
    return pltpu.CompilerParams(dimension_semantics=sem, vmem_limit_bytes=VMEM_LIMIT)


def _resident(shape):
    nd = len(shape)
    return pl.BlockSpec(shape, lambda *_: (0,) * nd, pipeline_mode=pl.Buffered(1))


def _dot(a, b):
    return jnp.dot(a, b, preferred_element_type=F32)


def _dot_nt(a, b):
    return lax.dot_general(a, b, (((1,), (1,)), ((), ())), preferred_element_type=F32)


def _dot_tn(a, b):
    return lax.dot_general(a, b, (((0,), (0,)), ((), ())), preferred_element_type=F32)


def _split2(a):
    hi = a.astype(BF16)
    lo = (a - hi.astype(F32)).astype(BF16)
    return hi, lo


def _dot3(a, b):
    ah, al = _split2(a)
    bh, bl = _split2(b)
    return _dot(ah, bh) + _dot(ah, bl) + _dot(al, bh)


def _silu(x):
    return x * jax.nn.sigmoid(x)


def _softplus(x):
    return jnp.maximum(x, 0.0) + jnp.log(1.0 + jnp.exp(-jnp.abs(x)))


def _adaln(x, nw, shift, scale):
    ms = jnp.mean(x * x, axis=-1, keepdims=True)
    return (x * lax.rsqrt(ms + EPS)) * nw * (1.0 + scale) + shift


def _head_rmsnorm(x, w, n_heads):
    parts = []
    for h in range(n_heads):
        xs = x[:, h * LANE:(h + 1) * LANE]
        ms = jnp.mean(xs * xs, axis=-1, keepdims=True)
        parts.append(xs * lax.rsqrt(ms + EPS) * w)
    return jnp.concatenate(parts, axis=1)


def _mod_kernel(cond_ref, w_ref, b_ref, o_ref):
    c = cond_ref[...]
    o_ref[0] = _dot(_silu(c).astype(BF16), w_ref[0].astype(BF16)) + b_ref[0]


def _modulation(cond, mod_w, mod_b):
    depth, d, n = mod_w.shape
    r = cond.shape[0]
    tn = n // 8
    out = pl.pallas_call(
        _mod_kernel,
        grid=(depth, n // tn),
        in_specs=[
            pl.BlockSpec((r, d), lambda l, j: (0, 0)),
            pl.BlockSpec((1, d, tn), lambda l, j: (l, 0, j)),
            pl.BlockSpec((1, 1, tn), lambda l, j: (l, 0, j)),
        ],
        out_specs=pl.BlockSpec((1, r, tn), lambda l, j: (l, 0, j)),
        out_shape=jax.ShapeDtypeStruct((depth, r, n), F32),
        compiler_params=_params("parallel", "parallel"),
        name="modulation",
    )(cond, mod_w, mod_b.reshape(depth, 1, n))
    return out.reshape(depth, r, N_MOD, d)


class _Layout:
    def __init__(self, batch, seq, dec_batch, dec_seq, past):
        self.batch, self.seq, self.dec_batch, self.dec_seq, self.past = batch, seq, dec_batch, dec_seq, past
        self.n_p = batch * seq
        self.n_s = dec_batch * dec_seq
        self.n = self.n_p + self.n_s
        self.tm = min(TOKEN_TILE, self.n_p, dec_seq)
        assert self.n_p % self.tm == 0 and dec_seq % self.tm == 0
        self.tiles = self.n // self.tm
        self.p_tiles = self.n_p // self.tm
        self.tiles_per_seq = dec_seq // self.tm

    def group(self, i):
        return jnp.where(i < self.p_tiles, 0, 1 + (i - self.p_tiles) // self.tiles_per_seq)

    def tok(self, width, tm=None):
        tm = tm or self.tm
        return pl.BlockSpec((tm, width), lambda i: (i, 0))

    def mod(self):
        return pl.BlockSpec((None, N_MOD, D_MODEL), lambda i: (self.group(i), 0, 0))


def _ffn_kernel(x_ref, mod_ref, nw_ref, win_ref, wout_ref, *rest, j, final):
    o_ref = rest[-1]
    x = x_ref[...]
    h = _adaln(x, nw_ref[...], mod_ref[3 * j:3 * j + 1, :], mod_ref[3 * j + 1:3 * j + 2, :]).astype(BF16)
    th = FFN_HIDDEN // FFN_CHUNKS
    acc = None
    for c in range(FFN_CHUNKS):
        a = _dot(h, win_ref[:, c * th:(c + 1) * th])
        b = _dot(h, win_ref[:, FFN_HIDDEN + c * th:FFN_HIDDEN + (c + 1) * th])
        y = _dot((_silu(a) * b).astype(BF16), wout_ref[c * th:(c + 1) * th, :])
        acc = y if acc is None else acc + y
    out = x + (0.5 * mod_ref[3 * j + 2:3 * j + 3, :]) * acc
    if final:
        fw = rest[0][...]
        ms = jnp.mean(out * out, axis=-1, keepdims=True)
        out = out * lax.rsqrt(ms + EPS) * fw
    o_ref[...] = out


def _ffn(lay, x, mods, nw, w_in, w_out, j, final_w=None):
    final = final_w is not None
    ins = [x, mods, nw.reshape(1, D_MODEL), w_in, w_out]
    specs = [lay.tok(D_MODEL), lay.mod(), _resident((1, D_MODEL)), _resident(w_in.shape), _resident(w_out.shape)]
    if final:
        ins.append(final_w.reshape(1, D_MODEL))
        specs.append(_resident((1, D_MODEL)))
    return pl.pallas_call(
        functools.partial(_ffn_kernel, j=j, final=final),
        grid=(lay.tiles,),
        in_specs=specs,
        out_specs=lay.tok(D_MODEL),
        out_shape=jax.ShapeDtypeStruct((lay.n, D_MODEL), F32),
        compiler_params=_params("parallel"),
        name="ffn_final" if final else "ffn",
    )(*ins)


EVEN_WIDTHS = (RET_HEADS * HEAD_PAD, RET_HEADS * HEAD_PAD, RET_V, RET_V, 2 * GDN_QK + GDN_V, GDN_V, LANE)


def _proj_kernel(x_ref, mod_ref, nw_ref, w_ref, *o_refs, j, widths):
    h = _adaln(x_ref[...], nw_ref[...], mod_ref[3 * j:3 * j + 1, :], mod_ref[3 * j + 1:3 * j + 2, :]).astype(BF16)
    off = 0
    for o_ref, wd in zip(o_refs, widths):
        o_ref[...] = _dot(h, w_ref[:, off:off + wd])
        off += wd


def _even_proj(lay, x, mods, nw, w):
    return pl.pallas_call(
        functools.partial(_proj_kernel, j=1, widths=EVEN_WIDTHS),
        grid=(lay.tiles,),
        in_specs=[lay.tok(D_MODEL), lay.mod(), _resident((1, D_MODEL)), _resident(w.shape)],
        out_specs=[lay.tok(wd) for wd in EVEN_WIDTHS],
        out_shape=[jax.ShapeDtypeStruct((lay.n, wd), F32) for wd in EVEN_WIDTHS],
        compiler_params=_params("parallel"),
        name="even_proj",
    )(x, mods, nw.reshape(1, D_MODEL), w)


def _even_weight(w):
    d = w.shape[0]
    o = 0
    rq = w[:, o:o + RET_QK]; o += RET_QK
    rk = w[:, o:o + RET_QK]; o += RET_QK
    rest = w[:, o:o + 2 * RET_V + 2 * GDN_QK + 2 * GDN_V]; o += 2 * RET_V + 2 * GDN_QK + 2 * GDN_V
    gab = w[:, o:]

    def pad_heads(m):
        m = m.reshape(d, RET_HEADS, RET_DK)
        return jnp.pad(m, ((0, 0), (0, 0), (0, HEAD_PAD - RET_DK))).reshape(d, RET_HEADS * HEAD_PAD)

    gab = jnp.pad(gab, ((0, 0), (0, LANE - gab.shape[1])))
    return jnp.concatenate([pad_heads(rq), pad_heads(rk * RET_DK ** -0.5), rest, gab], axis=1).astype(BF16)


def _conv_kernel(x_ref, prev_ref, next_ref, cw_ref, gab_ref, alog_ref, dt_ref,
                 cq_ref, ck_ref, cv_ref, gb_ref, gt_ref, *, n_p, seq, dec_seq):
    tc = x_ref.shape[0]
    start = pl.program_id(0) * tc
    in_prompt = start < n_p
    rel = jnp.where(in_prompt, start, start - n_p)
    slen = jnp.where(in_prompt, seq, dec_seq)
    pos = lax.rem(rel, slen)
    x = x_ref[...]
    prev_row = jnp.where(pos == 0, 0.0, prev_ref[SUBLANE - 1:SUBLANE, :])
    next_row = jnp.where(pos + tc == slen, 0.0, next_ref[0:1, :])
    row = lax.broadcasted_iota(jnp.int32, x.shape, 0)
    xm1 = jnp.where(row == 0, prev_row, pltpu.roll(x, 1, 0))
    xp1 = jnp.where(row == tc - 1, next_row, pltpu.roll(x, tc - 1, 0))
    y = _silu(xm1 * cw_ref[0:1, :] + x * cw_ref[1:2, :] + xp1 * cw_ref[2:3, :])
    for h in range(GDN_HEADS):
        q = y[:, h * LANE:(h + 1) * LANE]
        k = y[:, GDN_QK + h * LANE:GDN_QK + (h + 1) * LANE]
        cq_ref[:, h * LANE:(h + 1) * LANE] = (
            q * lax.rsqrt(jnp.sum(q * q, axis=-1, keepdims=True) + EPS) * GDN_DK ** -0.5)
        ck_ref[:, h * LANE:(h + 1) * LANE] = k * lax.rsqrt(jnp.sum(k * k, axis=-1, keepdims=True) + EPS)
    cv_ref[...] = y[:, 2 * GDN_QK:]
    g = gab_ref[...]
    lane = lax.broadcasted_iota(jnp.int32, g.shape, 1)
    gb = jnp.where(lane < 2 * GDN_HEADS, -jnp.exp(alog_ref[...]) * _softplus(g + dt_ref[...]), jax.nn.sigmoid(g))
    gb_ref[...] = gb
    gbt = gb.T
    for c in range(tc // CHUNK):
        gt_ref[c] = gbt[0:4 * GDN_HEADS, c * CHUNK:(c + 1) * CHUNK]


def _even_conv(lay, gqkv, gab, conv_w, a_log, dt_bias):
    tc = min(CONV_TILE, lay.seq, lay.dec_seq)
    assert lay.seq % tc == 0 and lay.dec_seq % tc == 0 and tc % LANE == 0
    width = gqkv.shape[1]
    nblk8 = lay.n // SUBLANE
    r = tc // SUBLANE

    def pad_row(v):
        v = v.reshape(1, -1).astype(F32)
        return jnp.pad(v, ((0, 0), (0, LANE - v.shape[1])))

    return pl.pallas_call(
        functools.partial(_conv_kernel, n_p=lay.n_p, seq=lay.seq, dec_seq=lay.dec_seq),
        grid=(lay.n // tc,),
        in_specs=[
            pl.BlockSpec((tc, width), lambda i: (i, 0)),
            pl.BlockSpec((SUBLANE, width), lambda i: (jnp.maximum(i * r - 1, 0), 0)),
            pl.BlockSpec((SUBLANE, width), lambda i: (jnp.minimum((i + 1) * r, nblk8 - 1), 0)),
            _resident(conv_w.shape),
            pl.BlockSpec((tc, LANE), lambda i: (i, 0)),
            _resident((1, LANE)),
            _resident((1, LANE)),
        ],
        out_specs=[
            pl.BlockSpec((tc, GDN_QK), lambda i: (i, 0)),
            pl.BlockSpec((tc, GDN_QK), lambda i: (i, 0)),
            pl.BlockSpec((tc, GDN_V), lambda i: (i, 0)),
            pl.BlockSpec((tc, LANE), lambda i: (i, 0)),
            pl.BlockSpec((tc // CHUNK, 4 * GDN_HEADS, CHUNK), lambda i: (i, 0, 0)),
        ],
        out_shape=[
            jax.ShapeDtypeStruct((lay.n, GDN_QK), F32),
            jax.ShapeDtypeStruct((lay.n, GDN_QK), F32),
            jax.ShapeDtypeStruct((lay.n, GDN_V), F32),
            jax.ShapeDtypeStruct((lay.n, LANE), F32),
            jax.ShapeDtypeStruct((lay.n // CHUNK, 4 * GDN_HEADS, CHUNK), F32),
        ],
        compiler_params=_params("parallel"),
        name="even_conv",
    )(gqkv, gqkv, gqkv, conv_w, gab, pad_row(a_log), pad_row(dt_bias))


N_HEADS = 4
STACK = N_HEADS * CHUNK


def _prep_kernel(logit_ref, rq_ref, rk_ref, rv_ref, cq_ref, ck_ref, cv_ref, gb_ref, gt_ref,
                 rin_ref, rqd_ref, rkd_ref, rvb_ref, gu_ref, gwq_ref, gat_ref, gkd_ref, gcd_ref, *, chunks):
    wr = lax.broadcasted_iota(jnp.int32, (CHUNK, STACK), 0)
    wcol = lax.broadcasted_iota(jnp.int32, (CHUNK, STACK), 1)
    wc = wcol & (CHUNK - 1)
    whead = wcol >> 6
    r4 = lax.broadcasted_iota(jnp.int32, (STACK, STACK), 0)
    c4 = lax.broadcasted_iota(jnp.int32, (STACK, STACK), 1)
    same_head = (r4 >> 6) == (c4 >> 6)
    rc = lax.broadcasted_iota(jnp.int32, (CHUNK, CHUNK), 0)
    cc = lax.broadcasted_iota(jnp.int32, (CHUNK, CHUNK), 1)
    rpos = (lax.broadcasted_iota(jnp.int32, (STACK, LANE), 0) & (CHUNK - 1)).astype(F32)
    eye_w = jnp.where(wr == wc, 1.0, 0.0)

    def rows(x, h):
        return x[h * CHUNK:(h + 1) * CHUNK]

    def by_head(parts):
        out = parts[N_HEADS - 1]
        for h in range(N_HEADS - 2, -1, -1):
            out = jnp.where(whead == h, parts[h], out)
        return out

    def block_diag(w):
        full = jnp.where(same_head, jnp.concatenate([w] * N_HEADS, axis=0), 0.0)
        return _split2(full)

    def wide_times(a, bh, bl):
        ah, al = _split2(a)
        hh = _dot(jnp.concatenate([ah, al], axis=0), bh)
        return hh[:CHUNK] + hh[CHUNK:] + _dot(ah, bl)

    ret_const = []
    for d in range(2):
        lg4 = jnp.concatenate(
            [-_softplus(-jnp.full((CHUNK, STACK), logit_ref[d * N_HEADS + h], F32)) for h in range(N_HEADS)], axis=0)
        tri4 = (r4 >= c4) if d == 0 else (r4 <= c4)
        dist4 = ((r4 - c4) if d == 0 else (c4 - r4)).astype(F32)
        dec4 = jnp.exp(jnp.where(same_head, jnp.where(tri4, lg4 * dist4, -jnp.inf), -jnp.inf))
        spos = rpos if d == 0 else (CHUNK - 1.0) - rpos
        lgc = lg4[:, :LANE]
        ret_const.append((dec4, jnp.exp(lgc * (spos + 1.0)), jnp.exp(lgc * ((CHUNK - 1.0) - spos))))

    for n in range(chunks):
        tok = slice(n * CHUNK, (n + 1) * CHUNK)

        def stack(ref):
            return jnp.concatenate([ref[tok, h * LANE:(h + 1) * LANE] for h in range(N_HEADS)], axis=0)

        q4, k4, v4 = stack(rq_ref), stack(rk_ref), stack(rv_ref)
        q4b, k4b, v4b = q4.astype(BF16), k4.astype(BF16), v4.astype(BF16)
        rvb_ref[n] = v4b
        qk_ret = _dot_nt(q4b, k4b)
        gq4, gk4, gv4 = stack(cq_ref), stack(ck_ref), stack(cv_ref)
        gq4b, gk4b = gq4.astype(BF16), gk4.astype(BF16)
        gb = gb_ref[tok, :]
        gt = gt_ref[n]
        gh, gl = _split2(gb)
        gll = (gb - gh.astype(F32) - gl.astype(F32)).astype(BF16)
        th, tl = _split2(gt)
        tll = (gt - th.astype(F32) - tl.astype(F32)).astype(BF16)

        for d in range(2):
            dec4, qdec, kdec = ret_const[d]
            rin_ref[n, d] = _dot((qk_ret * dec4).astype(BF16), v4b)
            rqd_ref[n, d] = (q4 * qdec).astype(BF16)
            rkd_ref[n, d] = (k4 * kdec).astype(BF16)

            fwd = d == 0
            low = jnp.where((rc >= cc) if fwd else (rc <= cc), 1.0, 0.0).astype(BF16)
            upp_w = jnp.where((wr <= wc) if fwd else (wr >= wc), 1.0, 0.0).astype(BF16)
            tri_w = (wr >= wc) if fwd else (wr <= wc)
            tri_c = (rc >= cc) if fwd else (rc <= cc)
            gc_col = _dot(low, gh) + _dot(low, gl) + _dot(low, gll)
            gc_row = _dot(th, upp_w) + _dot(tl, upp_w) + _dot(tll, upp_w)
            last_row = CHUNK - 1 if fwd else 0
            idx = [d * N_HEADS + h for h in range(N_HEADS)]
            gcol = [gc_col[:, i:i + 1] for i in idx]
            glast = [gc_col[last_row:last_row + 1, i:i + 1] for i in idx]
            beta = [gb[:, 2 * N_HEADS + i:2 * N_HEADS + i + 1] for i in idx]
            col4 = lambda parts: jnp.concatenate([jnp.broadcast_to(p, (CHUNK, 1)) for p in parts], axis=0)
            gcol4, glast4, beta4 = col4(gcol), col4(glast), col4(beta)
            eg4 = jnp.exp(gcol4)
            kb4 = gk4 * beta4
            gcol_w = by_head([jnp.broadcast_to(g, (CHUNK, STACK)) for g in gcol])
            grow_w = by_head([jnp.broadcast_to(gc_row[i:i + 1, :], (CHUNK, STACK)) for i in idx])
            dmat_w = jnp.exp(jnp.where(tri_w, gcol_w - grow_w, -jnp.inf))
            kk = _dot_nt(kb4.astype(BF16), gk4b)
            kk_w = by_head([rows(kk, h) for h in range(N_HEADS)])
            p = jnp.where(wr == wc, 0.0, -(kk_w * dmat_w))
            tinv = eye_w + p
            for _ in range(5):
                ph, pl_ = block_diag(p)
                p = wide_times(p, ph, pl_)
                ph, pl_ = block_diag(p)
                tinv = tinv + wide_times(tinv, ph, pl_)
            th_bd, tl_bd = block_diag(tinv)
            rh, rl = _split2(jnp.concatenate([gv4 * beta4, kb4 * eg4], axis=1))
            uw = _dot(th_bd, rh) + _dot(th_bd, rl) + _dot(tl_bd, rh)
            gu_ref[n, d] = uw[:, :GDN_DV]
            w4 = uw[:, GDN_DV:].astype(BF16)
            qd4 = (gq4 * eg4).astype(BF16)
            gkd_ref[n, d] = (gk4 * jnp.exp(glast4 - gcol4)).astype(BF16)
            for h in range(N_HEADS):
                gwq_ref[n, d, h] = jnp.concatenate([rows(w4, h), rows(qd4, h)], axis=0)
                dm = jnp.exp(jnp.where(tri_c, gcol[h] - gc_row[idx[h]:idx[h] + 1, :CHUNK], -jnp.inf))
                gat_ref[n, d, h] = (_dot_nt(rows(gq4b, h), rows(gk4b, h)) * dm).astype(BF16)
            gcd_ref[n, d] = jnp.concatenate(
                [jnp.broadcast_to(jnp.exp(g), (1, LANE)) for g in glast] + [jnp.zeros((SUBLANE - N_HEADS, LANE), F32)],
                axis=0)


def _scan_kernel(fblk, bblk, first, last, s0idx, soidx, logit_ref, *refs):
    del fblk, bblk, s0idx, soidx
    views = (refs[0:9], refs[9:18])
    s0r_ref, s0g_ref = refs[18], refs[19]
    o_refs = ((refs[20], refs[22]), (refs[21], refs[23]))
    sro_ref, sgo_ref = refs[24], refs[25]
    sr_ref, sg_ref = refs[26], refs[27]
    t = pl.program_id(0)

    @pl.when(first[t] == 1)
    def _():
        sr_ref[...] = s0r_ref[...]
        sg_ref[...] = s0g_ref[...]

    @pl.when(first[t] == 2)
    def _():
        sr_ref[...] = jnp.zeros(sr_ref.shape, F32)
        sg_ref[...] = jnp.zeros(sg_ref.shape, F32)

    def rows(ref, h):
        return ref[h * CHUNK:(h + 1) * CHUNK, :]

    for d in range(2):
        rin_ref, rqd_ref, rkd_ref, rvb_ref, gu_ref, gwq_ref, gat_ref, gkd_ref, gcd_ref = views[d]
        for h in range(N_HEADS):
            s = sr_ref[d, h]
            o_refs[d][0][:, h * LANE:(h + 1) * LANE] = rows(rin_ref, h) + _dot(rows(rqd_ref, h), s.astype(BF16))
            lg = -_softplus(-jnp.full((1, LANE), logit_ref[d * N_HEADS + h], F32))
            sr_ref[d, h] = s * jnp.exp(lg * float(CHUNK)) + _dot_tn(rows(rkd_ref, h), rows(rvb_ref, h))
        for h in range(N_HEADS):
            s = sg_ref[d, h]
            ws = _dot(gwq_ref[h], s.astype(BF16))
            v_new = (rows(gu_ref, h) - ws[:CHUNK]).astype(BF16)
            o_refs[d][1][:, h * LANE:(h + 1) * LANE] = ws[CHUNK:] + _dot(gat_ref[h], v_new)
            sg_ref[d, h] = s * gcd_ref[h:h + 1, :] + _dot_tn(rows(gkd_ref, h), v_new)

    @pl.when(last[t] == 1)
    def _():
        sro_ref[...] = sr_ref[...]
        sgo_ref[...] = sg_ref[...]


def _scan_tables(lay):
    fblk, bblk, first, last, s0idx, soidx = [], [], [], [], [], []
    cp, cs = lay.seq // CHUNK, lay.dec_seq // CHUNK
    for b in range(lay.dec_batch):
        base = lay.n_p // CHUNK + b * cs
        for s in range(cs):
            fblk.append(base + s); bblk.append(base + cs - 1 - s)
            first.append(1 if s == 0 else 0); last.append(0); s0idx.append(b); soidx.append(0)
    for b in range(lay.batch):
        base = b * cp
        for s in range(cp):
            fblk.append(base + s); bblk.append(base + cp - 1 - s)
            first.append(2 if s == 0 else 0); last.append(1 if s == cp - 1 else 0)
            s0idx.append(lay.dec_batch - 1); soidx.append(b)
    return [jnp.asarray(np.asarray(a, np.int32)) for a in (fblk, bblk, first, last, s0idx, soidx)]


def _scan_prep(lay, rq, rk, rv, cq, ck, cv, gb, gt, logit):
    n_chunks = lay.n // CHUNK
    nc = PREP_CHUNKS
    assert n_chunks % nc == 0
    width = N_HEADS * LANE

    def tok(wd):
        return pl.BlockSpec((nc * CHUNK, wd), lambda i: (i, 0))

    def per_dir(shape, dtype):
        spec = pl.BlockSpec((nc, 2) + shape, lambda i: (i,) + (0,) * (1 + len(shape)))
        return spec, jax.ShapeDtypeStruct((n_chunks, 2) + shape, dtype)

    outs = [
        per_dir((STACK, LANE), F32),
        per_dir((STACK, LANE), BF16),
        per_dir((STACK, LANE), BF16),
        (pl.BlockSpec((nc, STACK, LANE), lambda i: (i, 0, 0)), jax.ShapeDtypeStruct((n_chunks, STACK, LANE), BF16)),
        per_dir((STACK, LANE), F32),
        per_dir((N_HEADS, 2 * CHUNK, LANE), BF16),
        per_dir((N_HEADS, CHUNK, CHUNK), BF16),
        per_dir((STACK, LANE), BF16),
        per_dir((SUBLANE, LANE), F32),
    ]
    return pl.pallas_call(
        functools.partial(_prep_kernel, chunks=nc),
        grid=(n_chunks // nc,),
        in_specs=[pl.BlockSpec(memory_space=pltpu.SMEM)] + [tok(width)] * 6
                 + [tok(LANE), pl.BlockSpec((nc, 4 * GDN_HEADS, CHUNK), lambda i: (i, 0, 0))],
        out_specs=[o[0] for o in outs],
        out_shape=[o[1] for o in outs],
        compiler_params=_params("parallel"),
        name="scan_prep",
    )(logit.reshape(-1).astype(F32), rq, rk, rv, cq, ck, cv, gb, gt)


def _even_scan(lay, prep, logit, s0_ret, s0_gdn):
    tables = _scan_tables(lay)
    steps = int(tables[0].shape[0])
    width = N_HEADS * LANE

    def view(which):
        def blk(arr, with_dir=True):
            shape = arr.shape[2:] if with_dir else arr.shape[1:]
            if with_dir:
                return pl.BlockSpec((None, None) + shape, lambda t, *tb: (tb[which][t], which) + (0,) * len(shape))
            return pl.BlockSpec((None,) + shape, lambda t, *tb: (tb[which][t],) + (0,) * len(shape))
        return [blk(a, with_dir=(i != 3)) for i, a in enumerate(prep)]

    state_shape = (2, N_HEADS, LANE, LANE)
    s0_spec = pl.BlockSpec((None,) + state_shape, lambda t, *tb: (tb[4][t], 0, 0, 0, 0))
    so_spec = pl.BlockSpec((None,) + state_shape, lambda t, *tb: (tb[5][t], 0, 0, 0, 0))
    out_f = pl.BlockSpec((CHUNK, width), lambda t, *tb: (tb[0][t], 0))
    out_b = pl.BlockSpec((CHUNK, width), lambda t, *tb: (tb[1][t], 0))
    o_sds = jax.ShapeDtypeStruct((lay.n, width), F32)
    so_sds = jax.ShapeDtypeStruct((lay.batch,) + state_shape, F32)
    args = list(prep)
    return pl.pallas_call(
        _scan_kernel,
        grid_spec=pltpu.PrefetchScalarGridSpec(
            num_scalar_prefetch=6,
            grid=(steps,),
            in_specs=[pl.BlockSpec(memory_space=pltpu.SMEM)] + view(0) + view(1) + [s0_spec, s0_spec],
            out_specs=[out_f, out_b, out_f, out_b, so_spec, so_spec],
            scratch_shapes=[pltpu.VMEM(state_shape, F32), pltpu.VMEM(state_shape, F32)],
        ),
        out_shape=[o_sds, o_sds, o_sds, o_sds, so_sds, so_sds],
        compiler_params=_params("arbitrary"),
        name="even_scan",
    )(*tables, logit.reshape(-1).astype(F32), *args, *args, s0_ret, s0_gdn)


def _even_out_kernel(x_ref, mod_ref, rf_ref, rb_ref, gf_ref, gb_ref, rg_ref, gg_ref, rnw_ref, gnw_ref, w_ref, o_ref):
    ret = _head_rmsnorm(rf_ref[...] + rb_ref[...], rnw_ref[...], RET_HEADS) * _silu(rg_ref[...])
    gdn = _head_rmsnorm(gf_ref[...] + gb_ref[...], gnw_ref[...], GDN_HEADS) * _silu(gg_ref[...])
    y = _dot(jnp.concatenate([ret, gdn], axis=1).astype(BF16), w_ref[...])
    o_ref[...] = x_ref[...] + mod_ref[5:6, :] * y


def _even_out(lay, x, mods, orf, orb, ogf, ogb, rg, gg, rnw, gnw, w):
    return pl.pallas_call(
        _even_out_kernel,
        grid=(lay.tiles,),
        in_specs=[lay.tok(D_MODEL), lay.mod()] + [lay.tok(RET_V)] * 6
                 + [_resident((1, LANE)), _resident((1, LANE)), _resident(w.shape)],
        out_specs=lay.tok(D_MODEL),
        out_shape=jax.ShapeDtypeStruct((lay.n, D_MODEL), F32),
        compiler_params=_params("parallel"),
        name="even_out",
    )(x, mods, orf, orb, ogf, ogb, rg, gg, rnw.reshape(1, LANE), gnw.reshape(1, LANE), w)


def _odd_proj_kernel(x_ref, mod_ref, nw_ref, w_ref, qnw_ref, knw_ref, cos_ref, sin_ref,
                     q_ref, kd_ref, vt_ref, ks_ref, vs_ref, *, p_tiles):
    h = _adaln(x_ref[...], nw_ref[...], mod_ref[3:4, :], mod_ref[4:5, :]).astype(BF16)
    tm = h.shape[0]
    lane = lax.broadcasted_iota(jnp.int32, (tm, LANE), 1)
    lo = lane < ATT_HD
    first16 = (lane & (2 * ROPE_PAIRS - 1)) < ROPE_PAIRS
    cos, sin = cos_ref[...], sin_ref[...]

    def norm_rope(xs, w2):
        sq = xs * xs
        ms_lo = jnp.sum(jnp.where(lo, sq, 0.0), axis=-1, keepdims=True) * (1.0 / ATT_HD)
        ms_hi = jnp.sum(jnp.where(lo, 0.0, sq), axis=-1, keepdims=True) * (1.0 / ATT_HD)
        xn = xs * jnp.where(lo, lax.rsqrt(ms_lo + EPS), lax.rsqrt(ms_hi + EPS)) * w2
        partner = jnp.where(first16, pltpu.roll(xn, LANE - ROPE_PAIRS, 1), pltpu.roll(xn, ROPE_PAIRS, 1))
        return xn * cos + partner * sin

    def both_halves(xs, j):
        swapped = pltpu.roll(xs, ATT_HD, 1)
        return jnp.where(lo, xs, swapped) if j % 2 == 0 else jnp.where(lo, swapped, xs)

    for p in range(ATT_Q // LANE):
        qs = norm_rope(_dot(h, w_ref[:, p * LANE:(p + 1) * LANE]), qnw_ref[...])
        q_ref[:, p * LANE:(p + 1) * LANE] = (qs * Q_SCALE).astype(BF16)
    is_prompt = pl.program_id(0) < p_tiles
    for p in range(ATT_KV // LANE):
        ks = norm_rope(_dot(h, w_ref[:, ATT_Q + p * LANE:ATT_Q + (p + 1) * LANE]), knw_ref[...])
        vs = _dot(h, w_ref[:, ATT_Q + ATT_KV + p * LANE:ATT_Q + ATT_KV + (p + 1) * LANE])
        vst = vs.T.astype(BF16)
        for j in (2 * p, 2 * p + 1):
            kd_ref[j] = both_halves(ks, j).astype(BF16)
            for c in range(tm // KEY_CHUNK):
                vt_ref[j, c] = vst[(j % 2) * ATT_HD:(j % 2 + 1) * ATT_HD, c * KEY_CHUNK:(c + 1) * KEY_CHUNK]

        @pl.when(is_prompt)
        def _():
            ks_ref[:, p * LANE:(p + 1) * LANE] = ks
            vs_ref[:, p * LANE:(p + 1) * LANE] = vs


def _rope_tables(lay):
    t = jnp.arange(lay.dec_seq)
    inv = ROPE_THETA ** (-jnp.arange(ROPE_PAIRS, dtype=F32) / ROPE_PAIRS)
    ar = (t // GRID_W).astype(F32)[:, None] * inv
    ac = (t % GRID_W).astype(F32)[:, None] * inv
    cos = jnp.concatenate([jnp.cos(ar), jnp.cos(ar), jnp.cos(ac), jnp.cos(ac)], axis=1)
    sin = jnp.concatenate([-jnp.sin(ar), jnp.sin(ar), -jnp.sin(ac), jnp.sin(ac)], axis=1)
    cos = jnp.concatenate([jnp.tile(cos, (1, 2)), jnp.ones((lay.tm, LANE), F32)], axis=0)
    sin = jnp.concatenate([jnp.tile(sin, (1, 2)), jnp.zeros((lay.tm, LANE), F32)], axis=0)
    return cos, sin


def _odd_proj(lay, x, mods, nw, w, qnw, knw):
    cos, sin = _rope_tables(lay)

    def table_block(i):
        return jnp.where(i < lay.p_tiles, lay.tiles_per_seq, lax.rem(jnp.maximum(i - lay.p_tiles, 0), lay.tiles_per_seq))

    tab = pl.BlockSpec((lay.tm, LANE), lambda i: (table_block(i), 0))
    std = pl.BlockSpec((lay.tm, ATT_KV), lambda i: (jnp.minimum(i, lay.p_tiles - 1), 0))
    tm = lay.tm
    return pl.pallas_call(
        functools.partial(_odd_proj_kernel, p_tiles=lay.p_tiles),
        grid=(lay.tiles,),
        in_specs=[lay.tok(D_MODEL), lay.mod(), _resident((1, D_MODEL)), _resident(w.shape),
                  _resident((1, LANE)), _resident((1, LANE)), tab, tab],
        out_specs=[
            lay.tok(ATT_Q),
            pl.BlockSpec((ATT_KV_HEADS, tm, LANE), lambda i: (0, i, 0)),
            pl.BlockSpec((ATT_KV_HEADS, tm // KEY_CHUNK, ATT_HD, KEY_CHUNK), lambda i: (0, i, 0, 0)),
            std, std,
        ],
        out_shape=[
            jax.ShapeDtypeStruct((lay.n, ATT_Q), BF16),
            jax.ShapeDtypeStruct((ATT_KV_HEADS, lay.n, LANE), BF16),
            jax.ShapeDtypeStruct((ATT_KV_HEADS, lay.n // KEY_CHUNK, ATT_HD, KEY_CHUNK), BF16),
            jax.ShapeDtypeStruct((lay.n_p, ATT_KV), F32),
            jax.ShapeDtypeStruct((lay.n_p, ATT_KV), F32),
        ],
        compiler_params=_params("arbitrary"),
        name="odd_proj",
    )(x, mods, nw.reshape(1, D_MODEL), w, jnp.tile(qnw.reshape(1, ATT_HD), (1, 2)),
      jnp.tile(knw.reshape(1, ATT_HD), (1, 2)), cos, sin)


def _attn_kernel(*refs, cached):
    if cached:
        zero_ref, q_ref, kn_ref, vn_ref, kc_ref, vc_ref, _, o_ref, *s_scr = refs
    else:
        zero_ref, q_ref, kn_ref, vn_ref, o_ref, *s_scr = refs
    z = zero_ref[0]
    tq = q_ref.shape[0]
    ck = KEY_CHUNK
    n_c = kc_ref.shape[0] // ck if cached else 0
    n_n = kn_ref.shape[0] // ck
    lo = lax.broadcasted_iota(jnp.int32, (tq, LANE), 1) < ATT_HD
    qms = []
    for g in range(ATT_GROUP):
        q2 = q_ref[:, (g // 2) * LANE:(g // 2 + 1) * LANE]
        qms.append(jnp.where(lo, q2, jnp.zeros_like(q2)) if g % 2 == 0 else jnp.where(lo, jnp.zeros_like(q2), q2))

    ones_rows = jnp.where(lax.broadcasted_iota(jnp.int32, (2 * SUBLANE, ck), 0) == 0, 1.0, 0.0).astype(BF16)
    m_fin = [None] * ATT_GROUP
    outs = [None] * ATT_GROUP
    for ph in range(ATT_GROUP + 1):
        ga = ph if ph < ATT_GROUP else None
        gb = ph - 1 if ph >= 1 else None
        ma = jnp.full((1, tq), -jnp.inf, F32)
        acc = jnp.zeros((ATT_HD + 2 * SUBLANE, tq), F32)
        for c in range(n_c + n_n):
            if c < n_c:
                k_blk, vt_blk = kc_ref[c * ck:(c + 1) * ck, :], vc_ref[c]
            else:
                k_blk, vt_blk = kn_ref[(c - n_c) * ck:(c - n_c + 1) * ck, :], vn_ref[c - n_c]
            if ga is not None:
                st = _dot_nt(k_blk, qms[ga])
                s_scr[ga % 2][z + c] = st
                ma = jnp.maximum(ma, jnp.max(st, axis=0, keepdims=True))
            if gb is not None:
                p = jnp.exp2(s_scr[gb % 2][z + c] - m_fin[gb]).astype(BF16)
                acc = acc + _dot(jnp.concatenate([vt_blk, ones_rows], axis=0), p)
        if ga is not None:
            m_fin[ga] = ma
        if gb is not None:
            outs[gb] = acc[:ATT_HD] / acc[ATT_HD:ATT_HD + 1]
    o_ref[...] = jnp.concatenate(outs, axis=0).T


def _attention(lay, q, kd, vt, kc, vc):
    gw = ATT_GROUP * ATT_HD
    o_sds = jax.ShapeDtypeStruct((lay.n, ATT_Q), F32)
    ck = KEY_CHUNK
    assert lay.seq % ck == 0 and lay.dec_seq % ck == 0 and lay.past % ck == 0

    zero = jnp.zeros((1,), jnp.int32)
    zero_spec = pl.BlockSpec(memory_space=pltpu.SMEM)

    tq = min(ATT_Q_TILE, lay.seq)
    nq = lay.seq // tq
    nck = lay.seq // ck
    out = pl.pallas_call(
        functools.partial(_attn_kernel, cached=False),
        grid=(lay.batch, ATT_KV_HEADS, nq),
        in_specs=[
            zero_spec,
            pl.BlockSpec((tq, gw), lambda b, j, i: (b * nq + i, j)),
            pl.BlockSpec((None, lay.seq, LANE), lambda b, j, i: (j, b, 0)),
            pl.BlockSpec((None, nck, ATT_HD, ck), lambda b, j, i: (j, b, 0, 0)),
        ],
        out_specs=pl.BlockSpec((tq, gw), lambda b, j, i: (b * nq + i, j)),
        out_shape=o_sds,
        scratch_shapes=[pltpu.VMEM((nck, ck, tq), F32)] * 2,
        compiler_params=_params("parallel", "parallel", "parallel"),
        name="attn_context",
    )(zero, q, kd, vt)

    tq = min(ATT_Q_TILE, lay.dec_seq)
    nq = lay.dec_seq // tq
    nck = lay.dec_seq // ck
    qbase = lay.n_p // tq
    kbase = lay.n_p // lay.dec_seq
    assert lay.n_p % lay.dec_seq == 0
    return pl.pallas_call(
        functools.partial(_attn_kernel, cached=True),
        grid=(lay.dec_batch, ATT_KV_HEADS, nq),
        in_specs=[
            zero_spec,
            pl.BlockSpec((tq, gw), lambda b, j, i: (qbase + b * nq + i, j)),
            pl.BlockSpec((None, lay.dec_seq, LANE), lambda b, j, i: (j, kbase + b, 0)),
            pl.BlockSpec((None, nck, ATT_HD, ck), lambda b, j, i: (j, kbase + b, 0, 0)),
            pl.BlockSpec((None, None, lay.past, LANE), lambda b, j, i: (b, j, 0, 0)),
            pl.BlockSpec((None, None, lay.past // ck, ATT_HD, ck), lambda b, j, i: (b, j, 0, 0, 0)),
            pl.BlockSpec(memory_space=pl.ANY),
        ],
        out_specs=pl.BlockSpec((tq, gw), lambda b, j, i: (qbase + b * nq + i, j)),
        out_shape=o_sds,
        scratch_shapes=[pltpu.VMEM((lay.past // ck + nck, ck, tq), F32)] * 2,
        input_output_aliases={6: 0},
        compiler_params=_params("parallel", "parallel", "parallel"),
        name="attn_latent",
    )(zero, q, kd, vt, kc, vc, out)


def _odd_out_kernel(x_ref, mod_ref, a_ref, w_ref, o_ref):
    o_ref[...] = x_ref[...] + mod_ref[5:6, :] * _dot(a_ref[...].astype(BF16), w_ref[...])


def _odd_out(lay, x, mods, a, w):
    return pl.pallas_call(
        _odd_out_kernel,
        grid=(lay.tiles,),
        in_specs=[lay.tok(D_MODEL), lay.mod(), lay.tok(ATT_Q), _resident(w.shape)],
        out_specs=lay.tok(D_MODEL),
        out_shape=jax.ShapeDtypeStruct((lay.n, D_MODEL), F32),
        compiler_params=_params("parallel"),
        name="odd_out",
    )(x, mods, a, w)


def _pad_state(s):
    return jnp.pad(s, [(0, 0)] * (s.ndim - 2) + [(0, LANE - s.shape[-2]), (0, 0)])


def _cache_keys(c):
    return jnp.tile(c.transpose(0, 2, 1, 3), (1, 1, 1, 2)).astype(BF16)


def _cache_values(c):
    b, t, kvh, hd = c.shape
    return c.reshape(b, t // KEY_CHUNK, KEY_CHUNK, kvh, hd).transpose(0, 3, 1, 4, 2).astype(BF16)


def kernel(x_prompt, x_sample, state_ret, state_gdn, cache_k, cache_v, c, c_ctx,
           mod_w, mod_b, norm_w, ffn_w_in, ffn_w_out, even_w_in, even_w_out,
           ret_decay_logit, ret_norm_w, gdn_conv_w, gdn_A_log, gdn_dt_bias, gdn_norm_w,
           odd_w_in, odd_w_out, q_norm_w, k_norm_w, final_norm_w):
    batch, seq, d = x_prompt.shape
    dec_batch, dec_seq, _ = x_sample.shape
    depth = mod_w.shape[0]
    lay = _Layout(batch, seq, dec_batch, dec_seq, cache_k.shape[2])

    x = jnp.concatenate([x_prompt.reshape(lay.n_p, d), x_sample.reshape(lay.n_s, d)], axis=0)
    n_cond = -(-(1 + dec_batch) // (2 * SUBLANE)) * (2 * SUBLANE)
    cond = jnp.zeros((n_cond, d), F32).at[0].set(c_ctx).at[1:1 + dec_batch].set(c)
    mods = _modulation(cond, mod_w, mod_b)

    new_ret, new_gdn, new_k, new_v = [], [], [], []
    for l in range(depth):
        m = mods[l]
        last = l == depth - 1
        x = _ffn(lay, x, m, norm_w[l, 0], ffn_w_in[l, 0].astype(BF16), ffn_w_out[l, 0].astype(BF16), 0)
        if l % 2 == 0:
            e = l // 2
            rq, rk, rv, rg, gqkv, gg, gab = _even_proj(lay, x, m, norm_w[l, 1], _even_weight(even_w_in[e]))
            cq, ck, cv, gb, gt = _even_conv(lay, gqkv, gab, gdn_conv_w[e], gdn_A_log[e], gdn_dt_bias[e])
            prep = _scan_prep(lay, rq, rk, rv, cq, ck, cv, gb, gt, ret_decay_logit[e])
            orf, orb, ogf, ogb, sr, sg = _even_scan(
                lay, prep, ret_decay_logit[e], _pad_state(state_ret[:, e]), state_gdn[:, e])
            new_ret.append(sr[:, :, :, :RET_DK, :])
            new_gdn.append(sg)
            x = _even_out(lay, x, m, orf, orb, ogf, ogb, rg, gg, ret_norm_w[e], gdn_norm_w[e],
                          even_w_out[e].astype(BF16))
        else:
            o = l // 2
            q, kd, vt, ks, vs = _odd_proj(lay, x, m, norm_w[l, 1], odd_w_in[o].astype(BF16), q_norm_w[o], k_norm_w[o])
            a = _attention(lay, q, kd, vt, _cache_keys(cache_k[:, o]), _cache_values(cache_v[:, o]))
            new_k.append(ks.reshape(batch, seq, ATT_KV_HEADS, ATT_HD))
            new_v.append(vs.reshape(batch, seq, ATT_KV_HEADS, ATT_HD))
            x = _odd_out(lay, x, m, a, odd_w_out[o].astype(BF16))
        x = _ffn(lay, x, m, norm_w[l, 2], ffn_w_in[l, 1].astype(BF16), ffn_w_out[l, 1].astype(BF16), 2,
                 final_w=final_norm_w if last else None)

    y_prompt = x[:lay.n_p].reshape(batch, seq, d)
    y_sample = x[lay.n_p:].reshape(dec_batch, dec_seq, d)
    return (y_prompt, y_sample, jnp.stack(new_ret, axis=1), jnp.stack(new_gdn, axis=1),
            jnp.stack(new_k, axis=1), jnp.stack(new_v, axis=1))
```

```python
import functools

import numpy as np
import jax
import jax.numpy as jnp
from jax import lax
from jax.experimental import pallas as pl
from jax.experimental.pallas import tpu as pltpu

F32 = jnp.float32
BF16 = jnp.bfloat16

D_MODEL = 1024
GRID_W = 64
RET_HEADS = 4
RET_DK = 64
RET_DV = 128
GDN_HEADS = 4
GDN_DK = 128
GDN_DV = 128
CHUNK = 64
ATT_HEADS = 16
ATT_KV_HEADS = 4
ATT_HD = 64
ATT_GROUP = ATT_HEADS // ATT_KV_HEADS
ROPE_THETA = 10000.0
ROPE_PAIRS = ATT_HD // 4
FFN_HIDDEN = 2816
N_MOD = 9
EPS = 1e-6

RET_QK = RET_HEADS * RET_DK
RET_V = RET_HEADS * RET_DV
GDN_QK = GDN_HEADS * GDN_DK
GDN_V = GDN_HEADS * GDN_DV
ATT_Q = ATT_HEADS * ATT_HD
ATT_KV = ATT_KV_HEADS * ATT_HD

LANE = 128
SUBLANE = 8
V7X_VMEM_BYTES = 64 * 1024 * 1024
VMEM_LIMIT = V7X_VMEM_BYTES - 8 * 1024 * 1024

TOKEN_TILE = 512
CONV_TILE = 256
ATT_Q_TILE = 512
KEY_CHUNK = 256
Q_SCALE = ATT_HD ** -0.5 * float(np.log2(np.e))
MXU_TILE = 256
FFN_CHUNKS = ((0, 6 * MXU_TILE), (6 * MXU_TILE, FFN_HIDDEN))
HEAD_PAD = LANE
PREP_CHUNKS = 4
SCAN_CHUNKS = 4


def _params(*sem):
    return pltpu.CompilerParams(dimension_semantics=sem, vmem_limit_bytes=VMEM_LIMIT)


def _resident(shape):
    nd = len(shape)
    return pl.BlockSpec(shape, lambda *_: (0,) * nd, pipeline_mode=pl.Buffered(1))


def _dot(a, b):
    return jnp.dot(a, b, preferred_element_type=F32)


def _dot_nt(a, b):
    return lax.dot_general(a, b, (((1,), (1,)), ((), ())), preferred_element_type=F32)


def _dot_tn(a, b):
    return lax.dot_general(a, b, (((0,), (0,)), ((), ())), preferred_element_type=F32)


def _split2(a):
    hi = a.astype(BF16)
    lo = (a - hi.astype(F32)).astype(BF16)
    return hi, lo


def _dot3(a, b):
    ah, al = _split2(a)
    bh, bl = _split2(b)
    return _dot(ah, bh) + _dot(ah, bl) + _dot(al, bh)


def _silu(x):
    return x * jax.nn.sigmoid(x)


def _softplus(x):
    return jnp.maximum(x, 0.0) + jnp.log(1.0 + jnp.exp(-jnp.abs(x)))


def _adaln(x, nw, shift, scale):
    ms = jnp.mean(x * x, axis=-1, keepdims=True)
    return (x * lax.rsqrt(ms + EPS)) * nw * (1.0 + scale) + shift


def _head_rmsnorm(x, w, n_heads):
    parts = []
    for h in range(n_heads):
        xs = x[:, h * LANE:(h + 1) * LANE]
        ms = jnp.mean(xs * xs, axis=-1, keepdims=True)
        parts.append(xs * lax.rsqrt(ms + EPS) * w)
    return jnp.concatenate(parts, axis=1)


def _mod_kernel(cond_ref, w_ref, b_ref, o_ref):
    c = cond_ref[...]
    o_ref[0] = _dot(_silu(c).astype(BF16), w_ref[0].astype(BF16)) + b_ref[0]


def _modulation(cond, mod_w, mod_b):
    depth, d, n = mod_w.shape
    r = cond.shape[0]
    tn = n // 8
    out = pl.pallas_call(
        _mod_kernel,
        grid=(depth, n // tn),
        in_specs=[
            pl.BlockSpec((r, d), lambda l, j: (0, 0)),
            pl.BlockSpec((1, d, tn), lambda l, j: (l, 0, j)),
            pl.BlockSpec((1, 1, tn), lambda l, j: (l, 0, j)),
        ],
        out_specs=pl.BlockSpec((1, r, tn), lambda l, j: (l, 0, j)),
        out_shape=jax.ShapeDtypeStruct((depth, r, n), F32),
        compiler_params=_params("parallel", "parallel"),
        name="modulation",
    )(cond, mod_w, mod_b.reshape(depth, 1, n))
    return out.reshape(depth, r, N_MOD, d)


class _Layout:
    def __init__(self, batch, seq, dec_batch, dec_seq, past):
        self.batch, self.seq, self.dec_batch, self.dec_seq, self.past = batch, seq, dec_batch, dec_seq, past
        self.n_p = batch * seq
        self.n_s = dec_batch * dec_seq
        self.n = self.n_p + self.n_s
        self.tm = min(TOKEN_TILE, self.n_p, dec_seq)
        assert self.n_p % self.tm == 0 and dec_seq % self.tm == 0
        self.tiles = self.n // self.tm
        self.p_tiles = self.n_p // self.tm
        self.tiles_per_seq = dec_seq // self.tm

    def group(self, i):
        return jnp.where(i < self.p_tiles, 0, 1 + (i - self.p_tiles) // self.tiles_per_seq)

    def tok(self, width, tm=None):
        tm = tm or self.tm
        return pl.BlockSpec((tm, width), lambda i: (i, 0))

    def mod(self):
        return pl.BlockSpec((None, N_MOD, D_MODEL), lambda i: (self.group(i), 0, 0))


def _ffn_kernel(*refs, j, first, final, p_tiles):
    refs = list(refs)
    x_refs = [refs.pop(0) for _ in range(2 if first else 1)]
    mod_ref, nw_ref, win_ref, wout_ref = (refs.pop(0) for _ in range(4))
    fw_ref = refs.pop(0) if final else None
    o_refs = refs
    is_prompt = pl.program_id(0) < p_tiles
    x = jnp.where(is_prompt, x_refs[0][...], x_refs[1][...]) if first else x_refs[0][...]
    h = _adaln(x, nw_ref[...], mod_ref[3 * j:3 * j + 1, :], mod_ref[3 * j + 1:3 * j + 2, :]).astype(BF16)
    acc = None
    for lo, hi in FFN_CHUNKS:
        a = _dot(h, win_ref[:, lo:hi])
        b = _dot(h, win_ref[:, FFN_HIDDEN + lo:FFN_HIDDEN + hi])
        y = _dot((_silu(a) * b).astype(BF16), wout_ref[lo:hi, :])
        acc = y if acc is None else acc + y
    out = x + (0.5 * mod_ref[3 * j + 2:3 * j + 3, :]) * acc
    if not final:
        o_refs[0][...] = out
        return
    ms = jnp.mean(out * out, axis=-1, keepdims=True)
    out = out * lax.rsqrt(ms + EPS) * fw_ref[...]

    @pl.when(is_prompt)
    def _():
        o_refs[0][...] = out

    @pl.when(jnp.logical_not(is_prompt))
    def _():
        o_refs[1][...] = out


def _ffn(lay, xs, mods, nw, w_in, w_out, j, final_w=None):
    first = isinstance(xs, tuple)
    final = final_w is not None
    prompt_rows = pl.BlockSpec((lay.tm, D_MODEL), lambda i: (jnp.minimum(i, lay.p_tiles - 1), 0))
    latent_rows = pl.BlockSpec((lay.tm, D_MODEL), lambda i: (jnp.maximum(i - lay.p_tiles, 0), 0))
    ins = list(xs) if first else [xs]
    specs = [prompt_rows, latent_rows] if first else [lay.tok(D_MODEL)]
    ins += [mods, nw.reshape(1, D_MODEL), w_in, w_out]
    specs += [lay.mod(), _resident((1, D_MODEL)), _resident(w_in.shape), _resident(w_out.shape)]
    if final:
        ins.append(final_w.reshape(1, D_MODEL))
        specs.append(_resident((1, D_MODEL)))
        out_specs = [prompt_rows, latent_rows]
        out_shape = [jax.ShapeDtypeStruct((lay.n_p, D_MODEL), F32), jax.ShapeDtypeStruct((lay.n_s, D_MODEL), F32)]
    else:
        out_specs = lay.tok(D_MODEL)
        out_shape = jax.ShapeDtypeStruct((lay.n, D_MODEL), F32)
    return pl.pallas_call(
        functools.partial(_ffn_kernel, j=j, first=first, final=final, p_tiles=lay.p_tiles),
        grid=(lay.tiles,),
        in_specs=specs,
        out_specs=out_specs,
        out_shape=out_shape,
        compiler_params=_params("arbitrary"),
        name="ffn_final" if final else ("ffn_first" if first else "ffn"),
    )(*ins)


EVEN_WIDTHS = (RET_HEADS * HEAD_PAD, RET_HEADS * HEAD_PAD, RET_V, RET_V, 2 * GDN_QK + GDN_V, GDN_V, LANE)


def _proj_kernel(x_ref, mod_ref, nw_ref, w_ref, *o_refs, j, widths):
    h = _adaln(x_ref[...], nw_ref[...], mod_ref[3 * j:3 * j + 1, :], mod_ref[3 * j + 1:3 * j + 2, :]).astype(BF16)
    off = 0
    for o_ref, wd in zip(o_refs, widths):
        o_ref[...] = _dot(h, w_ref[:, off:off + wd])
        off += wd


def _even_proj(lay, x, mods, nw, w):
    return pl.pallas_call(
        functools.partial(_proj_kernel, j=1, widths=EVEN_WIDTHS),
        grid=(lay.tiles,),
        in_specs=[lay.tok(D_MODEL), lay.mod(), _resident((1, D_MODEL)), _resident(w.shape)],
        out_specs=[lay.tok(wd) for wd in EVEN_WIDTHS],
        out_shape=[jax.ShapeDtypeStruct((lay.n, wd), F32) for wd in EVEN_WIDTHS],
        compiler_params=_params("parallel"),
        name="even_proj",
    )(x, mods, nw.reshape(1, D_MODEL), w)


def _even_weight(w):
    d = w.shape[0]
    o = 0
    rq = w[:, o:o + RET_QK]; o += RET_QK
    rk = w[:, o:o + RET_QK]; o += RET_QK
    rest = w[:, o:o + 2 * RET_V + 2 * GDN_QK + 2 * GDN_V]; o += 2 * RET_V + 2 * GDN_QK + 2 * GDN_V
    gab = w[:, o:]

    def pad_heads(m):
        m = m.reshape(d, RET_HEADS, RET_DK)
        return jnp.pad(m, ((0, 0), (0, 0), (0, HEAD_PAD - RET_DK))).reshape(d, RET_HEADS * HEAD_PAD)

    gab = jnp.pad(gab, ((0, 0), (0, LANE - gab.shape[1])))
    return jnp.concatenate([pad_heads(rq), pad_heads(rk * RET_DK ** -0.5), rest, gab], axis=1).astype(BF16)


def _conv_kernel(x_ref, prev_ref, next_ref, cw_ref, gab_ref, alog_ref, dt_ref,
                 cq_ref, ck_ref, cv_ref, gb_ref, gt_ref, *, n_p, seq, dec_seq):
    tc = x_ref.shape[0]
    start = pl.program_id(0) * tc
    in_prompt = start < n_p
    rel = jnp.where(in_prompt, start, start - n_p)
    slen = jnp.where(in_prompt, seq, dec_seq)
    pos = lax.rem(rel, slen)
    x = x_ref[...]
    prev_row = jnp.where(pos == 0, 0.0, prev_ref[SUBLANE - 1:SUBLANE, :])
    next_row = jnp.where(pos + tc == slen, 0.0, next_ref[0:1, :])
    row = lax.broadcasted_iota(jnp.int32, x.shape, 0)
    xm1 = jnp.where(row == 0, prev_row, pltpu.roll(x, 1, 0))
    xp1 = jnp.where(row == tc - 1, next_row, pltpu.roll(x, tc - 1, 0))
    y = _silu(xm1 * cw_ref[0:1, :] + x * cw_ref[1:2, :] + xp1 * cw_ref[2:3, :])
    for h in range(GDN_HEADS):
        q = y[:, h * LANE:(h + 1) * LANE]
        k = y[:, GDN_QK + h * LANE:GDN_QK + (h + 1) * LANE]
        cq_ref[:, h * LANE:(h + 1) * LANE] = (
            q * lax.rsqrt(jnp.sum(q * q, axis=-1, keepdims=True) + EPS) * GDN_DK ** -0.5)
        ck_ref[:, h * LANE:(h + 1) * LANE] = k * lax.rsqrt(jnp.sum(k * k, axis=-1, keepdims=True) + EPS)
    cv_ref[...] = y[:, 2 * GDN_QK:]
    g = gab_ref[...]
    lane = lax.broadcasted_iota(jnp.int32, g.shape, 1)
    gb = jnp.where(lane < 2 * GDN_HEADS, -jnp.exp(alog_ref[...]) * _softplus(g + dt_ref[...]), jax.nn.sigmoid(g))
    gb_ref[...] = gb
    gbt = gb.T
    for c in range(tc // CHUNK):
        gt_ref[c] = gbt[0:4 * GDN_HEADS, c * CHUNK:(c + 1) * CHUNK]


def _even_conv(lay, gqkv, gab, conv_w, a_log, dt_bias):
    tc = min(CONV_TILE, lay.seq, lay.dec_seq)
    assert lay.seq % tc == 0 and lay.dec_seq % tc == 0 and tc % LANE == 0
    width = gqkv.shape[1]
    nblk8 = lay.n // SUBLANE
    r = tc // SUBLANE

    def pad_row(v):
        v = v.reshape(1, -1).astype(F32)
        return jnp.pad(v, ((0, 0), (0, LANE - v.shape[1])))

    return pl.pallas_call(
        functools.partial(_conv_kernel, n_p=lay.n_p, seq=lay.seq, dec_seq=lay.dec_seq),
        grid=(lay.n // tc,),
        in_specs=[
            pl.BlockSpec((tc, width), lambda i: (i, 0)),
            pl.BlockSpec((SUBLANE, width), lambda i: (jnp.maximum(i * r - 1, 0), 0)),
            pl.BlockSpec((SUBLANE, width), lambda i: (jnp.minimum((i + 1) * r, nblk8 - 1), 0)),
            _resident(conv_w.shape),
            pl.BlockSpec((tc, LANE), lambda i: (i, 0)),
            _resident((1, LANE)),
            _resident((1, LANE)),
        ],
        out_specs=[
            pl.BlockSpec((tc, GDN_QK), lambda i: (i, 0)),
            pl.BlockSpec((tc, GDN_QK), lambda i: (i, 0)),
            pl.BlockSpec((tc, GDN_V), lambda i: (i, 0)),
            pl.BlockSpec((tc, LANE), lambda i: (i, 0)),
            pl.BlockSpec((tc // CHUNK, 4 * GDN_HEADS, CHUNK), lambda i: (i, 0, 0)),
        ],
        out_shape=[
            jax.ShapeDtypeStruct((lay.n, GDN_QK), F32),
            jax.ShapeDtypeStruct((lay.n, GDN_QK), F32),
            jax.ShapeDtypeStruct((lay.n, GDN_V), F32),
            jax.ShapeDtypeStruct((lay.n, LANE), F32),
            jax.ShapeDtypeStruct((lay.n // CHUNK, 4 * GDN_HEADS, CHUNK), F32),
        ],
        compiler_params=_params("parallel"),
        name="even_conv",
    )(gqkv, gqkv, gqkv, conv_w, gab, pad_row(a_log), pad_row(dt_bias))


N_HEADS = 4
STACK = N_HEADS * CHUNK


def _prep_kernel(logit_ref, rq_ref, rk_ref, rv_ref, cq_ref, ck_ref, cv_ref, gb_ref, gt_ref,
                 rin_ref, rqd_ref, rkd_ref, rvb_ref, gu_ref, gwq_ref, gat_ref, gkd_ref, gcd_ref, *, chunks):
    wr = lax.broadcasted_iota(jnp.int32, (CHUNK, STACK), 0)
    wcol = lax.broadcasted_iota(jnp.int32, (CHUNK, STACK), 1)
    wc = wcol & (CHUNK - 1)
    whead = wcol >> 6
    r4 = lax.broadcasted_iota(jnp.int32, (STACK, STACK), 0)
    c4 = lax.broadcasted_iota(jnp.int32, (STACK, STACK), 1)
    same_head = (r4 >> 6) == (c4 >> 6)
    rc = lax.broadcasted_iota(jnp.int32, (CHUNK, CHUNK), 0)
    cc = lax.broadcasted_iota(jnp.int32, (CHUNK, CHUNK), 1)
    rpos = (lax.broadcasted_iota(jnp.int32, (STACK, LANE), 0) & (CHUNK - 1)).astype(F32)
    eye_w = jnp.where(wr == wc, 1.0, 0.0)

    def rows(x, h):
        return x[h * CHUNK:(h + 1) * CHUNK]

    def by_head(parts):
        out = parts[N_HEADS - 1]
        for h in range(N_HEADS - 2, -1, -1):
            out = jnp.where(whead == h, parts[h], out)
        return out

    def block_diag(w):
        return tuple(jnp.where(same_head, jnp.concatenate([x] * N_HEADS, axis=0), jnp.zeros((STACK, STACK), BF16))
                     for x in _split2(w))

    def wide_times(a, bh, bl):
        ah, al = _split2(a)
        hh = _dot(jnp.concatenate([ah, al], axis=0), bh)
        return hh[:CHUNK] + hh[CHUNK:] + _dot(ah, bl)

    ret_const = []
    for d in range(2):
        lg4 = jnp.concatenate(
            [-_softplus(-jnp.full((CHUNK, STACK), logit_ref[d * N_HEADS + h], F32)) for h in range(N_HEADS)], axis=0)
        tri4 = (r4 >= c4) if d == 0 else (r4 <= c4)
        dist4 = ((r4 - c4) if d == 0 else (c4 - r4)).astype(F32)
        dec4 = jnp.exp(jnp.where(same_head, jnp.where(tri4, lg4 * dist4, -jnp.inf), -jnp.inf))
        spos = rpos if d == 0 else (CHUNK - 1.0) - rpos
        lgc = lg4[:, :LANE]
        ret_const.append((dec4, jnp.exp(lgc * (spos + 1.0)), jnp.exp(lgc * ((CHUNK - 1.0) - spos))))

    setup = []
    for n in range(chunks):
        tok = slice(n * CHUNK, (n + 1) * CHUNK)

        def stack(ref):
            return jnp.concatenate([ref[tok, h * LANE:(h + 1) * LANE] for h in range(N_HEADS)], axis=0)

        q4, k4, v4 = stack(rq_ref), stack(rk_ref), stack(rv_ref)
        q4b, k4b, v4b = q4.astype(BF16), k4.astype(BF16), v4.astype(BF16)
        rvb_ref[n] = v4b
        qk_ret = _dot_nt(q4b, k4b)
        gq4, gk4, gv4 = stack(cq_ref), stack(ck_ref), stack(cv_ref)
        gq4b, gk4b = gq4.astype(BF16), gk4.astype(BF16)
        gb = gb_ref[tok, :]
        gt = gt_ref[n]
        gh, gl = _split2(gb)
        gll = (gb - gh.astype(F32) - gl.astype(F32)).astype(BF16)
        th, tl = _split2(gt)
        tll = (gt - th.astype(F32) - tl.astype(F32)).astype(BF16)

        for d in range(2):
            dec4, qdec, kdec = ret_const[d]
            rin_ref[n, d] = _dot((qk_ret * dec4).astype(BF16), v4b)
            rqd_ref[n, d] = (q4 * qdec).astype(BF16)
            rkd_ref[n, d] = (k4 * kdec).astype(BF16)

        for d in range(2):
            fwd = d == 0
            low = jnp.where((rc >= cc) if fwd else (rc <= cc), 1.0, 0.0).astype(BF16)
            upp_w = jnp.where((wr <= wc) if fwd else (wr >= wc), 1.0, 0.0).astype(BF16)
            tri_w = (wr >= wc) if fwd else (wr <= wc)
            tri_c = (rc >= cc) if fwd else (rc <= cc)
            gc_col = _dot(low, gh) + _dot(low, gl) + _dot(low, gll)
            gc_row = _dot(th, upp_w) + _dot(tl, upp_w) + _dot(tll, upp_w)
            last_row = CHUNK - 1 if fwd else 0
            idx = [d * N_HEADS + h for h in range(N_HEADS)]
            gcol = [gc_col[:, i:i + 1] for i in idx]
            glast = [gc_col[last_row:last_row + 1, i:i + 1] for i in idx]
            beta = [gb[:, 2 * N_HEADS + i:2 * N_HEADS + i + 1] for i in idx]
            col4 = lambda parts: jnp.concatenate([jnp.broadcast_to(p, (CHUNK, 1)) for p in parts], axis=0)
            gcol4, glast4, beta4 = col4(gcol), col4(glast), col4(beta)
            eg4 = jnp.exp(gcol4)
            kb4 = gk4 * beta4
            gcol_w = by_head([jnp.broadcast_to(g, (CHUNK, STACK)) for g in gcol])
            grow_w = by_head([jnp.broadcast_to(gc_row[i:i + 1, :], (CHUNK, STACK)) for i in idx])
            dmat_w = jnp.exp(jnp.where(tri_w, gcol_w - grow_w, -jnp.inf))
            kk = _dot_nt(kb4.astype(BF16), gk4b)
            kk_w = by_head([rows(kk, h) for h in range(N_HEADS)])
            p = jnp.where(wr == wc, 0.0, -(kk_w * dmat_w))
            setup.append((n, d, p, gcol, glast, gcol4, glast4, beta4, eg4, kb4, gc_row, idx, tri_c,
                          gq4, gk4, gv4, gq4b, gk4b))

    ps = [s[2] for s in setup]
    bds = [block_diag(p) for p in ps]
    tinvs = [eye_w + p for p in ps]
    for _ in range(5):
        ps = [wide_times(p, *bd) for p, bd in zip(ps, bds)]
        bds = [block_diag(p) for p in ps]
        tinvs = [t + wide_times(t, *bd) for t, bd in zip(tinvs, bds)]
    t_bds = [block_diag(t) for t in tinvs]
    for (n, d, _, gcol, glast, gcol4, glast4, beta4, eg4, kb4, gc_row, idx, tri_c,
         gq4, gk4, gv4, gq4b, gk4b), (th_bd, tl_bd) in zip(setup, t_bds):
        rh, rl = _split2(jnp.concatenate([gv4 * beta4, kb4 * eg4], axis=1))
        uw = _dot(th_bd, rh) + _dot(th_bd, rl) + _dot(tl_bd, rh)
        gu_ref[n, d] = uw[:, :GDN_DV]
        w4 = uw[:, GDN_DV:].astype(BF16)
        qd4 = (gq4 * eg4).astype(BF16)
        gkd_ref[n, d] = (gk4 * jnp.exp(glast4 - gcol4)).astype(BF16)
        for h in range(N_HEADS):
            gwq_ref[n, d, h] = jnp.concatenate([rows(w4, h), rows(qd4, h)], axis=0)
            dm = jnp.exp(jnp.where(tri_c, gcol[h] - gc_row[idx[h]:idx[h] + 1, :CHUNK], -jnp.inf))
            gat_ref[n, d, h] = (_dot_nt(rows(gq4b, h), rows(gk4b, h)) * dm).astype(BF16)
        gcd_ref[n, d] = jnp.concatenate(
            [jnp.broadcast_to(jnp.exp(g), (1, LANE)) for g in glast] + [jnp.zeros((SUBLANE - N_HEADS, LANE), F32)],
            axis=0)


def _scan_kernel(fblk, bblk, first, last, s0idx, soidx, logit_ref, *refs):
    del fblk, bblk, s0idx, soidx
    views = (refs[0:9], refs[9:18])
    s0r_ref, s0g_ref = refs[18], refs[19]
    o_refs = ((refs[20], refs[22]), (refs[21], refs[23]))
    sro_ref, sgo_ref = refs[24], refs[25]
    sr_ref, sg_ref = refs[26], refs[27]
    t = pl.program_id(0)

    @pl.when(first[t] == 1)
    def _():
        sr_ref[...] = s0r_ref[...]
        sg_ref[...] = s0g_ref[...]

    @pl.when(first[t] == 2)
    def _():
        sr_ref[...] = jnp.zeros(sr_ref.shape, F32)
        sg_ref[...] = jnp.zeros(sg_ref.shape, F32)

    chains = [(d, h) for d in range(2) for h in range(N_HEADS)]
    s_gdn = {c: sg_ref[c[0], c[1]] for c in chains}
    s_ret = {c: sr_ref[c[0], c[1]] for c in chains}
    ret_decay = {(d, h): jnp.exp(-_softplus(-jnp.full((1, LANE), logit_ref[d * N_HEADS + h], F32)) * float(CHUNK))
                 for d, h in chains}
    n_sub = views[0][0].shape[0]
    for sub in range(n_sub):
        at = (sub, n_sub - 1 - sub)

        def rows(d, k, h):
            return views[d][k][at[d], h * CHUNK:(h + 1) * CHUNK, :]

        ws = {(d, h): _dot(views[d][5][at[d], h], s_gdn[d, h].astype(BF16)) for d, h in chains}
        inter = {(d, h): _dot(rows(d, 1, h), s_ret[d, h].astype(BF16)) for d, h in chains}
        kv = {(d, h): _dot_tn(rows(d, 2, h), rows(d, 3, h)) for d, h in chains}
        v_new = {(d, h): (rows(d, 4, h) - ws[d, h][:CHUNK]).astype(BF16) for d, h in chains}
        intra = {(d, h): _dot(views[d][6][at[d], h], v_new[d, h]) for d, h in chains}
        kvn = {(d, h): _dot_tn(rows(d, 7, h), v_new[d, h]) for d, h in chains}
        for d, h in chains:
            tok = slice(at[d] * CHUNK, (at[d] + 1) * CHUNK)
            lanes = slice(h * LANE, (h + 1) * LANE)
            o_refs[d][0][tok, lanes] = rows(d, 0, h) + inter[d, h]
            o_refs[d][1][tok, lanes] = ws[d, h][CHUNK:] + intra[d, h]
            s_ret[d, h] = s_ret[d, h] * ret_decay[d, h] + kv[d, h]
            s_gdn[d, h] = s_gdn[d, h] * views[d][8][at[d], h:h + 1, :] + kvn[d, h]
    for d, h in chains:
        sr_ref[d, h] = s_ret[d, h]
        sg_ref[d, h] = s_gdn[d, h]

    @pl.when(last[t] == 1)
    def _():
        sro_ref[...] = sr_ref[...]
        sgo_ref[...] = sg_ref[...]


def _scan_tables(lay):
    fblk, bblk, first, last, s0idx, soidx = [], [], [], [], [], []
    blk = SCAN_CHUNKS * CHUNK
    assert lay.seq % blk == 0 and lay.dec_seq % blk == 0
    cp, cs = lay.seq // blk, lay.dec_seq // blk
    for b in range(lay.dec_batch):
        base = lay.n_p // blk + b * cs
        for s in range(cs):
            fblk.append(base + s); bblk.append(base + cs - 1 - s)
            first.append(1 if s == 0 else 0); last.append(0); s0idx.append(b); soidx.append(0)
    for b in range(lay.batch):
        base = b * cp
        for s in range(cp):
            fblk.append(base + s); bblk.append(base + cp - 1 - s)
            first.append(2 if s == 0 else 0); last.append(1 if s == cp - 1 else 0)
            s0idx.append(lay.dec_batch - 1); soidx.append(b)
    return [jnp.asarray(np.asarray(a, np.int32)) for a in (fblk, bblk, first, last, s0idx, soidx)]


def _scan_prep(lay, rq, rk, rv, cq, ck, cv, gb, gt, logit):
    n_chunks = lay.n // CHUNK
    nc = PREP_CHUNKS
    assert n_chunks % nc == 0
    width = N_HEADS * LANE

    def tok(wd):
        return pl.BlockSpec((nc * CHUNK, wd), lambda i: (i, 0))

    def per_dir(shape, dtype):
        spec = pl.BlockSpec((nc, 2) + shape, lambda i: (i,) + (0,) * (1 + len(shape)))
        return spec, jax.ShapeDtypeStruct((n_chunks, 2) + shape, dtype)

    outs = [
        per_dir((STACK, LANE), F32),
        per_dir((STACK, LANE), BF16),
        per_dir((STACK, LANE), BF16),
        (pl.BlockSpec((nc, STACK, LANE), lambda i: (i, 0, 0)), jax.ShapeDtypeStruct((n_chunks, STACK, LANE), BF16)),
        per_dir((STACK, LANE), F32),
        per_dir((N_HEADS, 2 * CHUNK, LANE), BF16),
        per_dir((N_HEADS, CHUNK, CHUNK), BF16),
        per_dir((STACK, LANE), BF16),
        per_dir((SUBLANE, LANE), F32),
    ]
    return pl.pallas_call(
        functools.partial(_prep_kernel, chunks=nc),
        grid=(n_chunks // nc,),
        in_specs=[pl.BlockSpec(memory_space=pltpu.SMEM)] + [tok(width)] * 6
                 + [tok(LANE), pl.BlockSpec((nc, 4 * GDN_HEADS, CHUNK), lambda i: (i, 0, 0))],
        out_specs=[o[0] for o in outs],
        out_shape=[o[1] for o in outs],
        compiler_params=_params("parallel"),
        name="scan_prep",
    )(logit.reshape(-1).astype(F32), rq, rk, rv, cq, ck, cv, gb, gt)


def _even_scan(lay, prep, logit, s0_ret, s0_gdn):
    tables = _scan_tables(lay)
    steps = int(tables[0].shape[0])
    width = N_HEADS * LANE

    def view(which):
        def blk(arr, with_dir=True):
            shape = arr.shape[2:] if with_dir else arr.shape[1:]
            if with_dir:
                return pl.BlockSpec((SCAN_CHUNKS, None) + shape,
                                    lambda t, *tb: (tb[which][t], which) + (0,) * len(shape))
            return pl.BlockSpec((SCAN_CHUNKS,) + shape, lambda t, *tb: (tb[which][t],) + (0,) * len(shape))
        return [blk(a, with_dir=(i != 3)) for i, a in enumerate(prep)]

    state_shape = (2, N_HEADS, LANE, LANE)
    s0_spec = pl.BlockSpec((None,) + state_shape, lambda t, *tb: (tb[4][t], 0, 0, 0, 0))
    so_spec = pl.BlockSpec((None,) + state_shape, lambda t, *tb: (tb[5][t], 0, 0, 0, 0))
    out_f = pl.BlockSpec((SCAN_CHUNKS * CHUNK, width), lambda t, *tb: (tb[0][t], 0))
    out_b = pl.BlockSpec((SCAN_CHUNKS * CHUNK, width), lambda t, *tb: (tb[1][t], 0))
    o_sds = jax.ShapeDtypeStruct((lay.n, width), F32)
    so_sds = jax.ShapeDtypeStruct((lay.batch,) + state_shape, F32)
    args = list(prep)
    return pl.pallas_call(
        _scan_kernel,
        grid_spec=pltpu.PrefetchScalarGridSpec(
            num_scalar_prefetch=6,
            grid=(steps,),
            in_specs=[pl.BlockSpec(memory_space=pltpu.SMEM)] + view(0) + view(1) + [s0_spec, s0_spec],
            out_specs=[out_f, out_b, out_f, out_b, so_spec, so_spec],
            scratch_shapes=[pltpu.VMEM(state_shape, F32), pltpu.VMEM(state_shape, F32)],
        ),
        out_shape=[o_sds, o_sds, o_sds, o_sds, so_sds, so_sds],
        compiler_params=_params("arbitrary"),
        name="even_scan",
    )(*tables, logit.reshape(-1).astype(F32), *args, *args, s0_ret, s0_gdn)


def _even_out_kernel(x_ref, mod_ref, rf_ref, rb_ref, gf_ref, gb_ref, rg_ref, gg_ref, rnw_ref, gnw_ref, w_ref, o_ref):
    ret = _head_rmsnorm(rf_ref[...] + rb_ref[...], rnw_ref[...], RET_HEADS) * _silu(rg_ref[...])
    gdn = _head_rmsnorm(gf_ref[...] + gb_ref[...], gnw_ref[...], GDN_HEADS) * _silu(gg_ref[...])
    y = _dot(jnp.concatenate([ret, gdn], axis=1).astype(BF16), w_ref[...])
    o_ref[...] = x_ref[...] + mod_ref[5:6, :] * y


def _even_out(lay, x, mods, orf, orb, ogf, ogb, rg, gg, rnw, gnw, w):
    return pl.pallas_call(
        _even_out_kernel,
        grid=(lay.tiles,),
        in_specs=[lay.tok(D_MODEL), lay.mod()] + [lay.tok(RET_V)] * 6
                 + [_resident((1, LANE)), _resident((1, LANE)), _resident(w.shape)],
        out_specs=lay.tok(D_MODEL),
        out_shape=jax.ShapeDtypeStruct((lay.n, D_MODEL), F32),
        compiler_params=_params("parallel"),
        name="even_out",
    )(x, mods, orf, orb, ogf, ogb, rg, gg, rnw.reshape(1, LANE), gnw.reshape(1, LANE), w)


def _odd_proj_kernel(x_ref, mod_ref, nw_ref, w_ref, qnw_ref, knw_ref, cos_ref, sin_ref,
                     q_ref, kd_ref, vt_ref, ks_ref, vs_ref, *, p_tiles):
    h = _adaln(x_ref[...], nw_ref[...], mod_ref[3:4, :], mod_ref[4:5, :]).astype(BF16)
    tm = h.shape[0]
    lane = lax.broadcasted_iota(jnp.int32, (tm, LANE), 1)
    lo = lane < ATT_HD
    first16 = (lane & (2 * ROPE_PAIRS - 1)) < ROPE_PAIRS
    cos, sin = cos_ref[...], sin_ref[...]

    def norm_rope(xs, w2):
        sq = xs * xs
        ms_lo = jnp.sum(jnp.where(lo, sq, 0.0), axis=-1, keepdims=True) * (1.0 / ATT_HD)
        ms_hi = jnp.sum(jnp.where(lo, 0.0, sq), axis=-1, keepdims=True) * (1.0 / ATT_HD)
        xn = xs * jnp.where(lo, lax.rsqrt(ms_lo + EPS), lax.rsqrt(ms_hi + EPS)) * w2
        partner = jnp.where(first16, pltpu.roll(xn, LANE - ROPE_PAIRS, 1), pltpu.roll(xn, ROPE_PAIRS, 1))
        return xn * cos + partner * sin

    def both_halves(xs, j):
        swapped = pltpu.roll(xs, ATT_HD, 1)
        return jnp.where(lo, xs, swapped) if j % 2 == 0 else jnp.where(lo, swapped, xs)

    for t in range(ATT_Q // MXU_TILE):
        y = _dot(h, w_ref[:, t * MXU_TILE:(t + 1) * MXU_TILE])
        for p in (2 * t, 2 * t + 1):
            qs = norm_rope(y[:, (p % 2) * LANE:(p % 2 + 1) * LANE], qnw_ref[...])
            q_ref[:, p * LANE:(p + 1) * LANE] = (qs * Q_SCALE).astype(BF16)
    is_prompt = pl.program_id(0) < p_tiles
    assert ATT_KV == MXU_TILE
    yk = _dot(h, w_ref[:, ATT_Q:ATT_Q + ATT_KV])
    yv = _dot(h, w_ref[:, ATT_Q + ATT_KV:ATT_Q + 2 * ATT_KV])
    for p in range(ATT_KV // LANE):
        ks = norm_rope(yk[:, p * LANE:(p + 1) * LANE], knw_ref[...])
        vs = yv[:, p * LANE:(p + 1) * LANE]
        vst = vs.T.astype(BF16)
        for j in (2 * p, 2 * p + 1):
            kd_ref[j] = both_halves(ks, j).astype(BF16)
            for c in range(tm // KEY_CHUNK):
                vt_ref[j, c] = vst[(j % 2) * ATT_HD:(j % 2 + 1) * ATT_HD, c * KEY_CHUNK:(c + 1) * KEY_CHUNK]

        @pl.when(is_prompt)
        def _():
            ks_ref[:, p * LANE:(p + 1) * LANE] = ks
            vs_ref[:, p * LANE:(p + 1) * LANE] = vs


def _rope_tables(lay):
    t = jnp.arange(lay.dec_seq)
    inv = ROPE_THETA ** (-jnp.arange(ROPE_PAIRS, dtype=F32) / ROPE_PAIRS)
    ar = (t // GRID_W).astype(F32)[:, None] * inv
    ac = (t % GRID_W).astype(F32)[:, None] * inv
    cos = jnp.concatenate([jnp.cos(ar), jnp.cos(ar), jnp.cos(ac), jnp.cos(ac)], axis=1)
    sin = jnp.concatenate([-jnp.sin(ar), jnp.sin(ar), -jnp.sin(ac), jnp.sin(ac)], axis=1)
    cos = jnp.concatenate([jnp.tile(cos, (1, 2)), jnp.ones((lay.tm, LANE), F32)], axis=0)
    sin = jnp.concatenate([jnp.tile(sin, (1, 2)), jnp.zeros((lay.tm, LANE), F32)], axis=0)
    return cos, sin


def _odd_proj(lay, x, mods, nw, w, qnw, knw):
    cos, sin = _rope_tables(lay)

    def table_block(i):
        return jnp.where(i < lay.p_tiles, lay.tiles_per_seq, lax.rem(jnp.maximum(i - lay.p_tiles, 0), lay.tiles_per_seq))

    tab = pl.BlockSpec((lay.tm, LANE), lambda i: (table_block(i), 0))
    std = pl.BlockSpec((lay.tm, ATT_KV), lambda i: (jnp.minimum(i, lay.p_tiles - 1), 0))
    tm = lay.tm
    return pl.pallas_call(
        functools.partial(_odd_proj_kernel, p_tiles=lay.p_tiles),
        grid=(lay.tiles,),
        in_specs=[lay.tok(D_MODEL), lay.mod(), _resident((1, D_MODEL)), _resident(w.shape),
                  _resident((1, LANE)), _resident((1, LANE)), tab, tab],
        out_specs=[
            lay.tok(ATT_Q),
            pl.BlockSpec((ATT_KV_HEADS, tm, LANE), lambda i: (0, i, 0)),
            pl.BlockSpec((ATT_KV_HEADS, tm // KEY_CHUNK, ATT_HD, KEY_CHUNK), lambda i: (0, i, 0, 0)),
            std, std,
        ],
        out_shape=[
            jax.ShapeDtypeStruct((lay.n, ATT_Q), BF16),
            jax.ShapeDtypeStruct((ATT_KV_HEADS, lay.n, LANE), BF16),
            jax.ShapeDtypeStruct((ATT_KV_HEADS, lay.n // KEY_CHUNK, ATT_HD, KEY_CHUNK), BF16),
            jax.ShapeDtypeStruct((lay.n_p, ATT_KV), F32),
            jax.ShapeDtypeStruct((lay.n_p, ATT_KV), F32),
        ],
        compiler_params=_params("arbitrary"),
        name="odd_proj",
    )(x, mods, nw.reshape(1, D_MODEL), w, jnp.tile(qnw.reshape(1, ATT_HD), (1, 2)),
      jnp.tile(knw.reshape(1, ATT_HD), (1, 2)), cos, sin)


def _attn_kernel(*refs, cached):
    if cached:
        zero_ref, q_ref, kn_ref, vn_ref, kc_ref, vc_ref, _, o_ref, *s_scr = refs
    else:
        zero_ref, q_ref, kn_ref, vn_ref, o_ref, *s_scr = refs
    z = zero_ref[0]
    tq = MXU_TILE
    n_sub = q_ref.shape[0] // tq
    ck = KEY_CHUNK
    n_c = kc_ref.shape[0] // ck if cached else 0
    n_n = kn_ref.shape[0] // ck
    lo = lax.broadcasted_iota(jnp.int32, (tq, LANE), 1) < ATT_HD
    qms = []
    for s in range(n_sub):
        for g in range(ATT_GROUP):
            q2 = q_ref[s * tq:(s + 1) * tq, (g // 2) * LANE:(g // 2 + 1) * LANE]
            qms.append(jnp.where(lo, q2, jnp.zeros_like(q2)) if g % 2 == 0 else jnp.where(lo, jnp.zeros_like(q2), q2))
    n_units = len(qms)

    ones_rows = jnp.where(lax.broadcasted_iota(jnp.int32, (2 * SUBLANE, ck), 0) == 0, 1.0, 0.0).astype(BF16)
    m_fin = [None] * n_units
    outs = [None] * n_units
    for ph in range(n_units + 1):
        ga = ph if ph < n_units else None
        gb = ph - 1 if ph >= 1 else None
        ma = jnp.full((1, tq), -jnp.inf, F32)
        acc = jnp.zeros((ATT_HD + 2 * SUBLANE, tq), F32)
        for c in range(n_c + n_n):
            if c < n_c:
                k_blk, vt_blk = kc_ref[c * ck:(c + 1) * ck, :], vc_ref[c]
            else:
                k_blk, vt_blk = kn_ref[(c - n_c) * ck:(c - n_c + 1) * ck, :], vn_ref[c - n_c]
            if ga is not None:
                st = _dot_nt(k_blk, qms[ga])
                s_scr[ga % 2][z + c] = st
                ma = jnp.maximum(ma, jnp.max(st, axis=0, keepdims=True))
            if gb is not None:
                p = jnp.exp2(s_scr[gb % 2][z + c] - m_fin[gb]).astype(BF16)
                acc = acc + _dot(jnp.concatenate([vt_blk, ones_rows], axis=0), p)
        if ga is not None:
            m_fin[ga] = ma
        if gb is not None:
            outs[gb] = acc[:ATT_HD] / acc[ATT_HD:ATT_HD + 1]
    for s in range(n_sub):
        o_ref[s * tq:(s + 1) * tq, :] = jnp.concatenate(outs[s * ATT_GROUP:(s + 1) * ATT_GROUP], axis=0).T


def _attention(lay, q, kd, vt, kc, vc):
    gw = ATT_GROUP * ATT_HD
    o_sds = jax.ShapeDtypeStruct((lay.n, ATT_Q), F32)
    ck = KEY_CHUNK
    assert lay.seq % ck == 0 and lay.dec_seq % ck == 0 and lay.past % ck == 0
    assert lay.seq % MXU_TILE == 0 and lay.dec_seq % MXU_TILE == 0 and ATT_Q_TILE % MXU_TILE == 0

    zero = jnp.zeros((1,), jnp.int32)
    zero_spec = pl.BlockSpec(memory_space=pltpu.SMEM)

    tq = min(ATT_Q_TILE, lay.seq)
    nq = lay.seq // tq
    nck = lay.seq // ck
    out = pl.pallas_call(
        functools.partial(_attn_kernel, cached=False),
        grid=(lay.batch, ATT_KV_HEADS, nq),
        in_specs=[
            zero_spec,
            pl.BlockSpec((tq, gw), lambda b, j, i: (b * nq + i, j)),
            pl.BlockSpec((None, lay.seq, LANE), lambda b, j, i: (j, b, 0)),
            pl.BlockSpec((None, nck, ATT_HD, ck), lambda b, j, i: (j, b, 0, 0)),
        ],
        out_specs=pl.BlockSpec((tq, gw), lambda b, j, i: (b * nq + i, j)),
        out_shape=o_sds,
        scratch_shapes=[pltpu.VMEM((nck, ck, MXU_TILE), F32)] * 2,
        compiler_params=_params("parallel", "parallel", "parallel"),
        name="attn_context",
    )(zero, q, kd, vt)

    tq = min(ATT_Q_TILE, lay.dec_seq)
    nq = lay.dec_seq // tq
    nck = lay.dec_seq // ck
    qbase = lay.n_p // tq
    kbase = lay.n_p // lay.dec_seq
    assert lay.n_p % lay.dec_seq == 0
    return pl.pallas_call(
        functools.partial(_attn_kernel, cached=True),
        grid=(lay.dec_batch, ATT_KV_HEADS, nq),
        in_specs=[
            zero_spec,
            pl.BlockSpec((tq, gw), lambda b, j, i: (qbase + b * nq + i, j)),
            pl.BlockSpec((None, lay.dec_seq, LANE), lambda b, j, i: (j, kbase + b, 0)),
            pl.BlockSpec((None, nck, ATT_HD, ck), lambda b, j, i: (j, kbase + b, 0, 0)),
            pl.BlockSpec((None, None, lay.past, LANE), lambda b, j, i: (b, j, 0, 0)),
            pl.BlockSpec((None, None, lay.past // ck, ATT_HD, ck), lambda b, j, i: (b, j, 0, 0, 0)),
            pl.BlockSpec(memory_space=pl.ANY),
        ],
        out_specs=pl.BlockSpec((tq, gw), lambda b, j, i: (qbase + b * nq + i, j)),
        out_shape=o_sds,
        scratch_shapes=[pltpu.VMEM((lay.past // ck + nck, ck, MXU_TILE), F32)] * 2,
        input_output_aliases={6: 0},
        compiler_params=_params("parallel", "parallel", "parallel"),
        name="attn_latent",
    )(zero, q, kd, vt, kc, vc, out)


def _odd_out_kernel(x_ref, mod_ref, a_ref, w_ref, o_ref):
    o_ref[...] = x_ref[...] + mod_ref[5:6, :] * _dot(a_ref[...].astype(BF16), w_ref[...])


def _odd_out(lay, x, mods, a, w):
    return pl.pallas_call(
        _odd_out_kernel,
        grid=(lay.tiles,),
        in_specs=[lay.tok(D_MODEL), lay.mod(), lay.tok(ATT_Q), _resident(w.shape)],
        out_specs=lay.tok(D_MODEL),
        out_shape=jax.ShapeDtypeStruct((lay.n, D_MODEL), F32),
        compiler_params=_params("parallel"),
        name="odd_out",
    )(x, mods, a, w)


def _pad_state(s):
    return jnp.pad(s, [(0, 0)] * (s.ndim - 2) + [(0, LANE - s.shape[-2]), (0, 0)])


def _cache_keys(c):
    return jnp.tile(c.transpose(0, 2, 1, 3), (1, 1, 1, 2)).astype(BF16)


def _cache_values(c):
    b, t, kvh, hd = c.shape
    return c.reshape(b, t // KEY_CHUNK, KEY_CHUNK, kvh, hd).transpose(0, 3, 1, 4, 2).astype(BF16)


def kernel(x_prompt, x_sample, state_ret, state_gdn, cache_k, cache_v, c, c_ctx,
           mod_w, mod_b, norm_w, ffn_w_in, ffn_w_out, even_w_in, even_w_out,
           ret_decay_logit, ret_norm_w, gdn_conv_w, gdn_A_log, gdn_dt_bias, gdn_norm_w,
           odd_w_in, odd_w_out, q_norm_w, k_norm_w, final_norm_w):
    batch, seq, d = x_prompt.shape
    dec_batch, dec_seq, _ = x_sample.shape
    depth = mod_w.shape[0]
    lay = _Layout(batch, seq, dec_batch, dec_seq, cache_k.shape[2])

    x = (x_prompt.reshape(lay.n_p, d), x_sample.reshape(lay.n_s, d))
    n_cond = -(-(1 + dec_batch) // (2 * SUBLANE)) * (2 * SUBLANE)
    cond = jnp.zeros((n_cond, d), F32).at[0].set(c_ctx).at[1:1 + dec_batch].set(c)
    mods = _modulation(cond, mod_w, mod_b)

    new_ret, new_gdn, new_k, new_v = [], [], [], []
    for l in range(depth):
        m = mods[l]
        last = l == depth - 1
        x = _ffn(lay, x, m, norm_w[l, 0], ffn_w_in[l, 0].astype(BF16), ffn_w_out[l, 0].astype(BF16), 0)
        if l % 2 == 0:
            e = l // 2
            rq, rk, rv, rg, gqkv, gg, gab = _even_proj(lay, x, m, norm_w[l, 1], _even_weight(even_w_in[e]))
            cq, ck, cv, gb, gt = _even_conv(lay, gqkv, gab, gdn_conv_w[e], gdn_A_log[e], gdn_dt_bias[e])
            prep = _scan_prep(lay, rq, rk, rv, cq, ck, cv, gb, gt, ret_decay_logit[e])
            orf, orb, ogf, ogb, sr, sg = _even_scan(
                lay, prep, ret_decay_logit[e], _pad_state(state_ret[:, e]), state_gdn[:, e])
            new_ret.append(sr[:, :, :, :RET_DK, :])
            new_gdn.append(sg)
            x = _even_out(lay, x, m, orf, orb, ogf, ogb, rg, gg, ret_norm_w[e], gdn_norm_w[e],
                          even_w_out[e].astype(BF16))
        else:
            o = l // 2
            q, kd, vt, ks, vs = _odd_proj(lay, x, m, norm_w[l, 1], odd_w_in[o].astype(BF16), q_norm_w[o], k_norm_w[o])
            a = _attention(lay, q, kd, vt, _cache_keys(cache_k[:, o]), _cache_values(cache_v[:, o]))
            new_k.append(ks.reshape(batch, seq, ATT_KV_HEADS, ATT_HD))
            new_v.append(vs.reshape(batch, seq, ATT_KV_HEADS, ATT_HD))
            x = _odd_out(lay, x, m, a, odd_w_out[o].astype(BF16))
        x = _ffn(lay, x, m, norm_w[l, 2], ffn_w_in[l, 1].astype(BF16), ffn_w_out[l, 1].astype(BF16), 2,
                 final_w=final_norm_w if last else None)

    y_prompt = x[0].reshape(batch, seq, d)
    y_sample = x[1].reshape(dec_batch, dec_seq, d)
    return (y_prompt, y_sample, jnp.stack(new_ret, axis=1), jnp.stack(new_gdn, axis=1),
            jnp.stack(new_k, axis=1), jnp.stack(new_v, axis=1))
```

```python
import functools

import numpy as np
import jax
import jax.numpy as jnp
from jax import lax
from jax.experimental import pallas as pl
from jax.experimental.pallas import tpu as pltpu

F32 = jnp.float32
BF16 = jnp.bfloat16

D_MODEL = 1024
GRID_W = 64
RET_HEADS = 4
RET_DK = 64
RET_DV = 128
GDN_HEADS = 4
GDN_DK = 128
GDN_DV = 128
CHUNK = 64
ATT_HEADS = 16
ATT_KV_HEADS = 4
ATT_HD = 64
ATT_GROUP = ATT_HEADS // ATT_KV_HEADS
ROPE_THETA = 10000.0
ROPE_PAIRS = ATT_HD // 4
FFN_HIDDEN = 2816
N_MOD = 9
EPS = 1e-6

RET_QK = RET_HEADS * RET_DK
RET_V = RET_HEADS * RET_DV
GDN_QK = GDN_HEADS * GDN_DK
GDN_V = GDN_HEADS * GDN_DV
ATT_Q = ATT_HEADS * ATT_HD
ATT_KV = ATT_KV_HEADS * ATT_HD

LANE = 128
SUBLANE = 8
V7X_VMEM_BYTES = 64 * 1024 * 1024
VMEM_LIMIT = V7X_VMEM_BYTES - 8 * 1024 * 1024

TOKEN_TILE = 512
FFN_TILE = 1024
CONV_TILE = 256
ATT_Q_TILE = 1024
KEY_CHUNK = 256
Q_SCALE = ATT_HD ** -0.5 * float(np.log2(np.e))
MXU_TILE = 256
FFN_CHUNKS = ((0, 6 * MXU_TILE), (6 * MXU_TILE, FFN_HIDDEN))
HEAD_PAD = LANE
PREP_CHUNKS = 4
SCAN_CHUNKS = 4


def _params(*sem):
    return pltpu.CompilerParams(dimension_semantics=sem, vmem_limit_bytes=VMEM_LIMIT)


def _resident(shape):
    nd = len(shape)
    return pl.BlockSpec(shape, lambda *_: (0,) * nd, pipeline_mode=pl.Buffered(1))


def _resident_at(arr, lead):
    shape = arr.shape[len(lead):]
    return pl.BlockSpec((None,) * len(lead) + shape, lambda *_: tuple(lead) + (0,) * len(shape),
                        pipeline_mode=pl.Buffered(1))


def _dot(a, b):
    return jnp.dot(a, b, preferred_element_type=F32)


def _dot_nt(a, b):
    return lax.dot_general(a, b, (((1,), (1,)), ((), ())), preferred_element_type=F32)


def _dot_tn(a, b):
    return lax.dot_general(a, b, (((0,), (0,)), ((), ())), preferred_element_type=F32)


def _split2(a):
    hi = a.astype(BF16)
    lo = (a - hi.astype(F32)).astype(BF16)
    return hi, lo


def _dot3(a, b):
    ah, al = _split2(a)
    bh, bl = _split2(b)
    return _dot(ah, bh) + _dot(ah, bl) + _dot(al, bh)


def _silu(x):
    return x * jax.nn.sigmoid(x)


def _softplus(x):
    return jnp.maximum(x, 0.0) + jnp.log(1.0 + jnp.exp(-jnp.abs(x)))


def _adaln(x, nw, shift, scale):
    ms = jnp.mean(x * x, axis=-1, keepdims=True)
    return (x * lax.rsqrt(ms + EPS)) * nw * (1.0 + scale) + shift


def _head_rmsnorm(x, w, n_heads):
    parts = []
    for h in range(n_heads):
        xs = x[:, h * LANE:(h + 1) * LANE]
        ms = jnp.mean(xs * xs, axis=-1, keepdims=True)
        parts.append(xs * lax.rsqrt(ms + EPS) * w)
    return jnp.concatenate(parts, axis=1)


def _mod_kernel(cond_ref, w_ref, b_ref, o_ref):
    c = cond_ref[...]
    o_ref[0] = _dot(_silu(c).astype(BF16), w_ref[0].astype(BF16)) + b_ref[0]


def _modulation(cond, mod_w, mod_b):
    depth, d, n = mod_w.shape
    r = cond.shape[0]
    tn = n // 8
    out = pl.pallas_call(
        _mod_kernel,
        grid=(depth, n // tn),
        in_specs=[
            pl.BlockSpec((r, d), lambda l, j: (0, 0)),
            pl.BlockSpec((1, d, tn), lambda l, j: (l, 0, j)),
            pl.BlockSpec((1, 1, tn), lambda l, j: (l, 0, j)),
        ],
        out_specs=pl.BlockSpec((1, r, tn), lambda l, j: (l, 0, j)),
        out_shape=jax.ShapeDtypeStruct((depth, r, n), F32),
        compiler_params=_params("parallel", "parallel"),
        name="modulation",
    )(cond, mod_w, mod_b.reshape(depth, 1, n))
    return out.reshape(depth, r, N_MOD, d)


class _Layout:
    def __init__(self, batch, seq, dec_batch, dec_seq, past, tile=TOKEN_TILE):
        self.batch, self.seq, self.dec_batch, self.dec_seq, self.past = batch, seq, dec_batch, dec_seq, past
        self.n_p = batch * seq
        self.n_s = dec_batch * dec_seq
        self.n = self.n_p + self.n_s
        self.tm = min(tile, self.n_p, dec_seq)
        assert self.n_p % self.tm == 0 and dec_seq % self.tm == 0
        self.tiles = self.n // self.tm
        self.p_tiles = self.n_p // self.tm
        self.tiles_per_seq = dec_seq // self.tm

    def group(self, i):
        return jnp.where(i < self.p_tiles, 0, 1 + (i - self.p_tiles) // self.tiles_per_seq)

    def tok(self, width, tm=None):
        tm = tm or self.tm
        return pl.BlockSpec((tm, width), lambda i: (i, 0))

    def mod(self):
        return pl.BlockSpec((None, N_MOD, D_MODEL), lambda i: (self.group(i), 0, 0))


def _ffn_kernel(*refs, j, first, final, p_tiles):
    refs = list(refs)
    x_refs = [refs.pop(0) for _ in range(2 if first else 1)]
    mod_ref, nw_ref, win_ref, wout_ref = (refs.pop(0) for _ in range(4))
    fw_ref = refs.pop(0) if final else None
    o_refs = refs
    is_prompt = pl.program_id(0) < p_tiles
    x = jnp.where(is_prompt, x_refs[0][...], x_refs[1][...]) if first else x_refs[0][...]
    h = _adaln(x, nw_ref[...], mod_ref[3 * j:3 * j + 1, :], mod_ref[3 * j + 1:3 * j + 2, :]).astype(BF16)
    acc = None
    for lo, hi in FFN_CHUNKS:
        a = _dot(h, win_ref[:, lo:hi])
        b = _dot(h, win_ref[:, FFN_HIDDEN + lo:FFN_HIDDEN + hi])
        y = _dot((_silu(a) * b).astype(BF16), wout_ref[lo:hi, :])
        acc = y if acc is None else acc + y
    out = x + (0.5 * mod_ref[3 * j + 2:3 * j + 3, :]) * acc
    if not final:
        o_refs[0][...] = out
        return
    ms = jnp.mean(out * out, axis=-1, keepdims=True)
    out = out * lax.rsqrt(ms + EPS) * fw_ref[...]

    @pl.when(is_prompt)
    def _():
        o_refs[0][...] = out

    @pl.when(jnp.logical_not(is_prompt))
    def _():
        o_refs[1][...] = out


def _ffn(lay, xs, mods, nw, w_in, w_out, which, j, final_w=None):
    first = isinstance(xs, tuple)
    final = final_w is not None
    prompt_rows = pl.BlockSpec((lay.tm, D_MODEL), lambda i: (jnp.minimum(i, lay.p_tiles - 1), 0))
    latent_rows = pl.BlockSpec((lay.tm, D_MODEL), lambda i: (jnp.maximum(i - lay.p_tiles, 0), 0))
    ins = list(xs) if first else [xs]
    specs = [prompt_rows, latent_rows] if first else [lay.tok(D_MODEL)]
    ins += [mods, nw.reshape(1, D_MODEL), w_in, w_out]
    specs += [lay.mod(), _resident((1, D_MODEL)), _resident_at(w_in, which), _resident_at(w_out, which)]
    if final:
        ins.append(final_w.reshape(1, D_MODEL))
        specs.append(_resident((1, D_MODEL)))
        out_specs = [prompt_rows, latent_rows]
        out_shape = [jax.ShapeDtypeStruct((lay.n_p, D_MODEL), F32), jax.ShapeDtypeStruct((lay.n_s, D_MODEL), F32)]
    else:
        out_specs = lay.tok(D_MODEL)
        out_shape = jax.ShapeDtypeStruct((lay.n, D_MODEL), F32)
    return pl.pallas_call(
        functools.partial(_ffn_kernel, j=j, first=first, final=final, p_tiles=lay.p_tiles),
        grid=(lay.tiles,),
        in_specs=specs,
        out_specs=out_specs,
        out_shape=out_shape,
        compiler_params=_params("arbitrary"),
        name="ffn_final" if final else ("ffn_first" if first else "ffn"),
    )(*ins)


EVEN_WIDTHS = (RET_HEADS * HEAD_PAD, RET_HEADS * HEAD_PAD, RET_V, RET_V, 2 * GDN_QK + GDN_V, GDN_V, LANE)


def _proj_kernel(x_ref, mod_ref, nw_ref, w_ref, *o_refs, j, widths):
    h = _adaln(x_ref[...], nw_ref[...], mod_ref[3 * j:3 * j + 1, :], mod_ref[3 * j + 1:3 * j + 2, :]).astype(BF16)
    off = 0
    for o_ref, wd in zip(o_refs, widths):
        o_ref[...] = _dot(h, w_ref[:, off:off + wd])
        off += wd


def _even_proj(lay, x, mods, nw, w):
    return pl.pallas_call(
        functools.partial(_proj_kernel, j=1, widths=EVEN_WIDTHS),
        grid=(lay.tiles,),
        in_specs=[lay.tok(D_MODEL), lay.mod(), _resident((1, D_MODEL)), _resident(w.shape)],
        out_specs=[lay.tok(wd) for wd in EVEN_WIDTHS],
        out_shape=[jax.ShapeDtypeStruct((lay.n, wd), F32) for wd in EVEN_WIDTHS],
        compiler_params=_params("parallel"),
        name="even_proj",
    )(x, mods, nw.reshape(1, D_MODEL), w)


def _even_weight(w):
    d = w.shape[0]
    o = 0
    rq = w[:, o:o + RET_QK]; o += RET_QK
    rk = w[:, o:o + RET_QK]; o += RET_QK
    rest = w[:, o:o + 2 * RET_V + 2 * GDN_QK + 2 * GDN_V]; o += 2 * RET_V + 2 * GDN_QK + 2 * GDN_V
    gab = w[:, o:]

    def pad_heads(m):
        m = m.reshape(d, RET_HEADS, RET_DK)
        return jnp.pad(m, ((0, 0), (0, 0), (0, HEAD_PAD - RET_DK))).reshape(d, RET_HEADS * HEAD_PAD)

    gab = jnp.pad(gab, ((0, 0), (0, LANE - gab.shape[1])))
    return jnp.concatenate([pad_heads(rq), pad_heads(rk * RET_DK ** -0.5), rest, gab], axis=1).astype(BF16)


def _conv_kernel(x_ref, prev_ref, next_ref, cw_ref, gab_ref, alog_ref, dt_ref,
                 cq_ref, ck_ref, cv_ref, gb_ref, gt_ref, *, n_p, seq, dec_seq):
    tc = x_ref.shape[0]
    start = pl.program_id(0) * tc
    in_prompt = start < n_p
    rel = jnp.where(in_prompt, start, start - n_p)
    slen = jnp.where(in_prompt, seq, dec_seq)
    pos = lax.rem(rel, slen)
    x = x_ref[...]
    prev_row = jnp.where(pos == 0, 0.0, prev_ref[SUBLANE - 1:SUBLANE, :])
    next_row = jnp.where(pos + tc == slen, 0.0, next_ref[0:1, :])
    row = lax.broadcasted_iota(jnp.int32, x.shape, 0)
    xm1 = jnp.where(row == 0, prev_row, pltpu.roll(x, 1, 0))
    xp1 = jnp.where(row == tc - 1, next_row, pltpu.roll(x, tc - 1, 0))
    y = _silu(xm1 * cw_ref[0:1, :] + x * cw_ref[1:2, :] + xp1 * cw_ref[2:3, :])
    for h in range(GDN_HEADS):
        q = y[:, h * LANE:(h + 1) * LANE]
        k = y[:, GDN_QK + h * LANE:GDN_QK + (h + 1) * LANE]
        cq_ref[:, h * LANE:(h + 1) * LANE] = (
            q * lax.rsqrt(jnp.sum(q * q, axis=-1, keepdims=True) + EPS) * GDN_DK ** -0.5)
        ck_ref[:, h * LANE:(h + 1) * LANE] = k * lax.rsqrt(jnp.sum(k * k, axis=-1, keepdims=True) + EPS)
    cv_ref[...] = y[:, 2 * GDN_QK:]
    g = gab_ref[...]
    lane = lax.broadcasted_iota(jnp.int32, g.shape, 1)
    gb = jnp.where(lane < 2 * GDN_HEADS, -jnp.exp(alog_ref[...]) * _softplus(g + dt_ref[...]), jax.nn.sigmoid(g))
    gb_ref[...] = gb
    gbt = gb.T
    for c in range(tc // CHUNK):
        gt_ref[c] = gbt[0:4 * GDN_HEADS, c * CHUNK:(c + 1) * CHUNK]


def _even_conv(lay, gqkv, gab, conv_w, a_log, dt_bias):
    tc = min(CONV_TILE, lay.seq, lay.dec_seq)
    assert lay.seq % tc == 0 and lay.dec_seq % tc == 0 and tc % LANE == 0
    width = gqkv.shape[1]
    nblk8 = lay.n // SUBLANE
    r = tc // SUBLANE

    def pad_row(v):
        v = v.reshape(1, -1).astype(F32)
        return jnp.pad(v, ((0, 0), (0, LANE - v.shape[1])))

    return pl.pallas_call(
        functools.partial(_conv_kernel, n_p=lay.n_p, seq=lay.seq, dec_seq=lay.dec_seq),
        grid=(lay.n // tc,),
        in_specs=[
            pl.BlockSpec((tc, width), lambda i: (i, 0)),
            pl.BlockSpec((SUBLANE, width), lambda i: (jnp.maximum(i * r - 1, 0), 0)),
            pl.BlockSpec((SUBLANE, width), lambda i: (jnp.minimum((i + 1) * r, nblk8 - 1), 0)),
            _resident(conv_w.shape),
            pl.BlockSpec((tc, LANE), lambda i: (i, 0)),
            _resident((1, LANE)),
            _resident((1, LANE)),
        ],
        out_specs=[
            pl.BlockSpec((tc, GDN_QK), lambda i: (i, 0)),
            pl.BlockSpec((tc, GDN_QK), lambda i: (i, 0)),
            pl.BlockSpec((tc, GDN_V), lambda i: (i, 0)),
            pl.BlockSpec((tc, LANE), lambda i: (i, 0)),
            pl.BlockSpec((tc // CHUNK, 4 * GDN_HEADS, CHUNK), lambda i: (i, 0, 0)),
        ],
        out_shape=[
            jax.ShapeDtypeStruct((lay.n, GDN_QK), F32),
            jax.ShapeDtypeStruct((lay.n, GDN_QK), F32),
            jax.ShapeDtypeStruct((lay.n, GDN_V), F32),
            jax.ShapeDtypeStruct((lay.n, LANE), F32),
            jax.ShapeDtypeStruct((lay.n // CHUNK, 4 * GDN_HEADS, CHUNK), F32),
        ],
        compiler_params=_params("parallel"),
        name="even_conv",
    )(gqkv, gqkv, gqkv, conv_w, gab, pad_row(a_log), pad_row(dt_bias))


N_HEADS = 4
STACK = N_HEADS * CHUNK


def _prep_kernel(logit_ref, rq_ref, rk_ref, rv_ref, cq_ref, ck_ref, cv_ref, gb_ref, gt_ref,
                 rin_ref, rqd_ref, rkd_ref, rvb_ref, gu_ref, gwq_ref, gat_ref, gkd_ref, gcd_ref, *, chunks):
    wr = lax.broadcasted_iota(jnp.int32, (CHUNK, STACK), 0)
    wcol = lax.broadcasted_iota(jnp.int32, (CHUNK, STACK), 1)
    wc = wcol & (CHUNK - 1)
    whead = wcol >> 6
    r4 = lax.broadcasted_iota(jnp.int32, (STACK, STACK), 0)
    c4 = lax.broadcasted_iota(jnp.int32, (STACK, STACK), 1)
    same_head = (r4 >> 6) == (c4 >> 6)
    rc = lax.broadcasted_iota(jnp.int32, (CHUNK, CHUNK), 0)
    cc = lax.broadcasted_iota(jnp.int32, (CHUNK, CHUNK), 1)
    rpos = (lax.broadcasted_iota(jnp.int32, (STACK, LANE), 0) & (CHUNK - 1)).astype(F32)
    eye_w = jnp.where(wr == wc, 1.0, 0.0)

    def rows(x, h):
        return x[h * CHUNK:(h + 1) * CHUNK]

    def by_head(parts):
        out = parts[N_HEADS - 1]
        for h in range(N_HEADS - 2, -1, -1):
            out = jnp.where(whead == h, parts[h], out)
        return out

    def block_diag(w):
        return tuple(jnp.where(same_head, jnp.concatenate([x] * N_HEADS, axis=0), jnp.zeros((STACK, STACK), BF16))
                     for x in _split2(w))

    def wide_times(a, bh, bl):
        ah, al = _split2(a)
        hh = _dot(jnp.concatenate([ah, al], axis=0), bh)
        return hh[:CHUNK] + hh[CHUNK:] + _dot(ah, bl)

    ret_const = []
    for d in range(2):
        lg4 = jnp.concatenate(
            [-_softplus(-jnp.full((CHUNK, STACK), logit_ref[d * N_HEADS + h], F32)) for h in range(N_HEADS)], axis=0)
        tri4 = (r4 >= c4) if d == 0 else (r4 <= c4)
        dist4 = ((r4 - c4) if d == 0 else (c4 - r4)).astype(F32)
        dec4 = jnp.exp(jnp.where(same_head, jnp.where(tri4, lg4 * dist4, -jnp.inf), -jnp.inf))
        spos = rpos if d == 0 else (CHUNK - 1.0) - rpos
        lgc = lg4[:, :LANE]
        ret_const.append((dec4, jnp.exp(lgc * (spos + 1.0)), jnp.exp(lgc * ((CHUNK - 1.0) - spos))))

    setup = []
    for n in range(chunks):
        tok = slice(n * CHUNK, (n + 1) * CHUNK)

        def stack(ref):
            return jnp.concatenate([ref[tok, h * LANE:(h + 1) * LANE] for h in range(N_HEADS)], axis=0)

        q4, k4, v4 = stack(rq_ref), stack(rk_ref), stack(rv_ref)
        q4b, k4b, v4b = q4.astype(BF16), k4.astype(BF16), v4.astype(BF16)
        rvb_ref[n] = v4b
        qk_ret = _dot_nt(q4b, k4b)
        gq4, gk4, gv4 = stack(cq_ref), stack(ck_ref), stack(cv_ref)
        gq4b, gk4b = gq4.astype(BF16), gk4.astype(BF16)
        gb = gb_ref[tok, :]
        gt = gt_ref[n]
        gh, gl = _split2(gb)
        gll = (gb - gh.astype(F32) - gl.astype(F32)).astype(BF16)
        th, tl = _split2(gt)
        tll = (gt - th.astype(F32) - tl.astype(F32)).astype(BF16)

        for d in range(2):
            dec4, qdec, kdec = ret_const[d]
            rin_ref[n, d] = _dot((qk_ret * dec4).astype(BF16), v4b)
            rqd_ref[n, d] = (q4 * qdec).astype(BF16)
            rkd_ref[n, d] = (k4 * kdec).astype(BF16)

        for d in range(2):
            fwd = d == 0
            low = jnp.where((rc >= cc) if fwd else (rc <= cc), 1.0, 0.0).astype(BF16)
            upp_w = jnp.where((wr <= wc) if fwd else (wr >= wc), 1.0, 0.0).astype(BF16)
            tri_w = (wr >= wc) if fwd else (wr <= wc)
            tri_c = (rc >= cc) if fwd else (rc <= cc)
            gc_col = _dot(low, gh) + _dot(low, gl) + _dot(low, gll)
            gc_row = _dot(th, upp_w) + _dot(tl, upp_w) + _dot(tll, upp_w)
            last_row = CHUNK - 1 if fwd else 0
            idx = [d * N_HEADS + h for h in range(N_HEADS)]
            gcol = [gc_col[:, i:i + 1] for i in idx]
            glast = [gc_col[last_row:last_row + 1, i:i + 1] for i in idx]
            beta = [gb[:, 2 * N_HEADS + i:2 * N_HEADS + i + 1] for i in idx]
            col4 = lambda parts: jnp.concatenate([jnp.broadcast_to(p, (CHUNK, 1)) for p in parts], axis=0)
            gcol4, glast4, beta4 = col4(gcol), col4(glast), col4(beta)
            eg4 = jnp.exp(gcol4)
            kb4 = gk4 * beta4
            gcol_w = by_head([jnp.broadcast_to(g, (CHUNK, STACK)) for g in gcol])
            grow_w = by_head([jnp.broadcast_to(gc_row[i:i + 1, :], (CHUNK, STACK)) for i in idx])
            dmat_w = jnp.exp(jnp.where(tri_w, gcol_w - grow_w, -jnp.inf))
            kk = _dot_nt(kb4.astype(BF16), gk4b)
            kk_w = by_head([rows(kk, h) for h in range(N_HEADS)])
            p = jnp.where(wr == wc, 0.0, -(kk_w * dmat_w))
            setup.append((n, d, p, gcol, glast, gcol4, glast4, beta4, eg4, kb4, gc_row, idx, tri_c,
                          gq4, gk4, gv4, gq4b, gk4b))

    ps = [s[2] for s in setup]
    bds = [block_diag(p) for p in ps]
    tinvs = [eye_w + p for p in ps]
    for _ in range(5):
        ps = [wide_times(p, *bd) for p, bd in zip(ps, bds)]
        bds = [block_diag(p) for p in ps]
        tinvs = [t + wide_times(t, *bd) for t, bd in zip(tinvs, bds)]
    t_bds = [block_diag(t) for t in tinvs]
    for (n, d, _, gcol, glast, gcol4, glast4, beta4, eg4, kb4, gc_row, idx, tri_c,
         gq4, gk4, gv4, gq4b, gk4b), (th_bd, tl_bd) in zip(setup, t_bds):
        rh, rl = _split2(jnp.concatenate([gv4 * beta4, kb4 * eg4], axis=1))
        uw = _dot(th_bd, rh) + _dot(th_bd, rl) + _dot(tl_bd, rh)
        gu_ref[n, d] = uw[:, :GDN_DV]
        w4 = uw[:, GDN_DV:].astype(BF16)
        qd4 = (gq4 * eg4).astype(BF16)
        gkd_ref[n, d] = (gk4 * jnp.exp(glast4 - gcol4)).astype(BF16)
        for h in range(N_HEADS):
            gwq_ref[n, d, h] = jnp.concatenate([rows(w4, h), rows(qd4, h)], axis=0)
            dm = jnp.exp(jnp.where(tri_c, gcol[h] - gc_row[idx[h]:idx[h] + 1, :CHUNK], -jnp.inf))
            gat_ref[n, d, h] = (_dot_nt(rows(gq4b, h), rows(gk4b, h)) * dm).astype(BF16)
        gcd_ref[n, d] = jnp.concatenate(
            [jnp.broadcast_to(jnp.exp(g), (1, LANE)) for g in glast] + [jnp.zeros((SUBLANE - N_HEADS, LANE), F32)],
            axis=0)


def _scan_kernel(fblk, bblk, first, last, s0idx, soidx, logit_ref, *refs):
    del fblk, bblk, s0idx, soidx
    views = (refs[0:9], refs[9:18])
    s0r_ref, s0g_ref = refs[18], refs[19]
    o_refs = ((refs[20], refs[22]), (refs[21], refs[23]))
    sro_ref, sgo_ref = refs[24], refs[25]
    sr_ref, sg_ref = refs[26], refs[27]
    t = pl.program_id(0)

    @pl.when(first[t] == 1)
    def _():
        sr_ref[...] = s0r_ref[...]
        sg_ref[...] = s0g_ref[...]

    @pl.when(first[t] == 2)
    def _():
        sr_ref[...] = jnp.zeros(sr_ref.shape, F32)
        sg_ref[...] = jnp.zeros(sg_ref.shape, F32)

    chains = [(d, h) for d in range(2) for h in range(N_HEADS)]
    s_gdn = {c: sg_ref[c[0], c[1]] for c in chains}
    s_ret = {c: sr_ref[c[0], c[1]] for c in chains}
    ret_decay = {(d, h): jnp.exp(-_softplus(-jnp.full((1, LANE), logit_ref[d * N_HEADS + h], F32)) * float(CHUNK))
                 for d, h in chains}
    n_sub = views[0][0].shape[0]
    for sub in range(n_sub):
        at = (sub, n_sub - 1 - sub)

        def rows(d, k, h):
            return views[d][k][at[d], h * CHUNK:(h + 1) * CHUNK, :]

        ws = {(d, h): _dot(views[d][5][at[d], h], s_gdn[d, h].astype(BF16)) for d, h in chains}
        inter = {(d, h): _dot(rows(d, 1, h), s_ret[d, h].astype(BF16)) for d, h in chains}
        kv = {(d, h): _dot_tn(rows(d, 2, h), rows(d, 3, h)) for d, h in chains}
        v_new = {(d, h): (rows(d, 4, h) - ws[d, h][:CHUNK]).astype(BF16) for d, h in chains}
        intra = {(d, h): _dot(views[d][6][at[d], h], v_new[d, h]) for d, h in chains}
        kvn = {(d, h): _dot_tn(rows(d, 7, h), v_new[d, h]) for d, h in chains}
        for d, h in chains:
            tok = slice(at[d] * CHUNK, (at[d] + 1) * CHUNK)
            lanes = slice(h * LANE, (h + 1) * LANE)
            o_refs[d][0][tok, lanes] = rows(d, 0, h) + inter[d, h]
            o_refs[d][1][tok, lanes] = ws[d, h][CHUNK:] + intra[d, h]
            s_ret[d, h] = s_ret[d, h] * ret_decay[d, h] + kv[d, h]
            s_gdn[d, h] = s_gdn[d, h] * views[d][8][at[d], h:h + 1, :] + kvn[d, h]
    for d, h in chains:
        sr_ref[d, h] = s_ret[d, h]
        sg_ref[d, h] = s_gdn[d, h]

    @pl.when(last[t] == 1)
    def _():
        sro_ref[...] = sr_ref[...]
        sgo_ref[...] = sg_ref[...]


def _scan_tables(lay):
    fblk, bblk, first, last, s0idx, soidx = [], [], [], [], [], []
    blk = SCAN_CHUNKS * CHUNK
    assert lay.seq % blk == 0 and lay.dec_seq % blk == 0
    cp, cs = lay.seq // blk, lay.dec_seq // blk
    for b in range(lay.dec_batch):
        base = lay.n_p // blk + b * cs
        for s in range(cs):
            fblk.append(base + s); bblk.append(base + cs - 1 - s)
            first.append(1 if s == 0 else 0); last.append(0); s0idx.append(b); soidx.append(0)
    for b in range(lay.batch):
        base = b * cp
        for s in range(cp):
            fblk.append(base + s); bblk.append(base + cp - 1 - s)
            first.append(2 if s == 0 else 0); last.append(1 if s == cp - 1 else 0)
            s0idx.append(lay.dec_batch - 1); soidx.append(b)
    return [jnp.asarray(np.asarray(a, np.int32)) for a in (fblk, bblk, first, last, s0idx, soidx)]


def _scan_prep(lay, rq, rk, rv, cq, ck, cv, gb, gt, logit):
    n_chunks = lay.n // CHUNK
    nc = PREP_CHUNKS
    assert n_chunks % nc == 0
    width = N_HEADS * LANE

    def tok(wd):
        return pl.BlockSpec((nc * CHUNK, wd), lambda i: (i, 0))

    def per_dir(shape, dtype):
        spec = pl.BlockSpec((nc, 2) + shape, lambda i: (i,) + (0,) * (1 + len(shape)))
        return spec, jax.ShapeDtypeStruct((n_chunks, 2) + shape, dtype)

    outs = [
        per_dir((STACK, LANE), F32),
        per_dir((STACK, LANE), BF16),
        per_dir((STACK, LANE), BF16),
        (pl.BlockSpec((nc, STACK, LANE), lambda i: (i, 0, 0)), jax.ShapeDtypeStruct((n_chunks, STACK, LANE), BF16)),
        per_dir((STACK, LANE), F32),
        per_dir((N_HEADS, 2 * CHUNK, LANE), BF16),
        per_dir((N_HEADS, CHUNK, CHUNK), BF16),
        per_dir((STACK, LANE), BF16),
        per_dir((SUBLANE, LANE), F32),
    ]
    return pl.pallas_call(
        functools.partial(_prep_kernel, chunks=nc),
        grid=(n_chunks // nc,),
        in_specs=[pl.BlockSpec(memory_space=pltpu.SMEM)] + [tok(width)] * 6
                 + [tok(LANE), pl.BlockSpec((nc, 4 * GDN_HEADS, CHUNK), lambda i: (i, 0, 0))],
        out_specs=[o[0] for o in outs],
        out_shape=[o[1] for o in outs],
        compiler_params=_params("parallel"),
        name="scan_prep",
    )(logit.reshape(-1).astype(F32), rq, rk, rv, cq, ck, cv, gb, gt)


def _even_scan(lay, prep, logit, s0_ret, s0_gdn):
    tables = _scan_tables(lay)
    steps = int(tables[0].shape[0])
    width = N_HEADS * LANE

    def view(which):
        def blk(arr, with_dir=True):
            shape = arr.shape[2:] if with_dir else arr.shape[1:]
            if with_dir:
                return pl.BlockSpec((SCAN_CHUNKS, None) + shape,
                                    lambda t, *tb: (tb[which][t], which) + (0,) * len(shape))
            return pl.BlockSpec((SCAN_CHUNKS,) + shape, lambda t, *tb: (tb[which][t],) + (0,) * len(shape))
        return [blk(a, with_dir=(i != 3)) for i, a in enumerate(prep)]

    state_shape = (2, N_HEADS, LANE, LANE)
    s0_spec = pl.BlockSpec((None,) + state_shape, lambda t, *tb: (tb[4][t], 0, 0, 0, 0))
    so_spec = pl.BlockSpec((None,) + state_shape, lambda t, *tb: (tb[5][t], 0, 0, 0, 0))
    out_f = pl.BlockSpec((SCAN_CHUNKS * CHUNK, width), lambda t, *tb: (tb[0][t], 0))
    out_b = pl.BlockSpec((SCAN_CHUNKS * CHUNK, width), lambda t, *tb: (tb[1][t], 0))
    o_sds = jax.ShapeDtypeStruct((lay.n, width), F32)
    so_sds = jax.ShapeDtypeStruct((lay.batch,) + state_shape, F32)
    args = list(prep)
    return pl.pallas_call(
        _scan_kernel,
        grid_spec=pltpu.PrefetchScalarGridSpec(
            num_scalar_prefetch=6,
            grid=(steps,),
            in_specs=[pl.BlockSpec(memory_space=pltpu.SMEM)] + view(0) + view(1) + [s0_spec, s0_spec],
            out_specs=[out_f, out_b, out_f, out_b, so_spec, so_spec],
            scratch_shapes=[pltpu.VMEM(state_shape, F32), pltpu.VMEM(state_shape, F32)],
        ),
        out_shape=[o_sds, o_sds, o_sds, o_sds, so_sds, so_sds],
        compiler_params=_params("arbitrary"),
        name="even_scan",
    )(*tables, logit.reshape(-1).astype(F32), *args, *args, s0_ret, s0_gdn)


def _even_out_kernel(x_ref, mod_ref, rf_ref, rb_ref, gf_ref, gb_ref, rg_ref, gg_ref, rnw_ref, gnw_ref, w_ref, o_ref):
    ret = _head_rmsnorm(rf_ref[...] + rb_ref[...], rnw_ref[...], RET_HEADS) * _silu(rg_ref[...])
    gdn = _head_rmsnorm(gf_ref[...] + gb_ref[...], gnw_ref[...], GDN_HEADS) * _silu(gg_ref[...])
    y = _dot(jnp.concatenate([ret, gdn], axis=1).astype(BF16), w_ref[...])
    o_ref[...] = x_ref[...] + mod_ref[5:6, :] * y


def _even_out(lay, x, mods, orf, orb, ogf, ogb, rg, gg, rnw, gnw, w):
    return pl.pallas_call(
        _even_out_kernel,
        grid=(lay.tiles,),
        in_specs=[lay.tok(D_MODEL), lay.mod()] + [lay.tok(RET_V)] * 6
                 + [_resident((1, LANE)), _resident((1, LANE)), _resident(w.shape)],
        out_specs=lay.tok(D_MODEL),
        out_shape=jax.ShapeDtypeStruct((lay.n, D_MODEL), F32),
        compiler_params=_params("parallel"),
        name="even_out",
    )(x, mods, orf, orb, ogf, ogb, rg, gg, rnw.reshape(1, LANE), gnw.reshape(1, LANE), w)


def _odd_proj_kernel(x_ref, mod_ref, nw_ref, w_ref, qnw_ref, knw_ref, cos_ref, sin_ref,
                     q_ref, kd_ref, vt_ref, ks_ref, vs_ref, *, p_tiles):
    h = _adaln(x_ref[...], nw_ref[...], mod_ref[3:4, :], mod_ref[4:5, :]).astype(BF16)
    tm = h.shape[0]
    lane = lax.broadcasted_iota(jnp.int32, (tm, LANE), 1)
    lo = lane < ATT_HD
    first16 = (lane & (2 * ROPE_PAIRS - 1)) < ROPE_PAIRS
    cos, sin = cos_ref[...], sin_ref[...]

    def norm_rope(xs, w2):
        sq = xs * xs
        ms_lo = jnp.sum(jnp.where(lo, sq, 0.0), axis=-1, keepdims=True) * (1.0 / ATT_HD)
        ms_hi = jnp.sum(jnp.where(lo, 0.0, sq), axis=-1, keepdims=True) * (1.0 / ATT_HD)
        xn = xs * jnp.where(lo, lax.rsqrt(ms_lo + EPS), lax.rsqrt(ms_hi + EPS)) * w2
        partner = jnp.where(first16, pltpu.roll(xn, LANE - ROPE_PAIRS, 1), pltpu.roll(xn, ROPE_PAIRS, 1))
        return xn * cos + partner * sin

    def both_halves(xs, j):
        swapped = pltpu.roll(xs, ATT_HD, 1)
        return jnp.where(lo, xs, swapped) if j % 2 == 0 else jnp.where(lo, swapped, xs)

    for t in range(ATT_Q // MXU_TILE):
        y = _dot(h, w_ref[:, t * MXU_TILE:(t + 1) * MXU_TILE])
        for p in (2 * t, 2 * t + 1):
            qs = norm_rope(y[:, (p % 2) * LANE:(p % 2 + 1) * LANE], qnw_ref[...])
            q_ref[:, p * LANE:(p + 1) * LANE] = (qs * Q_SCALE).astype(BF16)
    is_prompt = pl.program_id(0) < p_tiles
    assert ATT_KV == MXU_TILE
    yk = _dot(h, w_ref[:, ATT_Q:ATT_Q + ATT_KV])
    yv = _dot(h, w_ref[:, ATT_Q + ATT_KV:ATT_Q + 2 * ATT_KV])
    for p in range(ATT_KV // LANE):
        ks = norm_rope(yk[:, p * LANE:(p + 1) * LANE], knw_ref[...])
        vs = yv[:, p * LANE:(p + 1) * LANE]
        vst = vs.T.astype(BF16)
        for j in (2 * p, 2 * p + 1):
            kd_ref[j] = both_halves(ks, j).astype(BF16)
            for c in range(tm // KEY_CHUNK):
                vt_ref[j, c] = vst[(j % 2) * ATT_HD:(j % 2 + 1) * ATT_HD, c * KEY_CHUNK:(c + 1) * KEY_CHUNK]

        @pl.when(is_prompt)
        def _():
            ks_ref[:, p * LANE:(p + 1) * LANE] = ks
            vs_ref[:, p * LANE:(p + 1) * LANE] = vs


def _rope_tables(lay):
    t = jnp.arange(lay.dec_seq)
    inv = ROPE_THETA ** (-jnp.arange(ROPE_PAIRS, dtype=F32) / ROPE_PAIRS)
    ar = (t // GRID_W).astype(F32)[:, None] * inv
    ac = (t % GRID_W).astype(F32)[:, None] * inv
    cos = jnp.concatenate([jnp.cos(ar), jnp.cos(ar), jnp.cos(ac), jnp.cos(ac)], axis=1)
    sin = jnp.concatenate([-jnp.sin(ar), jnp.sin(ar), -jnp.sin(ac), jnp.sin(ac)], axis=1)
    cos = jnp.concatenate([jnp.tile(cos, (1, 2)), jnp.ones((lay.tm, LANE), F32)], axis=0)
    sin = jnp.concatenate([jnp.tile(sin, (1, 2)), jnp.zeros((lay.tm, LANE), F32)], axis=0)
    return cos, sin


def _odd_proj(lay, x, mods, nw, w, qnw, knw):
    cos, sin = _rope_tables(lay)

    def table_block(i):
        return jnp.where(i < lay.p_tiles, lay.tiles_per_seq, lax.rem(jnp.maximum(i - lay.p_tiles, 0), lay.tiles_per_seq))

    tab = pl.BlockSpec((lay.tm, LANE), lambda i: (table_block(i), 0))
    std = pl.BlockSpec((lay.tm, ATT_KV), lambda i: (jnp.minimum(i, lay.p_tiles - 1), 0))
    tm = lay.tm
    return pl.pallas_call(
        functools.partial(_odd_proj_kernel, p_tiles=lay.p_tiles),
        grid=(lay.tiles,),
        in_specs=[lay.tok(D_MODEL), lay.mod(), _resident((1, D_MODEL)), _resident(w.shape),
                  _resident((1, LANE)), _resident((1, LANE)), tab, tab],
        out_specs=[
            lay.tok(ATT_Q),
            pl.BlockSpec((ATT_KV_HEADS, tm, LANE), lambda i: (0, i, 0)),
            pl.BlockSpec((ATT_KV_HEADS, tm // KEY_CHUNK, ATT_HD, KEY_CHUNK), lambda i: (0, i, 0, 0)),
            std, std,
        ],
        out_shape=[
            jax.ShapeDtypeStruct((lay.n, ATT_Q), BF16),
            jax.ShapeDtypeStruct((ATT_KV_HEADS, lay.n, LANE), BF16),
            jax.ShapeDtypeStruct((ATT_KV_HEADS, lay.n // KEY_CHUNK, ATT_HD, KEY_CHUNK), BF16),
            jax.ShapeDtypeStruct((lay.n_p, ATT_KV), F32),
            jax.ShapeDtypeStruct((lay.n_p, ATT_KV), F32),
        ],
        compiler_params=_params("arbitrary"),
        name="odd_proj",
    )(x, mods, nw.reshape(1, D_MODEL), w, jnp.tile(qnw.reshape(1, ATT_HD), (1, 2)),
      jnp.tile(knw.reshape(1, ATT_HD), (1, 2)), cos, sin)


def _attn_kernel(*refs, cached):
    if cached:
        zero_ref, q_ref, kn_ref, vn_ref, kc_ref, vc_ref, _, o_ref, *s_scr = refs
    else:
        zero_ref, q_ref, kn_ref, vn_ref, o_ref, *s_scr = refs
    z = zero_ref[0]
    tq = MXU_TILE
    n_sub = q_ref.shape[0] // tq
    n_kv = kn_ref.shape[0]
    gw = ATT_GROUP * ATT_HD
    ck = KEY_CHUNK
    n_c = kc_ref.shape[1] // ck if cached else 0
    n_n = kn_ref.shape[1] // ck
    lo = lax.broadcasted_iota(jnp.int32, (tq, LANE), 1) < ATT_HD
    qms, kv_of = [], []
    for s in range(n_sub):
        for j in range(n_kv):
            for g in range(ATT_GROUP):
                q2 = q_ref[s * tq:(s + 1) * tq, j * gw + (g // 2) * LANE:j * gw + (g // 2 + 1) * LANE]
                qms.append(jnp.where(lo, q2, jnp.zeros_like(q2)) if g % 2 == 0
                           else jnp.where(lo, jnp.zeros_like(q2), q2))
                kv_of.append(j)
    n_units = len(qms)

    def keys(j, c):
        return kc_ref[j, c * ck:(c + 1) * ck, :] if c < n_c else kn_ref[j, (c - n_c) * ck:(c - n_c + 1) * ck, :]

    def values_t(j, c):
        return vc_ref[j, c] if c < n_c else vn_ref[j, c - n_c]

    ones_rows = jnp.where(lax.broadcasted_iota(jnp.int32, (2 * SUBLANE, ck), 0) == 0, 1.0, 0.0).astype(BF16)
    m_fin = [None] * n_units
    outs = [None] * n_units
    for ph in range(n_units + 1):
        ga = ph if ph < n_units else None
        gb = ph - 1 if ph >= 1 else None
        ma = jnp.full((1, tq), -jnp.inf, F32)
        acc = jnp.zeros((ATT_HD + 2 * SUBLANE, tq), F32)
        for c in range(n_c + n_n):
            if ga is not None:
                st = _dot_nt(keys(kv_of[ga], c), qms[ga])
                s_scr[ga % 2][z + c] = st
                ma = jnp.maximum(ma, jnp.max(st, axis=0, keepdims=True))
            if gb is not None:
                p = jnp.exp2(s_scr[gb % 2][z + c] - m_fin[gb]).astype(BF16)
                acc = acc + _dot(jnp.concatenate([values_t(kv_of[gb], c), ones_rows], axis=0), p)
        if ga is not None:
            m_fin[ga] = ma
        if gb is not None:
            outs[gb] = acc[:ATT_HD] / acc[ATT_HD:ATT_HD + 1]
    per_sub = n_kv * ATT_GROUP
    for s in range(n_sub):
        o_ref[s * tq:(s + 1) * tq, :] = jnp.concatenate(outs[s * per_sub:(s + 1) * per_sub], axis=0).T


def _attention(lay, q, kd, vt, kc, vc):
    gw = ATT_GROUP * ATT_HD
    o_sds = jax.ShapeDtypeStruct((lay.n, ATT_Q), F32)
    ck = KEY_CHUNK
    assert lay.seq % ck == 0 and lay.dec_seq % ck == 0 and lay.past % ck == 0
    assert lay.seq % MXU_TILE == 0 and lay.dec_seq % MXU_TILE == 0 and ATT_Q_TILE % MXU_TILE == 0

    zero = jnp.zeros((1,), jnp.int32)
    zero_spec = pl.BlockSpec(memory_space=pltpu.SMEM)

    tq = min(ATT_Q_TILE, lay.seq)
    nq = lay.seq // tq
    nck = lay.seq // ck
    out = pl.pallas_call(
        functools.partial(_attn_kernel, cached=False),
        grid=(lay.batch, nq),
        in_specs=[
            zero_spec,
            pl.BlockSpec((tq, ATT_Q), lambda b, i: (b * nq + i, 0)),
            pl.BlockSpec((ATT_KV_HEADS, lay.seq, LANE), lambda b, i: (0, b, 0)),
            pl.BlockSpec((ATT_KV_HEADS, nck, ATT_HD, ck), lambda b, i: (0, b, 0, 0)),
        ],
        out_specs=pl.BlockSpec((tq, ATT_Q), lambda b, i: (b * nq + i, 0)),
        out_shape=o_sds,
        scratch_shapes=[pltpu.VMEM((nck, ck, MXU_TILE), F32)] * 2,
        compiler_params=_params("parallel", "parallel"),
        name="attn_context",
    )(zero, q, kd, vt)

    tq = min(ATT_Q_TILE, lay.dec_seq)
    nq = lay.dec_seq // tq
    nck = lay.dec_seq // ck
    qbase = lay.n_p // tq
    kbase = lay.n_p // lay.dec_seq
    assert lay.n_p % lay.dec_seq == 0
    return pl.pallas_call(
        functools.partial(_attn_kernel, cached=True),
        grid=(lay.dec_batch, ATT_KV_HEADS, nq),
        in_specs=[
            zero_spec,
            pl.BlockSpec((tq, gw), lambda b, j, i: (qbase + b * nq + i, j)),
            pl.BlockSpec((1, lay.dec_seq, LANE), lambda b, j, i: (j, kbase + b, 0)),
            pl.BlockSpec((1, nck, ATT_HD, ck), lambda b, j, i: (j, kbase + b, 0, 0)),
            pl.BlockSpec((None, 1, lay.past, LANE), lambda b, j, i: (b, j, 0, 0)),
            pl.BlockSpec((None, 1, lay.past // ck, ATT_HD, ck), lambda b, j, i: (b, j, 0, 0, 0)),
            pl.BlockSpec(memory_space=pl.ANY),
        ],
        out_specs=pl.BlockSpec((tq, gw), lambda b, j, i: (qbase + b * nq + i, j)),
        out_shape=o_sds,
        scratch_shapes=[pltpu.VMEM((lay.past // ck + nck, ck, MXU_TILE), F32)] * 2,
        input_output_aliases={6: 0},
        compiler_params=_params("parallel", "parallel", "parallel"),
        name="attn_latent",
    )(zero, q, kd, vt, kc, vc, out)


def _odd_out_kernel(x_ref, mod_ref, a_ref, w_ref, o_ref):
    o_ref[...] = x_ref[...] + mod_ref[5:6, :] * _dot(a_ref[...].astype(BF16), w_ref[...])


def _odd_out(lay, x, mods, a, w):
    return pl.pallas_call(
        _odd_out_kernel,
        grid=(lay.tiles,),
        in_specs=[lay.tok(D_MODEL), lay.mod(), lay.tok(ATT_Q), _resident(w.shape)],
        out_specs=lay.tok(D_MODEL),
        out_shape=jax.ShapeDtypeStruct((lay.n, D_MODEL), F32),
        compiler_params=_params("parallel"),
        name="odd_out",
    )(x, mods, a, w)


def _pad_state(s):
    return jnp.pad(s, [(0, 0)] * (s.ndim - 2) + [(0, LANE - s.shape[-2]), (0, 0)])


def _cache_keys(c):
    return jnp.tile(c.transpose(0, 2, 1, 3), (1, 1, 1, 2)).astype(BF16)


def _cache_values(c):
    b, t, kvh, hd = c.shape
    return c.reshape(b, t // KEY_CHUNK, KEY_CHUNK, kvh, hd).transpose(0, 3, 1, 4, 2).astype(BF16)


def kernel(x_prompt, x_sample, state_ret, state_gdn, cache_k, cache_v, c, c_ctx,
           mod_w, mod_b, norm_w, ffn_w_in, ffn_w_out, even_w_in, even_w_out,
           ret_decay_logit, ret_norm_w, gdn_conv_w, gdn_A_log, gdn_dt_bias, gdn_norm_w,
           odd_w_in, odd_w_out, q_norm_w, k_norm_w, final_norm_w):
    batch, seq, d = x_prompt.shape
    dec_batch, dec_seq, _ = x_sample.shape
    depth = mod_w.shape[0]
    lay = _Layout(batch, seq, dec_batch, dec_seq, cache_k.shape[2])
    lay_ffn = _Layout(batch, seq, dec_batch, dec_seq, cache_k.shape[2], tile=FFN_TILE)

    x = (x_prompt.reshape(lay.n_p, d), x_sample.reshape(lay.n_s, d))
    n_cond = -(-(1 + dec_batch) // (2 * SUBLANE)) * (2 * SUBLANE)
    cond = jnp.zeros((n_cond, d), F32).at[0].set(c_ctx).at[1:1 + dec_batch].set(c)
    mods = _modulation(cond, mod_w, mod_b)
    ffn_in, ffn_out = ffn_w_in.astype(BF16), ffn_w_out.astype(BF16)

    new_ret, new_gdn, new_k, new_v = [], [], [], []
    for l in range(depth):
        m = mods[l]
        last = l == depth - 1
        x = _ffn(lay_ffn, x, m, norm_w[l, 0], ffn_in, ffn_out, (l, 0), 0)
        if l % 2 == 0:
            e = l // 2
            rq, rk, rv, rg, gqkv, gg, gab = _even_proj(lay, x, m, norm_w[l, 1], _even_weight(even_w_in[e]))
            cq, ck, cv, gb, gt = _even_conv(lay, gqkv, gab, gdn_conv_w[e], gdn_A_log[e], gdn_dt_bias[e])
            prep = _scan_prep(lay, rq, rk, rv, cq, ck, cv, gb, gt, ret_decay_logit[e])
            orf, orb, ogf, ogb, sr, sg = _even_scan(
                lay, prep, ret_decay_logit[e], _pad_state(state_ret[:, e]), state_gdn[:, e])
            new_ret.append(sr[:, :, :, :RET_DK, :])
            new_gdn.append(sg)
            x = _even_out(lay, x, m, orf, orb, ogf, ogb, rg, gg, ret_norm_w[e], gdn_norm_w[e],
                          even_w_out[e].astype(BF16))
        else:
            o = l // 2
            q, kd, vt, ks, vs = _odd_proj(lay, x, m, norm_w[l, 1], odd_w_in[o].astype(BF16), q_norm_w[o], k_norm_w[o])
            a = _attention(lay, q, kd, vt, _cache_keys(cache_k[:, o]), _cache_values(cache_v[:, o]))
            new_k.append(ks.reshape(batch, seq, ATT_KV_HEADS, ATT_HD))
            new_v.append(vs.reshape(batch, seq, ATT_KV_HEADS, ATT_HD))
            x = _odd_out(lay, x, m, a, odd_w_out[o].astype(BF16))
        x = _ffn(lay_ffn, x, m, norm_w[l, 2], ffn_in, ffn_out, (l, 1), 2,
                 final_w=final_norm_w if last else None)

    y_prompt = x[0].reshape(batch, seq, d)
    y_sample = x[1].reshape(dec_batch, dec_seq, d)
    return (y_prompt, y_sample, jnp.stack(new_ret, axis=1), jnp.stack(new_gdn, axis=1),
            jnp.stack(new_k, axis=1), jnp.stack(new_v, axis=1))
```

```python
import functools

import numpy as np
import jax
import jax.numpy as jnp
from jax import lax
from jax.experimental import pallas as pl
from jax.experimental.pallas import tpu as pltpu

F32 = jnp.float32
BF16 = jnp.bfloat16

D_MODEL = 1024
GRID_W = 64
RET_HEADS = 4
RET_DK = 64
RET_DV = 128
GDN_HEADS = 4
GDN_DK = 128
GDN_DV = 128
CHUNK = 64
ATT_HEADS = 16
ATT_KV_HEADS = 4
ATT_HD = 64
ATT_GROUP = ATT_HEADS // ATT_KV_HEADS
ROPE_THETA = 10000.0
ROPE_PAIRS = ATT_HD // 4
FFN_HIDDEN = 2816
N_MOD = 9
EPS = 1e-6

RET_QK = RET_HEADS * RET_DK
RET_V = RET_HEADS * RET_DV
GDN_QK = GDN_HEADS * GDN_DK
GDN_V = GDN_HEADS * GDN_DV
ATT_Q = ATT_HEADS * ATT_HD
ATT_KV = ATT_KV_HEADS * ATT_HD

LANE = 128
SUBLANE = 8
V7X_VMEM_BYTES = 64 * 1024 * 1024
VMEM_LIMIT = V7X_VMEM_BYTES - 8 * 1024 * 1024

TOKEN_TILE = 512
FFN_TILE = 1024
CONV_TILE = 256
ATT_Q_TILE = 1024
KEY_CHUNK = 256
Q_SCALE = ATT_HD ** -0.5 * float(np.log2(np.e))
MXU_TILE = 256
FFN_CHUNKS = ((0, 6 * MXU_TILE), (6 * MXU_TILE, FFN_HIDDEN))
HEAD_PAD = LANE
PREP_CHUNKS = 4
SCAN_CHUNKS = 4


def _params(*sem):
    return pltpu.CompilerParams(dimension_semantics=sem, vmem_limit_bytes=VMEM_LIMIT)


def _resident(shape):
    nd = len(shape)
    return pl.BlockSpec(shape, lambda *_: (0,) * nd, pipeline_mode=pl.Buffered(1))


def _resident_at(arr, lead):
    shape = arr.shape[len(lead):]
    return pl.BlockSpec((None,) * len(lead) + shape, lambda *_: tuple(lead) + (0,) * len(shape),
                        pipeline_mode=pl.Buffered(1))


def _dot(a, b):
    return jnp.dot(a, b, preferred_element_type=F32)


def _dot_nt(a, b):
    return lax.dot_general(a, b, (((1,), (1,)), ((), ())), preferred_element_type=F32)


def _dot_tn(a, b):
    return lax.dot_general(a, b, (((0,), (0,)), ((), ())), preferred_element_type=F32)


def _split2(a):
    hi = a.astype(BF16)
    lo = (a - hi.astype(F32)).astype(BF16)
    return hi, lo


def _dot3(a, b):
    ah, al = _split2(a)
    bh, bl = _split2(b)
    return _dot(ah, bh) + _dot(ah, bl) + _dot(al, bh)


def _silu(x):
    return x * jax.nn.sigmoid(x)


def _softplus(x):
    return jnp.maximum(x, 0.0) + jnp.log(1.0 + jnp.exp(-jnp.abs(x)))


def _adaln(x, nw, shift, scale):
    ms = jnp.mean(x * x, axis=-1, keepdims=True)
    return (x * lax.rsqrt(ms + EPS)) * nw * (1.0 + scale) + shift


def _head_rmsnorm(x, w, n_heads):
    parts = []
    for h in range(n_heads):
        xs = x[:, h * LANE:(h + 1) * LANE]
        ms = jnp.mean(xs * xs, axis=-1, keepdims=True)
        parts.append(xs * lax.rsqrt(ms + EPS) * w)
    return jnp.concatenate(parts, axis=1)


def _mod_kernel(cond_ref, w_ref, b_ref, o_ref):
    c = cond_ref[...]
    o_ref[0] = _dot(_silu(c).astype(BF16), w_ref[0].astype(BF16)) + b_ref[0]


def _modulation(cond, mod_w, mod_b):
    depth, d, n = mod_w.shape
    r = cond.shape[0]
    tn = n // 8
    out = pl.pallas_call(
        _mod_kernel,
        grid=(depth, n // tn),
        in_specs=[
            pl.BlockSpec((r, d), lambda l, j: (0, 0)),
            pl.BlockSpec((1, d, tn), lambda l, j: (l, 0, j)),
            pl.BlockSpec((1, 1, tn), lambda l, j: (l, 0, j)),
        ],
        out_specs=pl.BlockSpec((1, r, tn), lambda l, j: (l, 0, j)),
        out_shape=jax.ShapeDtypeStruct((depth, r, n), F32),
        compiler_params=_params("parallel", "parallel"),
        name="modulation",
    )(cond, mod_w, mod_b.reshape(depth, 1, n))
    return out.reshape(depth, r, N_MOD, d)


class _Layout:
    def __init__(self, batch, seq, dec_batch, dec_seq, past, tile=TOKEN_TILE):
        self.batch, self.seq, self.dec_batch, self.dec_seq, self.past = batch, seq, dec_batch, dec_seq, past
        self.n_p = batch * seq
        self.n_s = dec_batch * dec_seq
        self.n = self.n_p + self.n_s
        self.tm = min(tile, self.n_p, dec_seq)
        assert self.n_p % self.tm == 0 and dec_seq % self.tm == 0
        self.tiles = self.n // self.tm
        self.p_tiles = self.n_p // self.tm
        self.tiles_per_seq = dec_seq // self.tm

    def group(self, i):
        return jnp.where(i < self.p_tiles, 0, 1 + (i - self.p_tiles) // self.tiles_per_seq)

    def tok(self, width, tm=None):
        tm = tm or self.tm
        return pl.BlockSpec((tm, width), lambda i: (i, 0))

    def mod(self):
        return pl.BlockSpec((None, N_MOD, D_MODEL), lambda i: (self.group(i), 0, 0))


def _ffn_kernel(*refs, j, first, final, p_tiles, mixer):
    refs = list(refs)
    x_refs = [refs.pop(0) for _ in range(2 if first else 1)]
    n_mix = {None: 0, "odd": 2, "even": 9}[mixer]
    mix_refs = [refs.pop(0) for _ in range(n_mix)]
    mod_ref, nw_ref, win_ref, wout_ref = (refs.pop(0) for _ in range(4))
    fw_ref = refs.pop(0) if final else None
    o_refs = refs
    is_prompt = pl.program_id(0) < p_tiles
    x = jnp.where(is_prompt, x_refs[0][...], x_refs[1][...]) if first else x_refs[0][...]
    if mixer == "odd":
        a_ref, wo_ref = mix_refs
        x = x + mod_ref[5:6, :] * _dot(a_ref[...], wo_ref[...])
    elif mixer == "even":
        rf_ref, rb_ref, gf_ref, gb_ref, rg_ref, gg_ref, rnw_ref, gnw_ref, wo_ref = mix_refs
        ret = _head_rmsnorm(rf_ref[...] + rb_ref[...], rnw_ref[...], RET_HEADS) * _silu(rg_ref[...])
        gdn = _head_rmsnorm(gf_ref[...] + gb_ref[...], gnw_ref[...], GDN_HEADS) * _silu(gg_ref[...])
        x = x + mod_ref[5:6, :] * _dot(jnp.concatenate([ret, gdn], axis=1).astype(BF16), wo_ref[...])
    h = _adaln(x, nw_ref[...], mod_ref[3 * j:3 * j + 1, :], mod_ref[3 * j + 1:3 * j + 2, :]).astype(BF16)
    acc = None
    for lo, hi in FFN_CHUNKS:
        a = _dot(h, win_ref[:, lo:hi])
        b = _dot(h, win_ref[:, FFN_HIDDEN + lo:FFN_HIDDEN + hi])
        y = _dot((_silu(a) * b).astype(BF16), wout_ref[lo:hi, :])
        acc = y if acc is None else acc + y
    out = x + (0.5 * mod_ref[3 * j + 2:3 * j + 3, :]) * acc
    if not final:
        o_refs[0][...] = out
        return
    ms = jnp.mean(out * out, axis=-1, keepdims=True)
    out = out * lax.rsqrt(ms + EPS) * fw_ref[...]

    @pl.when(is_prompt)
    def _():
        o_refs[0][...] = out

    @pl.when(jnp.logical_not(is_prompt))
    def _():
        o_refs[1][...] = out


def _ffn(lay, xs, mods, nw, w_in, w_out, which, j, final_w=None, mixer=None, mixer_ins=()):
    first = isinstance(xs, tuple)
    final = final_w is not None
    prompt_rows = pl.BlockSpec((lay.tm, D_MODEL), lambda i: (jnp.minimum(i, lay.p_tiles - 1), 0))
    latent_rows = pl.BlockSpec((lay.tm, D_MODEL), lambda i: (jnp.maximum(i - lay.p_tiles, 0), 0))
    ins = list(xs) if first else [xs]
    specs = [prompt_rows, latent_rows] if first else [lay.tok(D_MODEL)]
    for arr in mixer_ins:
        ins.append(arr)
        specs.append(lay.tok(arr.shape[1]) if arr.shape[0] == lay.n else _resident(arr.shape))
    ins += [mods, nw.reshape(1, D_MODEL), w_in, w_out]
    specs += [lay.mod(), _resident((1, D_MODEL)), _resident_at(w_in, which), _resident_at(w_out, which)]
    if final:
        ins.append(final_w.reshape(1, D_MODEL))
        specs.append(_resident((1, D_MODEL)))
        out_specs = [prompt_rows, latent_rows]
        out_shape = [jax.ShapeDtypeStruct((lay.n_p, D_MODEL), F32), jax.ShapeDtypeStruct((lay.n_s, D_MODEL), F32)]
    else:
        out_specs = lay.tok(D_MODEL)
        out_shape = jax.ShapeDtypeStruct((lay.n, D_MODEL), F32)
    return pl.pallas_call(
        functools.partial(_ffn_kernel, j=j, first=first, final=final, p_tiles=lay.p_tiles, mixer=mixer),
        grid=(lay.tiles,),
        in_specs=specs,
        out_specs=out_specs,
        out_shape=out_shape,
        compiler_params=_params("arbitrary"),
        name=("ffn_final" if final else ("ffn_first" if first else "ffn")) + ("_" + mixer if mixer else ""),
    )(*ins)


EVEN_WIDTHS = (RET_HEADS * HEAD_PAD, RET_HEADS * HEAD_PAD, RET_V, RET_V, 2 * GDN_QK + GDN_V, GDN_V, LANE)


def _proj_kernel(x_ref, mod_ref, nw_ref, w_ref, *o_refs, j, widths):
    h = _adaln(x_ref[...], nw_ref[...], mod_ref[3 * j:3 * j + 1, :], mod_ref[3 * j + 1:3 * j + 2, :]).astype(BF16)
    off = 0
    for o_ref, wd in zip(o_refs, widths):
        o_ref[...] = _dot(h, w_ref[:, off:off + wd])
        off += wd


def _even_proj(lay, x, mods, nw, w):
    return pl.pallas_call(
        functools.partial(_proj_kernel, j=1, widths=EVEN_WIDTHS),
        grid=(lay.tiles,),
        in_specs=[lay.tok(D_MODEL), lay.mod(), _resident((1, D_MODEL)), _resident(w.shape)],
        out_specs=[lay.tok(wd) for wd in EVEN_WIDTHS],
        out_shape=[jax.ShapeDtypeStruct((lay.n, wd), F32) for wd in EVEN_WIDTHS],
        compiler_params=_params("parallel"),
        name="even_proj",
    )(x, mods, nw.reshape(1, D_MODEL), w)


def _even_weight(w):
    d = w.shape[0]
    o = 0
    rq = w[:, o:o + RET_QK]; o += RET_QK
    rk = w[:, o:o + RET_QK]; o += RET_QK
    rest = w[:, o:o + 2 * RET_V + 2 * GDN_QK + 2 * GDN_V]; o += 2 * RET_V + 2 * GDN_QK + 2 * GDN_V
    gab = w[:, o:]

    def pad_heads(m):
        m = m.reshape(d, RET_HEADS, RET_DK)
        return jnp.pad(m, ((0, 0), (0, 0), (0, HEAD_PAD - RET_DK))).reshape(d, RET_HEADS * HEAD_PAD)

    gab = jnp.pad(gab, ((0, 0), (0, LANE - gab.shape[1])))
    return jnp.concatenate([pad_heads(rq), pad_heads(rk * RET_DK ** -0.5), rest, gab], axis=1).astype(BF16)


def _conv_kernel(x_ref, prev_ref, next_ref, cw_ref, gab_ref, alog_ref, dt_ref,
                 cq_ref, ck_ref, cv_ref, gb_ref, gt_ref, *, n_p, seq, dec_seq):
    tc = x_ref.shape[0]
    start = pl.program_id(0) * tc
    in_prompt = start < n_p
    rel = jnp.where(in_prompt, start, start - n_p)
    slen = jnp.where(in_prompt, seq, dec_seq)
    pos = lax.rem(rel, slen)
    x = x_ref[...]
    prev_row = jnp.where(pos == 0, 0.0, prev_ref[SUBLANE - 1:SUBLANE, :])
    next_row = jnp.where(pos + tc == slen, 0.0, next_ref[0:1, :])
    row = lax.broadcasted_iota(jnp.int32, x.shape, 0)
    xm1 = jnp.where(row == 0, prev_row, pltpu.roll(x, 1, 0))
    xp1 = jnp.where(row == tc - 1, next_row, pltpu.roll(x, tc - 1, 0))
    y = _silu(xm1 * cw_ref[0:1, :] + x * cw_ref[1:2, :] + xp1 * cw_ref[2:3, :])
    for h in range(GDN_HEADS):
        q = y[:, h * LANE:(h + 1) * LANE]
        k = y[:, GDN_QK + h * LANE:GDN_QK + (h + 1) * LANE]
        cq_ref[:, h * LANE:(h + 1) * LANE] = (
            q * lax.rsqrt(jnp.sum(q * q, axis=-1, keepdims=True) + EPS) * GDN_DK ** -0.5)
        ck_ref[:, h * LANE:(h + 1) * LANE] = k * lax.rsqrt(jnp.sum(k * k, axis=-1, keepdims=True) + EPS)
    cv_ref[...] = y[:, 2 * GDN_QK:]
    g = gab_ref[...]
    lane = lax.broadcasted_iota(jnp.int32, g.shape, 1)
    gb = jnp.where(lane < 2 * GDN_HEADS, -jnp.exp(alog_ref[...]) * _softplus(g + dt_ref[...]), jax.nn.sigmoid(g))
    gb_ref[...] = gb
    gbt = gb.T
    for c in range(tc // CHUNK):
        gt_ref[c] = gbt[0:4 * GDN_HEADS, c * CHUNK:(c + 1) * CHUNK]


def _even_conv(lay, gqkv, gab, conv_w, a_log, dt_bias):
    tc = min(CONV_TILE, lay.seq, lay.dec_seq)
    assert lay.seq % tc == 0 and lay.dec_seq % tc == 0 and tc % LANE == 0
    width = gqkv.shape[1]
    nblk8 = lay.n // SUBLANE
    r = tc // SUBLANE

    def pad_row(v):
        v = v.reshape(1, -1).astype(F32)
        return jnp.pad(v, ((0, 0), (0, LANE - v.shape[1])))

    return pl.pallas_call(
        functools.partial(_conv_kernel, n_p=lay.n_p, seq=lay.seq, dec_seq=lay.dec_seq),
        grid=(lay.n // tc,),
        in_specs=[
            pl.BlockSpec((tc, width), lambda i: (i, 0)),
            pl.BlockSpec((SUBLANE, width), lambda i: (jnp.maximum(i * r - 1, 0), 0)),
            pl.BlockSpec((SUBLANE, width), lambda i: (jnp.minimum((i + 1) * r, nblk8 - 1), 0)),
            _resident(conv_w.shape),
            pl.BlockSpec((tc, LANE), lambda i: (i, 0)),
            _resident((1, LANE)),
            _resident((1, LANE)),
        ],
        out_specs=[
            pl.BlockSpec((tc, GDN_QK), lambda i: (i, 0)),
            pl.BlockSpec((tc, GDN_QK), lambda i: (i, 0)),
            pl.BlockSpec((tc, GDN_V), lambda i: (i, 0)),
            pl.BlockSpec((tc, LANE), lambda i: (i, 0)),
            pl.BlockSpec((tc // CHUNK, 4 * GDN_HEADS, CHUNK), lambda i: (i, 0, 0)),
        ],
        out_shape=[
            jax.ShapeDtypeStruct((lay.n, GDN_QK), F32),
            jax.ShapeDtypeStruct((lay.n, GDN_QK), F32),
            jax.ShapeDtypeStruct((lay.n, GDN_V), F32),
            jax.ShapeDtypeStruct((lay.n, LANE), F32),
            jax.ShapeDtypeStruct((lay.n // CHUNK, 4 * GDN_HEADS, CHUNK), F32),
        ],
        compiler_params=_params("parallel"),
        name="even_conv",
    )(gqkv, gqkv, gqkv, conv_w, gab, pad_row(a_log), pad_row(dt_bias))


N_HEADS = 4
STACK = N_HEADS * CHUNK


def _prep_kernel(logit_ref, rq_ref, rk_ref, rv_ref, cq_ref, ck_ref, cv_ref, gb_ref, gt_ref,
                 rin_ref, rqd_ref, rkd_ref, rvb_ref, gu_ref, gwq_ref, gat_ref, gkd_ref, gcd_ref, *, chunks):
    wr = lax.broadcasted_iota(jnp.int32, (CHUNK, STACK), 0)
    wcol = lax.broadcasted_iota(jnp.int32, (CHUNK, STACK), 1)
    wc = wcol & (CHUNK - 1)
    whead = wcol >> 6
    r4 = lax.broadcasted_iota(jnp.int32, (STACK, STACK), 0)
    c4 = lax.broadcasted_iota(jnp.int32, (STACK, STACK), 1)
    same_head = (r4 >> 6) == (c4 >> 6)
    rc = lax.broadcasted_iota(jnp.int32, (CHUNK, CHUNK), 0)
    cc = lax.broadcasted_iota(jnp.int32, (CHUNK, CHUNK), 1)
    rpos = (lax.broadcasted_iota(jnp.int32, (STACK, LANE), 0) & (CHUNK - 1)).astype(F32)
    eye_w = jnp.where(wr == wc, 1.0, 0.0)

    def rows(x, h):
        return x[h * CHUNK:(h + 1) * CHUNK]

    def by_head(parts):
        out = parts[N_HEADS - 1]
        for h in range(N_HEADS - 2, -1, -1):
            out = jnp.where(whead == h, parts[h], out)
        return out

    def block_diag(w):
        return tuple(jnp.where(same_head, jnp.concatenate([x] * N_HEADS, axis=0), jnp.zeros((STACK, STACK), BF16))
                     for x in _split2(w))

    def wide_times(a, bh, bl):
        ah, al = _split2(a)
        hh = _dot(jnp.concatenate([ah, al], axis=0), bh)
        return hh[:CHUNK] + hh[CHUNK:] + _dot(ah, bl)

    ret_const = []
    for d in range(2):
        lg4 = jnp.concatenate(
            [-_softplus(-jnp.full((CHUNK, STACK), logit_ref[d * N_HEADS + h], F32)) for h in range(N_HEADS)], axis=0)
        tri4 = (r4 >= c4) if d == 0 else (r4 <= c4)
        dist4 = ((r4 - c4) if d == 0 else (c4 - r4)).astype(F32)
        dec4 = jnp.exp(jnp.where(same_head, jnp.where(tri4, lg4 * dist4, -jnp.inf), -jnp.inf))
        spos = rpos if d == 0 else (CHUNK - 1.0) - rpos
        lgc = lg4[:, :LANE]
        ret_const.append((dec4, jnp.exp(lgc * (spos + 1.0)), jnp.exp(lgc * ((CHUNK - 1.0) - spos))))

    setup = []
    for n in range(chunks):
        tok = slice(n * CHUNK, (n + 1) * CHUNK)

        def stack(ref):
            return jnp.concatenate([ref[tok, h * LANE:(h + 1) * LANE] for h in range(N_HEADS)], axis=0)

        q4, k4, v4 = stack(rq_ref), stack(rk_ref), stack(rv_ref)
        q4b, k4b, v4b = q4.astype(BF16), k4.astype(BF16), v4.astype(BF16)
        rvb_ref[n] = v4b
        qk_ret = _dot_nt(q4b, k4b)
        gq4, gk4, gv4 = stack(cq_ref), stack(ck_ref), stack(cv_ref)
        gq4b, gk4b = gq4.astype(BF16), gk4.astype(BF16)
        gb = gb_ref[tok, :]
        gt = gt_ref[n]
        gh, gl = _split2(gb)
        gll = (gb - gh.astype(F32) - gl.astype(F32)).astype(BF16)
        th, tl = _split2(gt)
        tll = (gt - th.astype(F32) - tl.astype(F32)).astype(BF16)

        for d in range(2):
            dec4, qdec, kdec = ret_const[d]
            rin_ref[n, d] = _dot((qk_ret * dec4).astype(BF16), v4b)
            rqd_ref[n, d] = (q4 * qdec).astype(BF16)
            rkd_ref[n, d] = (k4 * kdec).astype(BF16)

        for d in range(2):
            fwd = d == 0
            low = jnp.where((rc >= cc) if fwd else (rc <= cc), 1.0, 0.0).astype(BF16)
            upp_w = jnp.where((wr <= wc) if fwd else (wr >= wc), 1.0, 0.0).astype(BF16)
            tri_w = (wr >= wc) if fwd else (wr <= wc)
            tri_c = (rc >= cc) if fwd else (rc <= cc)
            gc_col = _dot(low, gh) + _dot(low, gl) + _dot(low, gll)
            gc_row = _dot(th, upp_w) + _dot(tl, upp_w) + _dot(tll, upp_w)
            last_row = CHUNK - 1 if fwd else 0
            idx = [d * N_HEADS + h for h in range(N_HEADS)]
            gcol = [gc_col[:, i:i + 1] for i in idx]
            glast = [gc_col[last_row:last_row + 1, i:i + 1] for i in idx]
            beta = [gb[:, 2 * N_HEADS + i:2 * N_HEADS + i + 1] for i in idx]
            col4 = lambda parts: jnp.concatenate([jnp.broadcast_to(p, (CHUNK, 1)) for p in parts], axis=0)
            gcol4, glast4, beta4 = col4(gcol), col4(glast), col4(beta)
            eg4 = jnp.exp(gcol4)
            kb4 = gk4 * beta4
            gcol_w = by_head([jnp.broadcast_to(g, (CHUNK, STACK)) for g in gcol])
            grow_w = by_head([jnp.broadcast_to(gc_row[i:i + 1, :], (CHUNK, STACK)) for i in idx])
            dmat_w = jnp.exp(jnp.where(tri_w, gcol_w - grow_w, -jnp.inf))
            kk = _dot_nt(kb4.astype(BF16), gk4b)
            kk_w = by_head([rows(kk, h) for h in range(N_HEADS)])
            p = jnp.where(wr == wc, 0.0, -(kk_w * dmat_w))
            setup.append((n, d, p, gcol, glast, gcol4, glast4, beta4, eg4, kb4, gc_row, idx, tri_c,
                          gq4, gk4, gv4, gq4b, gk4b))

    ps = [s[2] for s in setup]
    bds = [block_diag(p) for p in ps]
    tinvs = [eye_w + p for p in ps]
    for _ in range(5):
        ps = [wide_times(p, *bd) for p, bd in zip(ps, bds)]
        bds = [block_diag(p) for p in ps]
        tinvs = [t + wide_times(t, *bd) for t, bd in zip(tinvs, bds)]
    t_bds = [block_diag(t) for t in tinvs]
    for (n, d, _, gcol, glast, gcol4, glast4, beta4, eg4, kb4, gc_row, idx, tri_c,
         gq4, gk4, gv4, gq4b, gk4b), (th_bd, tl_bd) in zip(setup, t_bds):
        rh, rl = _split2(jnp.concatenate([gv4 * beta4, kb4 * eg4], axis=1))
        uw = _dot(th_bd, rh) + _dot(th_bd, rl) + _dot(tl_bd, rh)
        gu_ref[n, d] = uw[:, :GDN_DV]
        w4 = uw[:, GDN_DV:].astype(BF16)
        qd4 = (gq4 * eg4).astype(BF16)
        gkd_ref[n, d] = (gk4 * jnp.exp(glast4 - gcol4)).astype(BF16)
        for h in range(N_HEADS):
            gwq_ref[n, d, h] = jnp.concatenate([rows(w4, h), rows(qd4, h)], axis=0)
            dm = jnp.exp(jnp.where(tri_c, gcol[h] - gc_row[idx[h]:idx[h] + 1, :CHUNK], -jnp.inf))
            gat_ref[n, d, h] = (_dot_nt(rows(gq4b, h), rows(gk4b, h)) * dm).astype(BF16)
        gcd_ref[n, d] = jnp.concatenate(
            [jnp.broadcast_to(jnp.exp(g), (1, LANE)) for g in glast] + [jnp.zeros((SUBLANE - N_HEADS, LANE), F32)],
            axis=0)


def _scan_kernel(fblk, bblk, first, last, s0idx, soidx, logit_ref, *refs):
    del fblk, bblk, s0idx, soidx
    views = (refs[0:9], refs[9:18])
    s0r_ref, s0g_ref = refs[18], refs[19]
    o_refs = ((refs[20], refs[22]), (refs[21], refs[23]))
    sro_ref, sgo_ref = refs[24], refs[25]
    sr_ref, sg_ref = refs[26], refs[27]
    t = pl.program_id(0)

    @pl.when(first[t] == 1)
    def _():
        sr_ref[...] = s0r_ref[...]
        sg_ref[...] = s0g_ref[...]

    @pl.when(first[t] == 2)
    def _():
        sr_ref[...] = jnp.zeros(sr_ref.shape, F32)
        sg_ref[...] = jnp.zeros(sg_ref.shape, F32)

    chains = [(d, h) for d in range(2) for h in range(N_HEADS)]
    s_gdn = {c: sg_ref[c[0], c[1]] for c in chains}
    s_ret = {c: sr_ref[c[0], c[1]] for c in chains}
    ret_decay = {(d, h): jnp.exp(-_softplus(-jnp.full((1, LANE), logit_ref[d * N_HEADS + h], F32)) * float(CHUNK))
                 for d, h in chains}
    n_sub = views[0][0].shape[0]
    for sub in range(n_sub):
        at = (sub, n_sub - 1 - sub)

        def rows(d, k, h):
            return views[d][k][at[d], h * CHUNK:(h + 1) * CHUNK, :]

        ws = {(d, h): _dot(views[d][5][at[d], h], s_gdn[d, h].astype(BF16)) for d, h in chains}
        inter = {(d, h): _dot(rows(d, 1, h), s_ret[d, h].astype(BF16)) for d, h in chains}
        kv = {(d, h): _dot_tn(rows(d, 2, h), rows(d, 3, h)) for d, h in chains}
        v_new = {(d, h): (rows(d, 4, h) - ws[d, h][:CHUNK]).astype(BF16) for d, h in chains}
        intra = {(d, h): _dot(views[d][6][at[d], h], v_new[d, h]) for d, h in chains}
        kvn = {(d, h): _dot_tn(rows(d, 7, h), v_new[d, h]) for d, h in chains}
        for d, h in chains:
            tok = slice(at[d] * CHUNK, (at[d] + 1) * CHUNK)
            lanes = slice(h * LANE, (h + 1) * LANE)
            o_refs[d][0][tok, lanes] = rows(d, 0, h) + inter[d, h]
            o_refs[d][1][tok, lanes] = ws[d, h][CHUNK:] + intra[d, h]
            s_ret[d, h] = s_ret[d, h] * ret_decay[d, h] + kv[d, h]
            s_gdn[d, h] = s_gdn[d, h] * views[d][8][at[d], h:h + 1, :] + kvn[d, h]
    for d, h in chains:
        sr_ref[d, h] = s_ret[d, h]
        sg_ref[d, h] = s_gdn[d, h]

    @pl.when(last[t] == 1)
    def _():
        sro_ref[...] = sr_ref[...]
        sgo_ref[...] = sg_ref[...]


def _scan_tables(lay):
    fblk, bblk, first, last, s0idx, soidx = [], [], [], [], [], []
    blk = SCAN_CHUNKS * CHUNK
    assert lay.seq % blk == 0 and lay.dec_seq % blk == 0
    cp, cs = lay.seq // blk, lay.dec_seq // blk
    for b in range(lay.dec_batch):
        base = lay.n_p // blk + b * cs
        for s in range(cs):
            fblk.append(base + s); bblk.append(base + cs - 1 - s)
            first.append(1 if s == 0 else 0); last.append(0); s0idx.append(b); soidx.append(0)
    for b in range(lay.batch):
        base = b * cp
        for s in range(cp):
            fblk.append(base + s); bblk.append(base + cp - 1 - s)
            first.append(2 if s == 0 else 0); last.append(1 if s == cp - 1 else 0)
            s0idx.append(lay.dec_batch - 1); soidx.append(b)
    return [jnp.asarray(np.asarray(a, np.int32)) for a in (fblk, bblk, first, last, s0idx, soidx)]


def _scan_prep(lay, rq, rk, rv, cq, ck, cv, gb, gt, logit):
    n_chunks = lay.n // CHUNK
    nc = PREP_CHUNKS
    assert n_chunks % nc == 0
    width = N_HEADS * LANE

    def tok(wd):
        return pl.BlockSpec((nc * CHUNK, wd), lambda i: (i, 0))

    def per_dir(shape, dtype):
        spec = pl.BlockSpec((nc, 2) + shape, lambda i: (i,) + (0,) * (1 + len(shape)))
        return spec, jax.ShapeDtypeStruct((n_chunks, 2) + shape, dtype)

    outs = [
        per_dir((STACK, LANE), F32),
        per_dir((STACK, LANE), BF16),
        per_dir((STACK, LANE), BF16),
        (pl.BlockSpec((nc, STACK, LANE), lambda i: (i, 0, 0)), jax.ShapeDtypeStruct((n_chunks, STACK, LANE), BF16)),
        per_dir((STACK, LANE), F32),
        per_dir((N_HEADS, 2 * CHUNK, LANE), BF16),
        per_dir((N_HEADS, CHUNK, CHUNK), BF16),
        per_dir((STACK, LANE), BF16),
        per_dir((SUBLANE, LANE), F32),
    ]
    return pl.pallas_call(
        functools.partial(_prep_kernel, chunks=nc),
        grid=(n_chunks // nc,),
        in_specs=[pl.BlockSpec(memory_space=pltpu.SMEM)] + [tok(width)] * 6
                 + [tok(LANE), pl.BlockSpec((nc, 4 * GDN_HEADS, CHUNK), lambda i: (i, 0, 0))],
        out_specs=[o[0] for o in outs],
        out_shape=[o[1] for o in outs],
        compiler_params=_params("parallel"),
        name="scan_prep",
    )(logit.reshape(-1).astype(F32), rq, rk, rv, cq, ck, cv, gb, gt)


def _even_scan(lay, prep, logit, s0_ret, s0_gdn):
    tables = _scan_tables(lay)
    steps = int(tables[0].shape[0])
    width = N_HEADS * LANE

    def view(which):
        def blk(arr, with_dir=True):
            shape = arr.shape[2:] if with_dir else arr.shape[1:]
            if with_dir:
                return pl.BlockSpec((SCAN_CHUNKS, None) + shape,
                                    lambda t, *tb: (tb[which][t], which) + (0,) * len(shape))
            return pl.BlockSpec((SCAN_CHUNKS,) + shape, lambda t, *tb: (tb[which][t],) + (0,) * len(shape))
        return [blk(a, with_dir=(i != 3)) for i, a in enumerate(prep)]

    state_shape = (2, N_HEADS, LANE, LANE)
    s0_spec = pl.BlockSpec((None,) + state_shape, lambda t, *tb: (tb[4][t], 0, 0, 0, 0))
    so_spec = pl.BlockSpec((None,) + state_shape, lambda t, *tb: (tb[5][t], 0, 0, 0, 0))
    out_f = pl.BlockSpec((SCAN_CHUNKS * CHUNK, width), lambda t, *tb: (tb[0][t], 0))
    out_b = pl.BlockSpec((SCAN_CHUNKS * CHUNK, width), lambda t, *tb: (tb[1][t], 0))
    o_sds = jax.ShapeDtypeStruct((lay.n, width), F32)
    so_sds = jax.ShapeDtypeStruct((lay.batch,) + state_shape, F32)
    args = list(prep)
    return pl.pallas_call(
        _scan_kernel,
        grid_spec=pltpu.PrefetchScalarGridSpec(
            num_scalar_prefetch=6,
            grid=(steps,),
            in_specs=[pl.BlockSpec(memory_space=pltpu.SMEM)] + view(0) + view(1) + [s0_spec, s0_spec],
            out_specs=[out_f, out_b, out_f, out_b, so_spec, so_spec],
            scratch_shapes=[pltpu.VMEM(state_shape, F32), pltpu.VMEM(state_shape, F32)],
        ),
        out_shape=[o_sds, o_sds, o_sds, o_sds, so_sds, so_sds],
        compiler_params=_params("arbitrary"),
        name="even_scan",
    )(*tables, logit.reshape(-1).astype(F32), *args, *args, s0_ret, s0_gdn)


def _odd_proj_kernel(x_ref, mod_ref, nw_ref, w_ref, qnw_ref, knw_ref, cos_ref, sin_ref,
                     q_ref, kd_ref, vt_ref, ks_ref, vs_ref, *, p_tiles):
    h = _adaln(x_ref[...], nw_ref[...], mod_ref[3:4, :], mod_ref[4:5, :]).astype(BF16)
    tm = h.shape[0]
    lane = lax.broadcasted_iota(jnp.int32, (tm, LANE), 1)
    lo = lane < ATT_HD
    first16 = (lane & (2 * ROPE_PAIRS - 1)) < ROPE_PAIRS
    cos, sin = cos_ref[...], sin_ref[...]

    def norm_rope(xs, w2):
        sq = xs * xs
        ms_lo = jnp.sum(jnp.where(lo, sq, 0.0), axis=-1, keepdims=True) * (1.0 / ATT_HD)
        ms_hi = jnp.sum(jnp.where(lo, 0.0, sq), axis=-1, keepdims=True) * (1.0 / ATT_HD)
        xn = xs * jnp.where(lo, lax.rsqrt(ms_lo + EPS), lax.rsqrt(ms_hi + EPS)) * w2
        partner = jnp.where(first16, pltpu.roll(xn, LANE - ROPE_PAIRS, 1), pltpu.roll(xn, ROPE_PAIRS, 1))
        return xn * cos + partner * sin

    def both_halves(xs, j):
        swapped = pltpu.roll(xs, ATT_HD, 1)
        return jnp.where(lo, xs, swapped) if j % 2 == 0 else jnp.where(lo, swapped, xs)

    for t in range(ATT_Q // MXU_TILE):
        y = _dot(h, w_ref[:, t * MXU_TILE:(t + 1) * MXU_TILE])
        for p in (2 * t, 2 * t + 1):
            qs = norm_rope(y[:, (p % 2) * LANE:(p % 2 + 1) * LANE], qnw_ref[...])
            q_ref[:, p * LANE:(p + 1) * LANE] = (qs * Q_SCALE).astype(BF16)
    is_prompt = pl.program_id(0) < p_tiles
    assert ATT_KV == MXU_TILE
    yk = _dot(h, w_ref[:, ATT_Q:ATT_Q + ATT_KV])
    yv = _dot(h, w_ref[:, ATT_Q + ATT_KV:ATT_Q + 2 * ATT_KV])
    for p in range(ATT_KV // LANE):
        ks = norm_rope(yk[:, p * LANE:(p + 1) * LANE], knw_ref[...])
        vs = yv[:, p * LANE:(p + 1) * LANE]
        vst = vs.T.astype(BF16)
        for j in (2 * p, 2 * p + 1):
            kd_ref[j] = both_halves(ks, j).astype(BF16)
            for c in range(tm // KEY_CHUNK):
                vt_ref[j, c] = vst[(j % 2) * ATT_HD:(j % 2 + 1) * ATT_HD, c * KEY_CHUNK:(c + 1) * KEY_CHUNK]

        @pl.when(is_prompt)
        def _():
            ks_ref[:, p * LANE:(p + 1) * LANE] = ks
            vs_ref[:, p * LANE:(p + 1) * LANE] = vs


def _rope_tables(lay):
    t = jnp.arange(lay.dec_seq)
    inv = ROPE_THETA ** (-jnp.arange(ROPE_PAIRS, dtype=F32) / ROPE_PAIRS)
    ar = (t // GRID_W).astype(F32)[:, None] * inv
    ac = (t % GRID_W).astype(F32)[:, None] * inv
    cos = jnp.concatenate([jnp.cos(ar), jnp.cos(ar), jnp.cos(ac), jnp.cos(ac)], axis=1)
    sin = jnp.concatenate([-jnp.sin(ar), jnp.sin(ar), -jnp.sin(ac), jnp.sin(ac)], axis=1)
    cos = jnp.concatenate([jnp.tile(cos, (1, 2)), jnp.ones((lay.tm, LANE), F32)], axis=0)
    sin = jnp.concatenate([jnp.tile(sin, (1, 2)), jnp.zeros((lay.tm, LANE), F32)], axis=0)
    return cos, sin


def _odd_proj(lay, x, mods, nw, w, qnw, knw):
    cos, sin = _rope_tables(lay)

    def table_block(i):
        return jnp.where(i < lay.p_tiles, lay.tiles_per_seq, lax.rem(jnp.maximum(i - lay.p_tiles, 0), lay.tiles_per_seq))

    tab = pl.BlockSpec((lay.tm, LANE), lambda i: (table_block(i), 0))
    std = pl.BlockSpec((lay.tm, ATT_KV), lambda i: (jnp.minimum(i, lay.p_tiles - 1), 0))
    tm = lay.tm
    return pl.pallas_call(
        functools.partial(_odd_proj_kernel, p_tiles=lay.p_tiles),
        grid=(lay.tiles,),
        in_specs=[lay.tok(D_MODEL), lay.mod(), _resident((1, D_MODEL)), _resident(w.shape),
                  _resident((1, LANE)), _resident((1, LANE)), tab, tab],
        out_specs=[
            lay.tok(ATT_Q),
            pl.BlockSpec((ATT_KV_HEADS, tm, LANE), lambda i: (0, i, 0)),
            pl.BlockSpec((ATT_KV_HEADS, tm // KEY_CHUNK, ATT_HD, KEY_CHUNK), lambda i: (0, i, 0, 0)),
            std, std,
        ],
        out_shape=[
            jax.ShapeDtypeStruct((lay.n, ATT_Q), BF16),
            jax.ShapeDtypeStruct((ATT_KV_HEADS, lay.n, LANE), BF16),
            jax.ShapeDtypeStruct((ATT_KV_HEADS, lay.n // KEY_CHUNK, ATT_HD, KEY_CHUNK), BF16),
            jax.ShapeDtypeStruct((lay.n_p, ATT_KV), F32),
            jax.ShapeDtypeStruct((lay.n_p, ATT_KV), F32),
        ],
        compiler_params=_params("arbitrary"),
        name="odd_proj",
    )(x, mods, nw.reshape(1, D_MODEL), w, jnp.tile(qnw.reshape(1, ATT_HD), (1, 2)),
      jnp.tile(knw.reshape(1, ATT_HD), (1, 2)), cos, sin)


def _attn_kernel(*refs, cached):
    if cached:
        zero_ref, q_ref, kn_ref, vn_ref, kc_ref, vc_ref, _, o_ref, *s_scr = refs
    else:
        zero_ref, q_ref, kn_ref, vn_ref, o_ref, *s_scr = refs
    z = zero_ref[0]
    tq = MXU_TILE
    n_sub = q_ref.shape[0] // tq
    n_kv = kn_ref.shape[0]
    gw = ATT_GROUP * ATT_HD
    ck = KEY_CHUNK
    n_c = kc_ref.shape[1] // ck if cached else 0
    n_n = kn_ref.shape[1] // ck
    lo = lax.broadcasted_iota(jnp.int32, (tq, LANE), 1) < ATT_HD
    qms, kv_of = [], []
    for s in range(n_sub):
        for j in range(n_kv):
            for g in range(ATT_GROUP):
                q2 = q_ref[s * tq:(s + 1) * tq, j * gw + (g // 2) * LANE:j * gw + (g // 2 + 1) * LANE]
                qms.append(jnp.where(lo, q2, jnp.zeros_like(q2)) if g % 2 == 0
                           else jnp.where(lo, jnp.zeros_like(q2), q2))
                kv_of.append(j)
    n_units = len(qms)

    def keys(j, c):
        return kc_ref[j, c * ck:(c + 1) * ck, :] if c < n_c else kn_ref[j, (c - n_c) * ck:(c - n_c + 1) * ck, :]

    def values_t(j, c):
        return vc_ref[j, c] if c < n_c else vn_ref[j, c - n_c]

    ones_rows = jnp.where(lax.broadcasted_iota(jnp.int32, (2 * SUBLANE, ck), 0) == 0, 1.0, 0.0).astype(BF16)
    m_fin = [None] * n_units
    outs = [None] * n_units
    for ph in range(n_units + 1):
        ga = ph if ph < n_units else None
        gb = ph - 1 if ph >= 1 else None
        ma = jnp.full((1, tq), -jnp.inf, F32)
        acc = jnp.zeros((ATT_HD + 2 * SUBLANE, tq), F32)
        for c in range(n_c + n_n):
            if ga is not None:
                st = _dot_nt(keys(kv_of[ga], c), qms[ga])
                s_scr[ga % 2][z + c] = st
                ma = jnp.maximum(ma, jnp.max(st, axis=0, keepdims=True))
            if gb is not None:
                p = jnp.exp2(s_scr[gb % 2][z + c] - m_fin[gb]).astype(BF16)
                acc = acc + _dot(jnp.concatenate([values_t(kv_of[gb], c), ones_rows], axis=0), p)
        if ga is not None:
            m_fin[ga] = ma
        if gb is not None:
            outs[gb] = acc[:ATT_HD] / acc[ATT_HD:ATT_HD + 1]
    per_sub = n_kv * ATT_GROUP
    for s in range(n_sub):
        o_ref[s * tq:(s + 1) * tq, :] = jnp.concatenate(outs[s * per_sub:(s + 1) * per_sub], axis=0).T.astype(BF16)


def _attention(lay, q, kd, vt, kc, vc):
    gw = ATT_GROUP * ATT_HD
    o_sds = jax.ShapeDtypeStruct((lay.n, ATT_Q), BF16)
    ck = KEY_CHUNK
    assert lay.seq % ck == 0 and lay.dec_seq % ck == 0 and lay.past % ck == 0
    assert lay.seq % MXU_TILE == 0 and lay.dec_seq % MXU_TILE == 0 and ATT_Q_TILE % MXU_TILE == 0

    zero = jnp.zeros((1,), jnp.int32)
    zero_spec = pl.BlockSpec(memory_space=pltpu.SMEM)

    tq = min(ATT_Q_TILE, lay.seq)
    nq = lay.seq // tq
    nck = lay.seq // ck
    out = pl.pallas_call(
        functools.partial(_attn_kernel, cached=False),
        grid=(lay.batch, nq),
        in_specs=[
            zero_spec,
            pl.BlockSpec((tq, ATT_Q), lambda b, i: (b * nq + i, 0)),
            pl.BlockSpec((ATT_KV_HEADS, lay.seq, LANE), lambda b, i: (0, b, 0)),
            pl.BlockSpec((ATT_KV_HEADS, nck, ATT_HD, ck), lambda b, i: (0, b, 0, 0)),
        ],
        out_specs=pl.BlockSpec((tq, ATT_Q), lambda b, i: (b * nq + i, 0)),
        out_shape=o_sds,
        scratch_shapes=[pltpu.VMEM((nck, ck, MXU_TILE), F32)] * 2,
        compiler_params=_params("parallel", "parallel"),
        name="attn_context",
    )(zero, q, kd, vt)

    tq = min(ATT_Q_TILE, lay.dec_seq)
    nq = lay.dec_seq // tq
    nck = lay.dec_seq // ck
    qbase = lay.n_p // tq
    kbase = lay.n_p // lay.dec_seq
    assert lay.n_p % lay.dec_seq == 0
    return pl.pallas_call(
        functools.partial(_attn_kernel, cached=True),
        grid=(lay.dec_batch, ATT_KV_HEADS, nq),
        in_specs=[
            zero_spec,
            pl.BlockSpec((tq, gw), lambda b, j, i: (qbase + b * nq + i, j)),
            pl.BlockSpec((1, lay.dec_seq, LANE), lambda b, j, i: (j, kbase + b, 0)),
            pl.BlockSpec((1, nck, ATT_HD, ck), lambda b, j, i: (j, kbase + b, 0, 0)),
            pl.BlockSpec((None, 1, lay.past, LANE), lambda b, j, i: (b, j, 0, 0)),
            pl.BlockSpec((None, 1, lay.past // ck, ATT_HD, ck), lambda b, j, i: (b, j, 0, 0, 0)),
            pl.BlockSpec(memory_space=pl.ANY),
        ],
        out_specs=pl.BlockSpec((tq, gw), lambda b, j, i: (qbase + b * nq + i, j)),
        out_shape=o_sds,
        scratch_shapes=[pltpu.VMEM((lay.past // ck + nck, ck, MXU_TILE), F32)] * 2,
        input_output_aliases={6: 0},
        compiler_params=_params("parallel", "parallel", "parallel"),
        name="attn_latent",
    )(zero, q, kd, vt, kc, vc, out)


def _pad_state(s):
    return jnp.pad(s, [(0, 0)] * (s.ndim - 2) + [(0, LANE - s.shape[-2]), (0, 0)])


def _cache_keys(c):
    return jnp.tile(c.transpose(0, 2, 1, 3), (1, 1, 1, 2)).astype(BF16)


def _cache_values(c):
    b, t, kvh, hd = c.shape
    return c.reshape(b, t // KEY_CHUNK, KEY_CHUNK, kvh, hd).transpose(0, 3, 1, 4, 2).astype(BF16)


def kernel(x_prompt, x_sample, state_ret, state_gdn, cache_k, cache_v, c, c_ctx,
           mod_w, mod_b, norm_w, ffn_w_in, ffn_w_out, even_w_in, even_w_out,
           ret_decay_logit, ret_norm_w, gdn_conv_w, gdn_A_log, gdn_dt_bias, gdn_norm_w,
           odd_w_in, odd_w_out, q_norm_w, k_norm_w, final_norm_w):
    batch, seq, d = x_prompt.shape
    dec_batch, dec_seq, _ = x_sample.shape
    depth = mod_w.shape[0]
    lay = _Layout(batch, seq, dec_batch, dec_seq, cache_k.shape[2])
    lay_ffn = _Layout(batch, seq, dec_batch, dec_seq, cache_k.shape[2], tile=FFN_TILE)

    x = (x_prompt.reshape(lay.n_p, d), x_sample.reshape(lay.n_s, d))
    n_cond = -(-(1 + dec_batch) // (2 * SUBLANE)) * (2 * SUBLANE)
    cond = jnp.zeros((n_cond, d), F32).at[0].set(c_ctx).at[1:1 + dec_batch].set(c)
    mods = _modulation(cond, mod_w, mod_b)
    ffn_in, ffn_out = ffn_w_in.astype(BF16), ffn_w_out.astype(BF16)

    new_ret, new_gdn, new_k, new_v = [], [], [], []
    for l in range(depth):
        m = mods[l]
        last = l == depth - 1
        x = _ffn(lay_ffn, x, m, norm_w[l, 0], ffn_in, ffn_out, (l, 0), 0)
        if l % 2 == 0:
            e = l // 2
            rq, rk, rv, rg, gqkv, gg, gab = _even_proj(lay, x, m, norm_w[l, 1], _even_weight(even_w_in[e]))
            cq, ck, cv, gb, gt = _even_conv(lay, gqkv, gab, gdn_conv_w[e], gdn_A_log[e], gdn_dt_bias[e])
            prep = _scan_prep(lay, rq, rk, rv, cq, ck, cv, gb, gt, ret_decay_logit[e])
            orf, orb, ogf, ogb, sr, sg = _even_scan(
                lay, prep, ret_decay_logit[e], _pad_state(state_ret[:, e]), state_gdn[:, e])
            new_ret.append(sr[:, :, :, :RET_DK, :])
            new_gdn.append(sg)
            mixer = "even"
            mixer_ins = (orf, orb, ogf, ogb, rg, gg, ret_norm_w[e].reshape(1, LANE), gdn_norm_w[e].reshape(1, LANE),
                         even_w_out[e].astype(BF16))
        else:
            o = l // 2
            q, kd, vt, ks, vs = _odd_proj(lay, x, m, norm_w[l, 1], odd_w_in[o].astype(BF16), q_norm_w[o], k_norm_w[o])
            a = _attention(lay, q, kd, vt, _cache_keys(cache_k[:, o]), _cache_values(cache_v[:, o]))
            new_k.append(ks.reshape(batch, seq, ATT_KV_HEADS, ATT_HD))
            new_v.append(vs.reshape(batch, seq, ATT_KV_HEADS, ATT_HD))
            mixer = "odd"
            mixer_ins = (a, odd_w_out[o].astype(BF16))
        x = _ffn(lay, x, m, norm_w[l, 2], ffn_in, ffn_out, (l, 1), 2,
                 final_w=final_norm_w if last else None, mixer=mixer, mixer_ins=mixer_ins)

    y_prompt = x[0].reshape(batch, seq, d)
    y_sample = x[1].reshape(dec_batch, dec_seq, d)
    return (y_prompt, y_sample, jnp.stack(new_ret, axis=1), jnp.stack(new_gdn, axis=1),
            jnp.stack(new_k, axis=1), jnp.stack(new_v, axis=1))
```

```python
import functools

import numpy as np
import jax
import jax.numpy as jnp
from jax import lax
from jax.experimental import pallas as pl
from jax.experimental.pallas import tpu as pltpu

F32 = jnp.float32
BF16 = jnp.bfloat16

D_MODEL = 1024
GRID_W = 64
RET_HEADS = 4
RET_DK = 64
RET_DV = 128
GDN_HEADS = 4
GDN_DK = 128
GDN_DV = 128
CHUNK = 64
ATT_HEADS = 16
ATT_KV_HEADS = 4
ATT_HD = 64
ATT_GROUP = ATT_HEADS // ATT_KV_HEADS
ROPE_THETA = 10000.0
ROPE_PAIRS = ATT_HD // 4
FFN_HIDDEN = 2816
N_MOD = 9
EPS = 1e-6

RET_QK = RET_HEADS * RET_DK
RET_V = RET_HEADS * RET_DV
GDN_QK = GDN_HEADS * GDN_DK
GDN_V = GDN_HEADS * GDN_DV
ATT_Q = ATT_HEADS * ATT_HD
ATT_KV = ATT_KV_HEADS * ATT_HD

LANE = 128
SUBLANE = 8
V7X_VMEM_BYTES = 64 * 1024 * 1024
VMEM_LIMIT = V7X_VMEM_BYTES - 8 * 1024 * 1024

TOKEN_TILE = 512
FFN_TILE = 1024
ATT_Q_TILE = 1024
KEY_CHUNK = 256
Q_SCALE = ATT_HD ** -0.5 * float(np.log2(np.e))
MXU_TILE = 256
FFN_CHUNKS = ((0, 6 * MXU_TILE), (6 * MXU_TILE, FFN_HIDDEN))
HEAD_PAD = LANE
PREP_CHUNKS = 4
SCAN_CHUNKS = 4


def _params(*sem):
    return pltpu.CompilerParams(dimension_semantics=sem, vmem_limit_bytes=VMEM_LIMIT)


def _resident(shape):
    nd = len(shape)
    return pl.BlockSpec(shape, lambda *_: (0,) * nd, pipeline_mode=pl.Buffered(1))


def _resident_at(arr, lead):
    shape = arr.shape[len(lead):]
    return pl.BlockSpec((None,) * len(lead) + shape, lambda *_: tuple(lead) + (0,) * len(shape),
                        pipeline_mode=pl.Buffered(1))


def _dot(a, b):
    return jnp.dot(a, b, preferred_element_type=F32)


def _dot_nt(a, b):
    return lax.dot_general(a, b, (((1,), (1,)), ((), ())), preferred_element_type=F32)


def _dot_tn(a, b):
    return lax.dot_general(a, b, (((0,), (0,)), ((), ())), preferred_element_type=F32)


def _split2(a):
    hi = a.astype(BF16)
    lo = (a - hi.astype(F32)).astype(BF16)
    return hi, lo


def _dot3(a, b):
    ah, al = _split2(a)
    bh, bl = _split2(b)
    return _dot(ah, bh) + _dot(ah, bl) + _dot(al, bh)


def _silu(x):
    return x * jax.nn.sigmoid(x)


def _softplus(x):
    return jnp.maximum(x, 0.0) + jnp.log(1.0 + jnp.exp(-jnp.abs(x)))


def _adaln(x, nw, shift, scale):
    ms = jnp.mean(x * x, axis=-1, keepdims=True)
    return (x * lax.rsqrt(ms + EPS)) * nw * (1.0 + scale) + shift


def _head_rmsnorm(x, w, n_heads):
    parts = []
    for h in range(n_heads):
        xs = x[:, h * LANE:(h + 1) * LANE]
        ms = jnp.mean(xs * xs, axis=-1, keepdims=True)
        parts.append(xs * lax.rsqrt(ms + EPS) * w)
    return jnp.concatenate(parts, axis=1)


def _mod_kernel(cond_ref, w_ref, b_ref, o_ref):
    c = cond_ref[...]
    o_ref[0] = _dot(_silu(c).astype(BF16), w_ref[0].astype(BF16)) + b_ref[0]


def _modulation(cond, mod_w, mod_b):
    depth, d, n = mod_w.shape
    r = cond.shape[0]
    tn = n // 8
    out = pl.pallas_call(
        _mod_kernel,
        grid=(depth, n // tn),
        in_specs=[
            pl.BlockSpec((r, d), lambda l, j: (0, 0)),
            pl.BlockSpec((1, d, tn), lambda l, j: (l, 0, j)),
            pl.BlockSpec((1, 1, tn), lambda l, j: (l, 0, j)),
        ],
        out_specs=pl.BlockSpec((1, r, tn), lambda l, j: (l, 0, j)),
        out_shape=jax.ShapeDtypeStruct((depth, r, n), F32),
        compiler_params=_params("parallel", "parallel"),
        name="modulation",
    )(cond, mod_w, mod_b.reshape(depth, 1, n))
    return out.reshape(depth, r, N_MOD, d)


class _Layout:
    def __init__(self, batch, seq, dec_batch, dec_seq, past, tile=TOKEN_TILE):
        self.batch, self.seq, self.dec_batch, self.dec_seq, self.past = batch, seq, dec_batch, dec_seq, past
        self.n_p = batch * seq
        self.n_s = dec_batch * dec_seq
        self.n = self.n_p + self.n_s
        self.tm = min(tile, self.n_p, dec_seq)
        assert self.n_p % self.tm == 0 and dec_seq % self.tm == 0
        self.tiles = self.n // self.tm
        self.p_tiles = self.n_p // self.tm
        self.tiles_per_seq = dec_seq // self.tm

    def group(self, i):
        return jnp.where(i < self.p_tiles, 0, 1 + (i - self.p_tiles) // self.tiles_per_seq)

    def tok(self, width, tm=None):
        tm = tm or self.tm
        return pl.BlockSpec((tm, width), lambda i: (i, 0))

    def mod(self):
        return pl.BlockSpec((None, N_MOD, D_MODEL), lambda i: (self.group(i), 0, 0))


def _ffn_kernel(*refs, j, first, final, p_tiles, mixer):
    refs = list(refs)
    x_refs = [refs.pop(0) for _ in range(2 if first else 1)]
    n_mix = {None: 0, "odd": 2, "even": 9}[mixer]
    mix_refs = [refs.pop(0) for _ in range(n_mix)]
    mod_ref, nw_ref, win_ref, wout_ref = (refs.pop(0) for _ in range(4))
    fw_ref = refs.pop(0) if final else None
    o_refs = refs
    is_prompt = pl.program_id(0) < p_tiles
    x = jnp.where(is_prompt, x_refs[0][...], x_refs[1][...]) if first else x_refs[0][...]
    if mixer == "odd":
        a_ref, wo_ref = mix_refs
        x = x + mod_ref[5:6, :] * _dot(a_ref[...], wo_ref[...])
    elif mixer == "even":
        rf_ref, rb_ref, gf_ref, gb_ref, rg_ref, gg_ref, rnw_ref, gnw_ref, wo_ref = mix_refs
        ret = _head_rmsnorm(rf_ref[...] + rb_ref[...], rnw_ref[...], RET_HEADS) * _silu(rg_ref[...])
        gdn = _head_rmsnorm(gf_ref[...] + gb_ref[...], gnw_ref[...], GDN_HEADS) * _silu(gg_ref[...])
        x = x + mod_ref[5:6, :] * _dot(jnp.concatenate([ret, gdn], axis=1).astype(BF16), wo_ref[...])
    h = _adaln(x, nw_ref[...], mod_ref[3 * j:3 * j + 1, :], mod_ref[3 * j + 1:3 * j + 2, :]).astype(BF16)
    acc = None
    for lo, hi in FFN_CHUNKS:
        a = _dot(h, win_ref[:, lo:hi])
        b = _dot(h, win_ref[:, FFN_HIDDEN + lo:FFN_HIDDEN + hi])
        y = _dot((_silu(a) * b).astype(BF16), wout_ref[lo:hi, :])
        acc = y if acc is None else acc + y
    out = x + (0.5 * mod_ref[3 * j + 2:3 * j + 3, :]) * acc
    if not final:
        o_refs[0][...] = out
        return
    ms = jnp.mean(out * out, axis=-1, keepdims=True)
    out = out * lax.rsqrt(ms + EPS) * fw_ref[...]

    @pl.when(is_prompt)
    def _():
        o_refs[0][...] = out

    @pl.when(jnp.logical_not(is_prompt))
    def _():
        o_refs[1][...] = out


def _ffn(lay, xs, mods, nw, w_in, w_out, which, j, final_w=None, mixer=None, mixer_ins=()):
    first = isinstance(xs, tuple)
    final = final_w is not None
    prompt_rows = pl.BlockSpec((lay.tm, D_MODEL), lambda i: (jnp.minimum(i, lay.p_tiles - 1), 0))
    latent_rows = pl.BlockSpec((lay.tm, D_MODEL), lambda i: (jnp.maximum(i - lay.p_tiles, 0), 0))
    ins = list(xs) if first else [xs]
    specs = [prompt_rows, latent_rows] if first else [lay.tok(D_MODEL)]
    for arr in mixer_ins:
        ins.append(arr)
        specs.append(lay.tok(arr.shape[1]) if arr.shape[0] == lay.n else _resident(arr.shape))
    ins += [mods, nw.reshape(1, D_MODEL), w_in, w_out]
    specs += [lay.mod(), _resident((1, D_MODEL)), _resident_at(w_in, which), _resident_at(w_out, which)]
    if final:
        ins.append(final_w.reshape(1, D_MODEL))
        specs.append(_resident((1, D_MODEL)))
        out_specs = [prompt_rows, latent_rows]
        out_shape = [jax.ShapeDtypeStruct((lay.n_p, D_MODEL), F32), jax.ShapeDtypeStruct((lay.n_s, D_MODEL), F32)]
    else:
        out_specs = lay.tok(D_MODEL)
        out_shape = jax.ShapeDtypeStruct((lay.n, D_MODEL), F32)
    return pl.pallas_call(
        functools.partial(_ffn_kernel, j=j, first=first, final=final, p_tiles=lay.p_tiles, mixer=mixer),
        grid=(lay.tiles,),
        in_specs=specs,
        out_specs=out_specs,
        out_shape=out_shape,
        compiler_params=_params("arbitrary"),
        name=("ffn_final" if final else ("ffn_first" if first else "ffn")) + ("_" + mixer if mixer else ""),
    )(*ins)


EVEN_WIDTHS = (RET_HEADS * HEAD_PAD, RET_HEADS * HEAD_PAD, RET_V, RET_V, 2 * GDN_QK + GDN_V, GDN_V, LANE)


def _even_proj_kernel(x_ref, prev_ref, next_ref, mod_ref, nw_ref, w_ref, cw_ref, alog_ref, dt_ref,
                      rq_ref, rk_ref, rv_ref, rg_ref, gg_ref, cq_ref, ck_ref, cv_ref, gb_ref, gt_ref,
                      *, n_p, seq, dec_seq):
    tm = x_ref.shape[0]
    shift, scale = mod_ref[3:4, :], mod_ref[4:5, :]
    h = _adaln(x_ref[...], nw_ref[...], shift, scale).astype(BF16)
    halo = jnp.concatenate([prev_ref[...], next_ref[...]], axis=0)
    hh = _adaln(halo, nw_ref[...], shift, scale).astype(BF16)
    offs = np.cumsum((0,) + EVEN_WIDTHS)
    col = lambda i: slice(int(offs[i]), int(offs[i + 1]))

    start = pl.program_id(0) * tm
    in_prompt = start < n_p
    rel = jnp.where(in_prompt, start, start - n_p)
    slen = jnp.where(in_prompt, seq, dec_seq)
    row = lax.broadcasted_iota(jnp.int32, (tm, GDN_QK), 0)
    pos = (rel + row) & (slen - 1)

    def conv(part):
        lanes = slice(int(offs[4]) + part * GDN_QK, int(offs[4]) + (part + 1) * GDN_QK)
        cw = cw_ref[:, part * GDN_QK:(part + 1) * GDN_QK]
        y = _dot(h, w_ref[:, lanes])
        yh = _dot(hh, w_ref[:, lanes])
        ym1 = jnp.where(pos == 0, 0.0, jnp.where(row == 0, yh[SUBLANE - 1:SUBLANE, :], pltpu.roll(y, 1, 0)))
        yp1 = jnp.where(pos == slen - 1, 0.0,
                        jnp.where(row == tm - 1, yh[SUBLANE:SUBLANE + 1, :], pltpu.roll(y, tm - 1, 0)))
        return _silu(ym1 * cw[0:1, :] + y * cw[1:2, :] + yp1 * cw[2:3, :])

    def l2norm_heads(y, o_ref, scale):
        for hd in range(GDN_HEADS):
            v = y[:, hd * LANE:(hd + 1) * LANE]
            o_ref[:, hd * LANE:(hd + 1) * LANE] = v * lax.rsqrt(jnp.sum(v * v, axis=-1, keepdims=True) + EPS) * scale

    l2norm_heads(conv(0), cq_ref, GDN_DK ** -0.5)
    rq_ref[...] = _dot(h, w_ref[:, col(0)])
    rk_ref[...] = _dot(h, w_ref[:, col(1)])
    l2norm_heads(conv(1), ck_ref, 1.0)
    rv_ref[...] = _dot(h, w_ref[:, col(2)])
    rg_ref[...] = _dot(h, w_ref[:, col(3)])
    cv_ref[...] = conv(2)
    gg_ref[...] = _dot(h, w_ref[:, col(5)])
    g = _dot(h, w_ref[:, col(6)])
    lane = lax.broadcasted_iota(jnp.int32, g.shape, 1)
    gb = jnp.where(lane < 2 * GDN_HEADS, -jnp.exp(alog_ref[...]) * _softplus(g + dt_ref[...]), jax.nn.sigmoid(g))
    gb_ref[...] = gb
    gbt = gb.T
    for c in range(tm // CHUNK):
        gt_ref[c] = gbt[0:4 * GDN_HEADS, c * CHUNK:(c + 1) * CHUNK]


def _even_proj(lay, x, mods, nw, w, conv_w, a_log, dt_bias):
    tm = lay.tm
    assert lay.seq & (lay.seq - 1) == 0 and lay.dec_seq & (lay.dec_seq - 1) == 0 and tm % LANE == 0
    nblk8 = lay.n // SUBLANE
    r = tm // SUBLANE
    width = GDN_V
    assert all(wd == width for wd in EVEN_WIDTHS[:4] + EVEN_WIDTHS[5:6]) and GDN_QK == width

    def pad_row(v):
        v = v.reshape(1, -1).astype(F32)
        return jnp.pad(v, ((0, 0), (0, LANE - v.shape[1])))

    tok_out = [(width, F32)] * 8 + [(LANE, F32)]
    return pl.pallas_call(
        functools.partial(_even_proj_kernel, n_p=lay.n_p, seq=lay.seq, dec_seq=lay.dec_seq),
        grid=(lay.tiles,),
        in_specs=[
            lay.tok(D_MODEL),
            pl.BlockSpec((SUBLANE, D_MODEL), lambda i: (jnp.maximum(i * r - 1, 0), 0)),
            pl.BlockSpec((SUBLANE, D_MODEL), lambda i: (jnp.minimum((i + 1) * r, nblk8 - 1), 0)),
            lay.mod(), _resident((1, D_MODEL)), _resident(w.shape), _resident(conv_w.shape),
            _resident((1, LANE)), _resident((1, LANE)),
        ],
        out_specs=[lay.tok(wd) for wd, _ in tok_out]
                  + [pl.BlockSpec((tm // CHUNK, 4 * GDN_HEADS, CHUNK), lambda i: (i, 0, 0))],
        out_shape=[jax.ShapeDtypeStruct((lay.n, wd), dt) for wd, dt in tok_out]
                  + [jax.ShapeDtypeStruct((lay.n // CHUNK, 4 * GDN_HEADS, CHUNK), F32)],
        compiler_params=_params("parallel"),
        name="even_proj",
    )(x, x, x, mods, nw.reshape(1, D_MODEL), w, conv_w, pad_row(a_log), pad_row(dt_bias))


def _even_weight(w):
    d = w.shape[0]
    o = 0
    rq = w[:, o:o + RET_QK]; o += RET_QK
    rk = w[:, o:o + RET_QK]; o += RET_QK
    rest = w[:, o:o + 2 * RET_V + 2 * GDN_QK + 2 * GDN_V]; o += 2 * RET_V + 2 * GDN_QK + 2 * GDN_V
    gab = w[:, o:]

    def pad_heads(m):
        m = m.reshape(d, RET_HEADS, RET_DK)
        return jnp.pad(m, ((0, 0), (0, 0), (0, HEAD_PAD - RET_DK))).reshape(d, RET_HEADS * HEAD_PAD)

    gab = jnp.pad(gab, ((0, 0), (0, LANE - gab.shape[1])))
    return jnp.concatenate([pad_heads(rq), pad_heads(rk * RET_DK ** -0.5), rest, gab], axis=1).astype(BF16)


N_HEADS = 4
STACK = N_HEADS * CHUNK


def _prep_kernel(logit_ref, rq_ref, rk_ref, rv_ref, cq_ref, ck_ref, cv_ref, gb_ref, gt_ref,
                 rin_ref, rqd_ref, rkd_ref, rvb_ref, gu_ref, gwq_ref, gat_ref, gkd_ref, gcd_ref, *, chunks):
    wr = lax.broadcasted_iota(jnp.int32, (CHUNK, STACK), 0)
    wcol = lax.broadcasted_iota(jnp.int32, (CHUNK, STACK), 1)
    wc = wcol & (CHUNK - 1)
    whead = wcol >> 6
    r4 = lax.broadcasted_iota(jnp.int32, (STACK, STACK), 0)
    c4 = lax.broadcasted_iota(jnp.int32, (STACK, STACK), 1)
    same_head = (r4 >> 6) == (c4 >> 6)
    rc = lax.broadcasted_iota(jnp.int32, (CHUNK, CHUNK), 0)
    cc = lax.broadcasted_iota(jnp.int32, (CHUNK, CHUNK), 1)
    rpos = (lax.broadcasted_iota(jnp.int32, (STACK, LANE), 0) & (CHUNK - 1)).astype(F32)
    eye_w = jnp.where(wr == wc, 1.0, 0.0)

    def rows(x, h):
        return x[h * CHUNK:(h + 1) * CHUNK]

    def by_head(parts):
        out = parts[N_HEADS - 1]
        for h in range(N_HEADS - 2, -1, -1):
            out = jnp.where(whead == h, parts[h], out)
        return out

    def block_diag(w):
        return tuple(jnp.where(same_head, jnp.concatenate([x] * N_HEADS, axis=0), jnp.zeros((STACK, STACK), BF16))
                     for x in _split2(w))

    def wide_times(a, bh, bl):
        ah, al = _split2(a)
        hh = _dot(jnp.concatenate([ah, al], axis=0), bh)
        return hh[:CHUNK] + hh[CHUNK:] + _dot(ah, bl)

    ret_const = []
    for d in range(2):
        lg4 = jnp.concatenate(
            [-_softplus(-jnp.full((CHUNK, STACK), logit_ref[d * N_HEADS + h], F32)) for h in range(N_HEADS)], axis=0)
        tri4 = (r4 >= c4) if d == 0 else (r4 <= c4)
        dist4 = ((r4 - c4) if d == 0 else (c4 - r4)).astype(F32)
        dec4 = jnp.exp(jnp.where(same_head, jnp.where(tri4, lg4 * dist4, -jnp.inf), -jnp.inf))
        spos = rpos if d == 0 else (CHUNK - 1.0) - rpos
        lgc = lg4[:, :LANE]
        ret_const.append((dec4, jnp.exp(lgc * (spos + 1.0)), jnp.exp(lgc * ((CHUNK - 1.0) - spos))))

    setup = []
    for n in range(chunks):
        tok = slice(n * CHUNK, (n + 1) * CHUNK)

        def stack(ref):
            return jnp.concatenate([ref[tok, h * LANE:(h + 1) * LANE] for h in range(N_HEADS)], axis=0)

        q4, k4, v4 = stack(rq_ref), stack(rk_ref), stack(rv_ref)
        q4b, k4b, v4b = q4.astype(BF16), k4.astype(BF16), v4.astype(BF16)
        rvb_ref[n] = v4b
        qk_ret = _dot_nt(q4b, k4b)
        gq4, gk4, gv4 = stack(cq_ref), stack(ck_ref), stack(cv_ref)
        gq4b, gk4b = gq4.astype(BF16), gk4.astype(BF16)
        gb = gb_ref[tok, :]
        gt = gt_ref[n]
        gh, gl = _split2(gb)
        gll = (gb - gh.astype(F32) - gl.astype(F32)).astype(BF16)
        th, tl = _split2(gt)
        tll = (gt - th.astype(F32) - tl.astype(F32)).astype(BF16)

        for d in range(2):
            dec4, qdec, kdec = ret_const[d]
            rin_ref[n, d] = _dot((qk_ret * dec4).astype(BF16), v4b)
            rqd_ref[n, d] = (q4 * qdec).astype(BF16)
            rkd_ref[n, d] = (k4 * kdec).astype(BF16)

        for d in range(2):
            fwd = d == 0
            low = jnp.where((rc >= cc) if fwd else (rc <= cc), 1.0, 0.0).astype(BF16)
            upp_w = jnp.where((wr <= wc) if fwd else (wr >= wc), 1.0, 0.0).astype(BF16)
            tri_w = (wr >= wc) if fwd else (wr <= wc)
            tri_c = (rc >= cc) if fwd else (rc <= cc)
            gc_col = _dot(low, gh) + _dot(low, gl) + _dot(low, gll)
            gc_row = _dot(th, upp_w) + _dot(tl, upp_w) + _dot(tll, upp_w)
            last_row = CHUNK - 1 if fwd else 0
            idx = [d * N_HEADS + h for h in range(N_HEADS)]
            gcol = [gc_col[:, i:i + 1] for i in idx]
            glast = [gc_col[last_row:last_row + 1, i:i + 1] for i in idx]
            beta = [gb[:, 2 * N_HEADS + i:2 * N_HEADS + i + 1] for i in idx]
            col4 = lambda parts: jnp.concatenate([jnp.broadcast_to(p, (CHUNK, 1)) for p in parts], axis=0)
            gcol4, glast4, beta4 = col4(gcol), col4(glast), col4(beta)
            eg4 = jnp.exp(gcol4)
            kb4 = gk4 * beta4
            gcol_w = by_head([jnp.broadcast_to(g, (CHUNK, STACK)) for g in gcol])
            grow_w = by_head([jnp.broadcast_to(gc_row[i:i + 1, :], (CHUNK, STACK)) for i in idx])
            dmat_w = jnp.exp(jnp.where(tri_w, gcol_w - grow_w, -jnp.inf))
            kk = _dot_nt(kb4.astype(BF16), gk4b)
            kk_w = by_head([rows(kk, h) for h in range(N_HEADS)])
            p = jnp.where(wr == wc, 0.0, -(kk_w * dmat_w))
            setup.append((n, d, p, gcol, glast, gcol4, glast4, beta4, eg4, kb4, gc_row, idx, tri_c,
                          gq4, gk4, gv4, gq4b, gk4b))

    ps = [s[2] for s in setup]
    bds = [block_diag(p) for p in ps]
    tinvs = [eye_w + p for p in ps]
    for _ in range(5):
        ps = [wide_times(p, *bd) for p, bd in zip(ps, bds)]
        bds = [block_diag(p) for p in ps]
        tinvs = [t + wide_times(t, *bd) for t, bd in zip(tinvs, bds)]
    t_bds = [block_diag(t) for t in tinvs]
    for (n, d, _, gcol, glast, gcol4, glast4, beta4, eg4, kb4, gc_row, idx, tri_c,
         gq4, gk4, gv4, gq4b, gk4b), (th_bd, tl_bd) in zip(setup, t_bds):
        rh, rl = _split2(jnp.concatenate([gv4 * beta4, kb4 * eg4], axis=1))
        uw = _dot(th_bd, rh) + _dot(th_bd, rl) + _dot(tl_bd, rh)
        gu_ref[n, d] = uw[:, :GDN_DV]
        w4 = uw[:, GDN_DV:].astype(BF16)
        qd4 = (gq4 * eg4).astype(BF16)
        gkd_ref[n, d] = (gk4 * jnp.exp(glast4 - gcol4)).astype(BF16)
        for h in range(N_HEADS):
            gwq_ref[n, d, h] = jnp.concatenate([rows(w4, h), rows(qd4, h)], axis=0)
            dm = jnp.exp(jnp.where(tri_c, gcol[h] - gc_row[idx[h]:idx[h] + 1, :CHUNK], -jnp.inf))
            gat_ref[n, d, h] = (_dot_nt(rows(gq4b, h), rows(gk4b, h)) * dm).astype(BF16)
        gcd_ref[n, d] = jnp.concatenate(
            [jnp.broadcast_to(jnp.exp(g), (1, LANE)) for g in glast] + [jnp.zeros((SUBLANE - N_HEADS, LANE), F32)],
            axis=0)


def _scan_kernel(fblk, bblk, first, last, s0idx, soidx, logit_ref, *refs):
    del fblk, bblk, s0idx, soidx
    views = (refs[0:9], refs[9:18])
    s0r_ref, s0g_ref = refs[18], refs[19]
    o_refs = ((refs[20], refs[22]), (refs[21], refs[23]))
    sro_ref, sgo_ref = refs[24], refs[25]
    sr_ref, sg_ref = refs[26], refs[27]
    t = pl.program_id(0)

    @pl.when(first[t] == 1)
    def _():
        sr_ref[...] = s0r_ref[...]
        sg_ref[...] = s0g_ref[...]

    @pl.when(first[t] == 2)
    def _():
        sr_ref[...] = jnp.zeros(sr_ref.shape, F32)
        sg_ref[...] = jnp.zeros(sg_ref.shape, F32)

    chains = [(d, h) for d in range(2) for h in range(N_HEADS)]
    s_gdn = {c: sg_ref[c[0], c[1]] for c in chains}
    s_ret = {c: sr_ref[c[0], c[1]] for c in chains}
    ret_decay = {(d, h): jnp.exp(-_softplus(-jnp.full((1, LANE), logit_ref[d * N_HEADS + h], F32)) * float(CHUNK))
                 for d, h in chains}
    n_sub = views[0][0].shape[0]
    for sub in range(n_sub):
        at = (sub, n_sub - 1 - sub)

        def rows(d, k, h):
            return views[d][k][at[d], h * CHUNK:(h + 1) * CHUNK, :]

        ws = {(d, h): _dot(views[d][5][at[d], h], s_gdn[d, h].astype(BF16)) for d, h in chains}
        inter = {(d, h): _dot(rows(d, 1, h), s_ret[d, h].astype(BF16)) for d, h in chains}
        kv = {(d, h): _dot_tn(rows(d, 2, h), rows(d, 3, h)) for d, h in chains}
        v_new = {(d, h): (rows(d, 4, h) - ws[d, h][:CHUNK]).astype(BF16) for d, h in chains}
        intra = {(d, h): _dot(views[d][6][at[d], h], v_new[d, h]) for d, h in chains}
        kvn = {(d, h): _dot_tn(rows(d, 7, h), v_new[d, h]) for d, h in chains}
        for d, h in chains:
            tok = slice(at[d] * CHUNK, (at[d] + 1) * CHUNK)
            lanes = slice(h * LANE, (h + 1) * LANE)
            o_refs[d][0][tok, lanes] = rows(d, 0, h) + inter[d, h]
            o_refs[d][1][tok, lanes] = ws[d, h][CHUNK:] + intra[d, h]
            s_ret[d, h] = s_ret[d, h] * ret_decay[d, h] + kv[d, h]
            s_gdn[d, h] = s_gdn[d, h] * views[d][8][at[d], h:h + 1, :] + kvn[d, h]
    for d, h in chains:
        sr_ref[d, h] = s_ret[d, h]
        sg_ref[d, h] = s_gdn[d, h]

    @pl.when(last[t] == 1)
    def _():
        sro_ref[...] = sr_ref[...]
        sgo_ref[...] = sg_ref[...]


def _scan_tables(lay):
    fblk, bblk, first, last, s0idx, soidx = [], [], [], [], [], []
    blk = SCAN_CHUNKS * CHUNK
    assert lay.seq % blk == 0 and lay.dec_seq % blk == 0
    cp, cs = lay.seq // blk, lay.dec_seq // blk
    for b in range(lay.dec_batch):
        base = lay.n_p // blk + b * cs
        for s in range(cs):
            fblk.append(base + s); bblk.append(base + cs - 1 - s)
            first.append(1 if s == 0 else 0); last.append(0); s0idx.append(b); soidx.append(0)
    for b in range(lay.batch):
        base = b * cp
        for s in range(cp):
            fblk.append(base + s); bblk.append(base + cp - 1 - s)
            first.append(2 if s == 0 else 0); last.append(1 if s == cp - 1 else 0)
            s0idx.append(lay.dec_batch - 1); soidx.append(b)
    return [jnp.asarray(np.asarray(a, np.int32)) for a in (fblk, bblk, first, last, s0idx, soidx)]


def _scan_prep(lay, rq, rk, rv, cq, ck, cv, gb, gt, logit):
    n_chunks = lay.n // CHUNK
    nc = PREP_CHUNKS
    assert n_chunks % nc == 0
    width = N_HEADS * LANE

    def tok(wd):
        return pl.BlockSpec((nc * CHUNK, wd), lambda i: (i, 0))

    def per_dir(shape, dtype):
        spec = pl.BlockSpec((nc, 2) + shape, lambda i: (i,) + (0,) * (1 + len(shape)))
        return spec, jax.ShapeDtypeStruct((n_chunks, 2) + shape, dtype)

    outs = [
        per_dir((STACK, LANE), F32),
        per_dir((STACK, LANE), BF16),
        per_dir((STACK, LANE), BF16),
        (pl.BlockSpec((nc, STACK, LANE), lambda i: (i, 0, 0)), jax.ShapeDtypeStruct((n_chunks, STACK, LANE), BF16)),
        per_dir((STACK, LANE), F32),
        per_dir((N_HEADS, 2 * CHUNK, LANE), BF16),
        per_dir((N_HEADS, CHUNK, CHUNK), BF16),
        per_dir((STACK, LANE), BF16),
        per_dir((SUBLANE, LANE), F32),
    ]
    return pl.pallas_call(
        functools.partial(_prep_kernel, chunks=nc),
        grid=(n_chunks // nc,),
        in_specs=[pl.BlockSpec(memory_space=pltpu.SMEM)] + [tok(width)] * 6
                 + [tok(LANE), pl.BlockSpec((nc, 4 * GDN_HEADS, CHUNK), lambda i: (i, 0, 0))],
        out_specs=[o[0] for o in outs],
        out_shape=[o[1] for o in outs],
        compiler_params=_params("parallel"),
        name="scan_prep",
    )(logit.reshape(-1).astype(F32), rq, rk, rv, cq, ck, cv, gb, gt)


def _even_scan(lay, prep, logit, s0_ret, s0_gdn):
    tables = _scan_tables(lay)
    steps = int(tables[0].shape[0])
    width = N_HEADS * LANE

    def view(which):
        def blk(arr, with_dir=True):
            shape = arr.shape[2:] if with_dir else arr.shape[1:]
            if with_dir:
                return pl.BlockSpec((SCAN_CHUNKS, None) + shape,
                                    lambda t, *tb: (tb[which][t], which) + (0,) * len(shape))
            return pl.BlockSpec((SCAN_CHUNKS,) + shape, lambda t, *tb: (tb[which][t],) + (0,) * len(shape))
        return [blk(a, with_dir=(i != 3)) for i, a in enumerate(prep)]

    state_shape = (2, N_HEADS, LANE, LANE)
    s0_spec = pl.BlockSpec((None,) + state_shape, lambda t, *tb: (tb[4][t], 0, 0, 0, 0))
    so_spec = pl.BlockSpec((None,) + state_shape, lambda t, *tb: (tb[5][t], 0, 0, 0, 0))
    out_f = pl.BlockSpec((SCAN_CHUNKS * CHUNK, width), lambda t, *tb: (tb[0][t], 0))
    out_b = pl.BlockSpec((SCAN_CHUNKS * CHUNK, width), lambda t, *tb: (tb[1][t], 0))
    o_sds = jax.ShapeDtypeStruct((lay.n, width), F32)
    so_sds = jax.ShapeDtypeStruct((lay.batch,) + state_shape, F32)
    args = list(prep)
    return pl.pallas_call(
        _scan_kernel,
        grid_spec=pltpu.PrefetchScalarGridSpec(
            num_scalar_prefetch=6,
            grid=(steps,),
            in_specs=[pl.BlockSpec(memory_space=pltpu.SMEM)] + view(0) + view(1) + [s0_spec, s0_spec],
            out_specs=[out_f, out_b, out_f, out_b, so_spec, so_spec],
            scratch_shapes=[pltpu.VMEM(state_shape, F32), pltpu.VMEM(state_shape, F32)],
        ),
        out_shape=[o_sds, o_sds, o_sds, o_sds, so_sds, so_sds],
        compiler_params=_params("arbitrary"),
        name="even_scan",
    )(*tables, logit.reshape(-1).astype(F32), *args, *args, s0_ret, s0_gdn)


def _odd_proj_kernel(x_ref, mod_ref, nw_ref, w_ref, qnw_ref, knw_ref, cos_ref, sin_ref,
                     q_ref, kd_ref, vt_ref, ks_ref, vs_ref, *, p_tiles):
    h = _adaln(x_ref[...], nw_ref[...], mod_ref[3:4, :], mod_ref[4:5, :]).astype(BF16)
    tm = h.shape[0]
    lane = lax.broadcasted_iota(jnp.int32, (tm, LANE), 1)
    lo = lane < ATT_HD
    first16 = (lane & (2 * ROPE_PAIRS - 1)) < ROPE_PAIRS
    cos, sin = cos_ref[...], sin_ref[...]

    def norm_rope(xs, w2):
        sq = xs * xs
        ms_lo = jnp.sum(jnp.where(lo, sq, 0.0), axis=-1, keepdims=True) * (1.0 / ATT_HD)
        ms_hi = jnp.sum(jnp.where(lo, 0.0, sq), axis=-1, keepdims=True) * (1.0 / ATT_HD)
        xn = xs * jnp.where(lo, lax.rsqrt(ms_lo + EPS), lax.rsqrt(ms_hi + EPS)) * w2
        partner = jnp.where(first16, pltpu.roll(xn, LANE - ROPE_PAIRS, 1), pltpu.roll(xn, ROPE_PAIRS, 1))
        return xn * cos + partner * sin

    def both_halves(xs, j):
        swapped = pltpu.roll(xs, ATT_HD, 1)
        return jnp.where(lo, xs, swapped) if j % 2 == 0 else jnp.where(lo, swapped, xs)

    for t in range(ATT_Q // MXU_TILE):
        y = _dot(h, w_ref[:, t * MXU_TILE:(t + 1) * MXU_TILE])
        for p in (2 * t, 2 * t + 1):
            qs = norm_rope(y[:, (p % 2) * LANE:(p % 2 + 1) * LANE], qnw_ref[...])
            q_ref[:, p * LANE:(p + 1) * LANE] = (qs * Q_SCALE).astype(BF16)
    is_prompt = pl.program_id(0) < p_tiles
    assert ATT_KV == MXU_TILE
    yk = _dot(h, w_ref[:, ATT_Q:ATT_Q + ATT_KV])
    yv = _dot(h, w_ref[:, ATT_Q + ATT_KV:ATT_Q + 2 * ATT_KV])
    for p in range(ATT_KV // LANE):
        ks = norm_rope(yk[:, p * LANE:(p + 1) * LANE], knw_ref[...])
        vs = yv[:, p * LANE:(p + 1) * LANE]
        vst = vs.T.astype(BF16)
        for j in (2 * p, 2 * p + 1):
            kd_ref[j] = both_halves(ks, j).astype(BF16)
            for c in range(tm // KEY_CHUNK):
                vt_ref[j, c] = vst[(j % 2) * ATT_HD:(j % 2 + 1) * ATT_HD, c * KEY_CHUNK:(c + 1) * KEY_CHUNK]

        @pl.when(is_prompt)
        def _():
            ks_ref[:, p * LANE:(p + 1) * LANE] = ks
            vs_ref[:, p * LANE:(p + 1) * LANE] = vs


def _rope_tables(lay):
    t = jnp.arange(lay.dec_seq)
    inv = ROPE_THETA ** (-jnp.arange(ROPE_PAIRS, dtype=F32) / ROPE_PAIRS)
    ar = (t // GRID_W).astype(F32)[:, None] * inv
    ac = (t % GRID_W).astype(F32)[:, None] * inv
    cos = jnp.concatenate([jnp.cos(ar), jnp.cos(ar), jnp.cos(ac), jnp.cos(ac)], axis=1)
    sin = jnp.concatenate([-jnp.sin(ar), jnp.sin(ar), -jnp.sin(ac), jnp.sin(ac)], axis=1)
    cos = jnp.concatenate([jnp.tile(cos, (1, 2)), jnp.ones((lay.tm, LANE), F32)], axis=0)
    sin = jnp.concatenate([jnp.tile(sin, (1, 2)), jnp.zeros((lay.tm, LANE), F32)], axis=0)
    return cos, sin


def _odd_proj(lay, x, mods, nw, w, qnw, knw):
    cos, sin = _rope_tables(lay)

    def table_block(i):
        return jnp.where(i < lay.p_tiles, lay.tiles_per_seq, lax.rem(jnp.maximum(i - lay.p_tiles, 0), lay.tiles_per_seq))

    tab = pl.BlockSpec((lay.tm, LANE), lambda i: (table_block(i), 0))
    std = pl.BlockSpec((lay.tm, ATT_KV), lambda i: (jnp.minimum(i, lay.p_tiles - 1), 0))
    tm = lay.tm
    return pl.pallas_call(
        functools.partial(_odd_proj_kernel, p_tiles=lay.p_tiles),
        grid=(lay.tiles,),
        in_specs=[lay.tok(D_MODEL), lay.mod(), _resident((1, D_MODEL)), _resident(w.shape),
                  _resident((1, LANE)), _resident((1, LANE)), tab, tab],
        out_specs=[
            lay.tok(ATT_Q),
            pl.BlockSpec((ATT_KV_HEADS, tm, LANE), lambda i: (0, i, 0)),
            pl.BlockSpec((ATT_KV_HEADS, tm // KEY_CHUNK, ATT_HD, KEY_CHUNK), lambda i: (0, i, 0, 0)),
            std, std,
        ],
        out_shape=[
            jax.ShapeDtypeStruct((lay.n, ATT_Q), BF16),
            jax.ShapeDtypeStruct((ATT_KV_HEADS, lay.n, LANE), BF16),
            jax.ShapeDtypeStruct((ATT_KV_HEADS, lay.n // KEY_CHUNK, ATT_HD, KEY_CHUNK), BF16),
            jax.ShapeDtypeStruct((lay.n_p, ATT_KV), F32),
            jax.ShapeDtypeStruct((lay.n_p, ATT_KV), F32),
        ],
        compiler_params=_params("arbitrary"),
        name="odd_proj",
    )(x, mods, nw.reshape(1, D_MODEL), w, jnp.tile(qnw.reshape(1, ATT_HD), (1, 2)),
      jnp.tile(knw.reshape(1, ATT_HD), (1, 2)), cos, sin)


def _attn_kernel(*refs, cached):
    if cached:
        zero_ref, q_ref, kn_ref, vn_ref, kc_ref, vc_ref, _, o_ref, *s_scr = refs
    else:
        zero_ref, q_ref, kn_ref, vn_ref, o_ref, *s_scr = refs
    z = zero_ref[0]
    tq = MXU_TILE
    n_sub = q_ref.shape[0] // tq
    n_kv = kn_ref.shape[0]
    gw = ATT_GROUP * ATT_HD
    ck = KEY_CHUNK
    n_c = kc_ref.shape[1] // ck if cached else 0
    n_n = kn_ref.shape[1] // ck
    lo = lax.broadcasted_iota(jnp.int32, (tq, LANE), 1) < ATT_HD
    qms, kv_of = [], []
    for s in range(n_sub):
        for j in range(n_kv):
            for g in range(ATT_GROUP):
                q2 = q_ref[s * tq:(s + 1) * tq, j * gw + (g // 2) * LANE:j * gw + (g // 2 + 1) * LANE]
                qms.append(jnp.where(lo, q2, jnp.zeros_like(q2)) if g % 2 == 0
                           else jnp.where(lo, jnp.zeros_like(q2), q2))
                kv_of.append(j)
    n_units = len(qms)

    def keys(j, c):
        return kc_ref[j, c * ck:(c + 1) * ck, :] if c < n_c else kn_ref[j, (c - n_c) * ck:(c - n_c + 1) * ck, :]

    def values_t(j, c):
        return vc_ref[j, c] if c < n_c else vn_ref[j, c - n_c]

    ones_rows = jnp.where(lax.broadcasted_iota(jnp.int32, (2 * SUBLANE, ck), 0) == 0, 1.0, 0.0).astype(BF16)
    m_fin = [None] * n_units
    outs = [None] * n_units
    for ph in range(n_units + 1):
        ga = ph if ph < n_units else None
        gb = ph - 1 if ph >= 1 else None
        ma = jnp.full((1, tq), -jnp.inf, F32)
        acc = jnp.zeros((ATT_HD + 2 * SUBLANE, tq), F32)
        for c in range(n_c + n_n):
            if ga is not None:
                st = _dot_nt(keys(kv_of[ga], c), qms[ga])
                s_scr[ga % 2][z + c] = st
                ma = jnp.maximum(ma, jnp.max(st, axis=0, keepdims=True))
            if gb is not None:
                p = jnp.exp2(s_scr[gb % 2][z + c] - m_fin[gb]).astype(BF16)
                acc = acc + _dot(jnp.concatenate([values_t(kv_of[gb], c), ones_rows], axis=0), p)
        if ga is not None:
            m_fin[ga] = ma
        if gb is not None:
            outs[gb] = acc[:ATT_HD] / acc[ATT_HD:ATT_HD + 1]
    per_sub = n_kv * ATT_GROUP
    for s in range(n_sub):
        o_ref[s * tq:(s + 1) * tq, :] = jnp.concatenate(outs[s * per_sub:(s + 1) * per_sub], axis=0).T.astype(BF16)


def _attention(lay, q, kd, vt, kc, vc):
    gw = ATT_GROUP * ATT_HD
    o_sds = jax.ShapeDtypeStruct((lay.n, ATT_Q), BF16)
    ck = KEY_CHUNK
    assert lay.seq % ck == 0 and lay.dec_seq % ck == 0 and lay.past % ck == 0
    assert lay.seq % MXU_TILE == 0 and lay.dec_seq % MXU_TILE == 0 and ATT_Q_TILE % MXU_TILE == 0

    zero = jnp.zeros((1,), jnp.int32)
    zero_spec = pl.BlockSpec(memory_space=pltpu.SMEM)

    tq = min(ATT_Q_TILE, lay.seq)
    nq = lay.seq // tq
    nck = lay.seq // ck
    out = pl.pallas_call(
        functools.partial(_attn_kernel, cached=False),
        grid=(lay.batch, nq),
        in_specs=[
            zero_spec,
            pl.BlockSpec((tq, ATT_Q), lambda b, i: (b * nq + i, 0)),
            pl.BlockSpec((ATT_KV_HEADS, lay.seq, LANE), lambda b, i: (0, b, 0)),
            pl.BlockSpec((ATT_KV_HEADS, nck, ATT_HD, ck), lambda b, i: (0, b, 0, 0)),
        ],
        out_specs=pl.BlockSpec((tq, ATT_Q), lambda b, i: (b * nq + i, 0)),
        out_shape=o_sds,
        scratch_shapes=[pltpu.VMEM((nck, ck, MXU_TILE), F32)] * 2,
        compiler_params=_params("parallel", "parallel"),
        name="attn_context",
    )(zero, q, kd, vt)

    tq = min(ATT_Q_TILE, lay.dec_seq)
    nq = lay.dec_seq // tq
    nck = lay.dec_seq // ck
    qbase = lay.n_p // tq
    kbase = lay.n_p // lay.dec_seq
    assert lay.n_p % lay.dec_seq == 0
    return pl.pallas_call(
        functools.partial(_attn_kernel, cached=True),
        grid=(lay.dec_batch, ATT_KV_HEADS, nq),
        in_specs=[
            zero_spec,
            pl.BlockSpec((tq, gw), lambda b, j, i: (qbase + b * nq + i, j)),
            pl.BlockSpec((1, lay.dec_seq, LANE), lambda b, j, i: (j, kbase + b, 0)),
            pl.BlockSpec((1, nck, ATT_HD, ck), lambda b, j, i: (j, kbase + b, 0, 0)),
            pl.BlockSpec((None, 1, lay.past, LANE), lambda b, j, i: (b, j, 0, 0)),
            pl.BlockSpec((None, 1, lay.past // ck, ATT_HD, ck), lambda b, j, i: (b, j, 0, 0, 0)),
            pl.BlockSpec(memory_space=pl.ANY),
        ],
        out_specs=pl.BlockSpec((tq, gw), lambda b, j, i: (qbase + b * nq + i, j)),
        out_shape=o_sds,
        scratch_shapes=[pltpu.VMEM((lay.past // ck + nck, ck, MXU_TILE), F32)] * 2,
        input_output_aliases={6: 0},
        compiler_params=_params("parallel", "parallel", "parallel"),
        name="attn_latent",
    )(zero, q, kd, vt, kc, vc, out)


def _pad_state(s):
    return jnp.pad(s, [(0, 0)] * (s.ndim - 2) + [(0, LANE - s.shape[-2]), (0, 0)])


def _cache_keys(c):
    return jnp.tile(c.transpose(0, 2, 1, 3), (1, 1, 1, 2)).astype(BF16)


def _cache_values(c):
    b, t, kvh, hd = c.shape
    return c.reshape(b, t // KEY_CHUNK, KEY_CHUNK, kvh, hd).transpose(0, 3, 1, 4, 2).astype(BF16)


def kernel(x_prompt, x_sample, state_ret, state_gdn, cache_k, cache_v, c, c_ctx,
           mod_w, mod_b, norm_w, ffn_w_in, ffn_w_out, even_w_in, even_w_out,
           ret_decay_logit, ret_norm_w, gdn_conv_w, gdn_A_log, gdn_dt_bias, gdn_norm_w,
           odd_w_in, odd_w_out, q_norm_w, k_norm_w, final_norm_w):
    batch, seq, d = x_prompt.shape
    dec_batch, dec_seq, _ = x_sample.shape
    depth = mod_w.shape[0]
    lay = _Layout(batch, seq, dec_batch, dec_seq, cache_k.shape[2])
    lay_ffn = _Layout(batch, seq, dec_batch, dec_seq, cache_k.shape[2], tile=FFN_TILE)

    x = (x_prompt.reshape(lay.n_p, d), x_sample.reshape(lay.n_s, d))
    n_cond = -(-(1 + dec_batch) // (2 * SUBLANE)) * (2 * SUBLANE)
    cond = jnp.zeros((n_cond, d), F32).at[0].set(c_ctx).at[1:1 + dec_batch].set(c)
    mods = _modulation(cond, mod_w, mod_b)
    ffn_in, ffn_out = ffn_w_in.astype(BF16), ffn_w_out.astype(BF16)

    new_ret, new_gdn, new_k, new_v = [], [], [], []
    for l in range(depth):
        m = mods[l]
        last = l == depth - 1
        x = _ffn(lay_ffn, x, m, norm_w[l, 0], ffn_in, ffn_out, (l, 0), 0)
        if l % 2 == 0:
            e = l // 2
            rq, rk, rv, rg, gg, cq, ck, cv, gb, gt = _even_proj(
                lay, x, m, norm_w[l, 1], _even_weight(even_w_in[e]), gdn_conv_w[e], gdn_A_log[e], gdn_dt_bias[e])
            prep = _scan_prep(lay, rq, rk, rv, cq, ck, cv, gb, gt, ret_decay_logit[e])
            orf, orb, ogf, ogb, sr, sg = _even_scan(
                lay, prep, ret_decay_logit[e], _pad_state(state_ret[:, e]), state_gdn[:, e])
            new_ret.append(sr[:, :, :, :RET_DK, :])
            new_gdn.append(sg)
            mixer = "even"
            mixer_ins = (orf, orb, ogf, ogb, rg, gg, ret_norm_w[e].reshape(1, LANE), gdn_norm_w[e].reshape(1, LANE),
                         even_w_out[e].astype(BF16))
        else:
            o = l // 2
            q, kd, vt, ks, vs = _odd_proj(lay, x, m, norm_w[l, 1], odd_w_in[o].astype(BF16), q_norm_w[o], k_norm_w[o])
            a = _attention(lay, q, kd, vt, _cache_keys(cache_k[:, o]), _cache_values(cache_v[:, o]))
            new_k.append(ks.reshape(batch, seq, ATT_KV_HEADS, ATT_HD))
            new_v.append(vs.reshape(batch, seq, ATT_KV_HEADS, ATT_HD))
            mixer = "odd"
            mixer_ins = (a, odd_w_out[o].astype(BF16))
        x = _ffn(lay, x, m, norm_w[l, 2], ffn_in, ffn_out, (l, 1), 2,
                 final_w=final_norm_w if last else None, mixer=mixer, mixer_ins=mixer_ins)

    y_prompt = x[0].reshape(batch, seq, d)
    y_sample = x[1].reshape(dec_batch, dec_seq, d)
    return (y_prompt, y_sample, jnp.stack(new_ret, axis=1), jnp.stack(new_gdn, axis=1),
            jnp.stack(new_k, axis=1), jnp.stack(new_v, axis=1))
```

```python
import functools

import numpy as np
import jax
import jax.numpy as jnp
from jax import lax
from jax.experimental import pallas as pl
from jax.experimental.pallas import tpu as pltpu

F32 = jnp.float32
BF16 = jnp.bfloat16

D_MODEL = 1024
GRID_W = 64
RET_HEADS = 4
RET_DK = 64
RET_DV = 128
GDN_HEADS = 4
GDN_DK = 128
GDN_DV = 128
CHUNK = 64
ATT_HEADS = 16
ATT_KV_HEADS = 4
ATT_HD = 64
ATT_GROUP = ATT_HEADS // ATT_KV_HEADS
ROPE_THETA = 10000.0
ROPE_PAIRS = ATT_HD // 4
FFN_HIDDEN = 2816
N_MOD = 9
EPS = 1e-6

RET_QK = RET_HEADS * RET_DK
RET_V = RET_HEADS * RET_DV
GDN_QK = GDN_HEADS * GDN_DK
GDN_V = GDN_HEADS * GDN_DV
ATT_Q = ATT_HEADS * ATT_HD
ATT_KV = ATT_KV_HEADS * ATT_HD

LANE = 128
SUBLANE = 8
V7X_VMEM_BYTES = 64 * 1024 * 1024
VMEM_LIMIT = V7X_VMEM_BYTES - 8 * 1024 * 1024

TOKEN_TILE = 512
FFN_TILE = 1024
ATT_Q_TILE = 1024
KEY_CHUNK = 256
Q_SCALE = ATT_HD ** -0.5 * float(np.log2(np.e))
MXU_TILE = 256
FFN_CHUNKS = ((0, 6 * MXU_TILE), (6 * MXU_TILE, FFN_HIDDEN))
HEAD_PAD = LANE
PREP_CHUNKS = 4
SCAN_CHUNKS = 4


def _params(*sem):
    return pltpu.CompilerParams(dimension_semantics=sem, vmem_limit_bytes=VMEM_LIMIT)


def _resident(shape):
    nd = len(shape)
    return pl.BlockSpec(shape, lambda *_: (0,) * nd, pipeline_mode=pl.Buffered(1))


def _resident_at(arr, lead):
    shape = arr.shape[len(lead):]
    return pl.BlockSpec((None,) * len(lead) + shape, lambda *_: tuple(lead) + (0,) * len(shape),
                        pipeline_mode=pl.Buffered(1))


def _dot(a, b):
    return jnp.dot(a, b, preferred_element_type=F32)


def _dot_nt(a, b):
    return lax.dot_general(a, b, (((1,), (1,)), ((), ())), preferred_element_type=F32)


def _dot_tn(a, b):
    return lax.dot_general(a, b, (((0,), (0,)), ((), ())), preferred_element_type=F32)


def _split2(a):
    hi = a.astype(BF16)
    lo = (a - hi.astype(F32)).astype(BF16)
    return hi, lo


def _silu(x):
    return x * jax.nn.sigmoid(x)


def _softplus(x):
    return jnp.maximum(x, 0.0) + jnp.log(1.0 + jnp.exp(-jnp.abs(x)))


def _adaln(x, nw, shift, scale):
    ms = jnp.mean(x * x, axis=-1, keepdims=True)
    return (x * lax.rsqrt(ms + EPS)) * nw * (1.0 + scale) + shift


def _head_rmsnorm(x, w, n_heads):
    parts = []
    for h in range(n_heads):
        xs = x[:, h * LANE:(h + 1) * LANE]
        ms = jnp.mean(xs * xs, axis=-1, keepdims=True)
        parts.append(xs * lax.rsqrt(ms + EPS) * w)
    return jnp.concatenate(parts, axis=1)


def _mod_kernel(cond_ref, w_ref, b_ref, o_ref):
    c = cond_ref[...]
    o_ref[0] = _dot(_silu(c).astype(BF16), w_ref[0].astype(BF16)) + b_ref[0]


def _modulation(cond, mod_w, mod_b):
    depth, d, n = mod_w.shape
    r = cond.shape[0]
    tn = n // 8
    out = pl.pallas_call(
        _mod_kernel,
        grid=(depth, n // tn),
        in_specs=[
            pl.BlockSpec((r, d), lambda l, j: (0, 0)),
            pl.BlockSpec((1, d, tn), lambda l, j: (l, 0, j)),
            pl.BlockSpec((1, 1, tn), lambda l, j: (l, 0, j)),
        ],
        out_specs=pl.BlockSpec((1, r, tn), lambda l, j: (l, 0, j)),
        out_shape=jax.ShapeDtypeStruct((depth, r, n), F32),
        compiler_params=_params("parallel", "parallel"),
        name="modulation",
    )(cond, mod_w, mod_b.reshape(depth, 1, n))
    return out.reshape(depth, r, N_MOD, d)


class _Layout:
    def __init__(self, batch, seq, dec_batch, dec_seq, past, tile=TOKEN_TILE):
        self.batch, self.seq, self.dec_batch, self.dec_seq, self.past = batch, seq, dec_batch, dec_seq, past
        self.n_p = batch * seq
        self.n_s = dec_batch * dec_seq
        self.n = self.n_p + self.n_s
        self.tm = min(tile, self.n_p, dec_seq)
        assert self.n_p % self.tm == 0 and dec_seq % self.tm == 0
        self.tiles = self.n // self.tm
        self.p_tiles = self.n_p // self.tm
        self.tiles_per_seq = dec_seq // self.tm

    def group(self, i):
        return jnp.where(i < self.p_tiles, 0, 1 + (i - self.p_tiles) // self.tiles_per_seq)

    def tok(self, width, tm=None):
        tm = tm or self.tm
        return pl.BlockSpec((tm, width), lambda i: (i, 0))

    def mod(self):
        return pl.BlockSpec((None, N_MOD, D_MODEL), lambda i: (self.group(i), 0, 0))


def _ffn_kernel(*refs, j, first, final, p_tiles, mixer):
    refs = list(refs)
    x_refs = [refs.pop(0) for _ in range(2 if first else 1)]
    n_mix = {None: 0, "odd": 2, "even": 9}[mixer]
    mix_refs = [refs.pop(0) for _ in range(n_mix)]
    mod_ref, nw_ref, win_ref, wout_ref = (refs.pop(0) for _ in range(4))
    fw_ref = refs.pop(0) if final else None
    o_refs = refs
    is_prompt = pl.program_id(0) < p_tiles
    x = jnp.where(is_prompt, x_refs[0][...], x_refs[1][...]) if first else x_refs[0][...]
    if mixer == "odd":
        a_ref, wo_ref = mix_refs
        x = x + mod_ref[5:6, :] * _dot(a_ref[...], wo_ref[...])
    elif mixer == "even":
        rf_ref, rb_ref, gf_ref, gb_ref, rg_ref, gg_ref, rnw_ref, gnw_ref, wo_ref = mix_refs
        ret = _head_rmsnorm(rf_ref[...] + rb_ref[...], rnw_ref[...], RET_HEADS) * _silu(rg_ref[...])
        gdn = _head_rmsnorm(gf_ref[...] + gb_ref[...], gnw_ref[...], GDN_HEADS) * _silu(gg_ref[...])
        x = x + mod_ref[5:6, :] * _dot(jnp.concatenate([ret, gdn], axis=1).astype(BF16), wo_ref[...])
    h = _adaln(x, nw_ref[...], mod_ref[3 * j:3 * j + 1, :], mod_ref[3 * j + 1:3 * j + 2, :]).astype(BF16)
    acc = None
    for lo, hi in FFN_CHUNKS:
        a = _dot(h, win_ref[:, lo:hi])
        b = _dot(h, win_ref[:, FFN_HIDDEN + lo:FFN_HIDDEN + hi])
        y = _dot((_silu(a) * b).astype(BF16), wout_ref[lo:hi, :])
        acc = y if acc is None else acc + y
    out = x + (0.5 * mod_ref[3 * j + 2:3 * j + 3, :]) * acc
    if not final:
        o_refs[0][...] = out
        return
    ms = jnp.mean(out * out, axis=-1, keepdims=True)
    out = out * lax.rsqrt(ms + EPS) * fw_ref[...]

    @pl.when(is_prompt)
    def _():
        o_refs[0][...] = out

    @pl.when(jnp.logical_not(is_prompt))
    def _():
        o_refs[1][...] = out


def _ffn(lay, xs, mods, nw, w_in, w_out, which, j, final_w=None, mixer=None, mixer_ins=()):
    first = isinstance(xs, tuple)
    final = final_w is not None
    prompt_rows = pl.BlockSpec((lay.tm, D_MODEL), lambda i: (jnp.minimum(i, lay.p_tiles - 1), 0))
    latent_rows = pl.BlockSpec((lay.tm, D_MODEL), lambda i: (jnp.maximum(i - lay.p_tiles, 0), 0))
    ins = list(xs) if first else [xs]
    specs = [prompt_rows, latent_rows] if first else [lay.tok(D_MODEL)]
    for arr in mixer_ins:
        ins.append(arr)
        specs.append(lay.tok(arr.shape[1]) if arr.shape[0] == lay.n else _resident(arr.shape))
    ins += [mods, nw.reshape(1, D_MODEL), w_in, w_out]
    specs += [lay.mod(), _resident((1, D_MODEL)), _resident_at(w_in, which), _resident_at(w_out, which)]
    if final:
        ins.append(final_w.reshape(1, D_MODEL))
        specs.append(_resident((1, D_MODEL)))
        out_specs = [prompt_rows, latent_rows]
        out_shape = [jax.ShapeDtypeStruct((lay.n_p, D_MODEL), F32), jax.ShapeDtypeStruct((lay.n_s, D_MODEL), F32)]
    else:
        out_specs = lay.tok(D_MODEL)
        out_shape = jax.ShapeDtypeStruct((lay.n, D_MODEL), F32)
    return pl.pallas_call(
        functools.partial(_ffn_kernel, j=j, first=first, final=final, p_tiles=lay.p_tiles, mixer=mixer),
        grid=(lay.tiles,),
        in_specs=specs,
        out_specs=out_specs,
        out_shape=out_shape,
        compiler_params=_params("arbitrary"),
        name=("ffn_final" if final else ("ffn_first" if first else "ffn")) + ("_" + mixer if mixer else ""),
    )(*ins)


EVEN_WIDTHS = (RET_HEADS * HEAD_PAD, RET_HEADS * HEAD_PAD, RET_V, RET_V, 2 * GDN_QK + GDN_V, GDN_V, LANE)


def _even_proj_kernel(x_ref, prev_ref, next_ref, mod_ref, nw_ref, w_ref, cw_ref, alog_ref, dt_ref,
                      rq_ref, rk_ref, rv_ref, rg_ref, gg_ref, cq_ref, ck_ref, cv_ref, gb_ref, gt_ref,
                      *, n_p, seq, dec_seq):
    tm = x_ref.shape[0]
    shift, scale = mod_ref[3:4, :], mod_ref[4:5, :]
    h = _adaln(x_ref[...], nw_ref[...], shift, scale).astype(BF16)
    halo = jnp.concatenate([prev_ref[...], next_ref[...]], axis=0)
    hh = _adaln(halo, nw_ref[...], shift, scale).astype(BF16)
    offs = np.cumsum((0,) + EVEN_WIDTHS)
    col = lambda i: slice(int(offs[i]), int(offs[i + 1]))

    start = pl.program_id(0) * tm
    in_prompt = start < n_p
    rel = jnp.where(in_prompt, start, start - n_p)
    slen = jnp.where(in_prompt, seq, dec_seq)
    row = lax.broadcasted_iota(jnp.int32, (tm, GDN_QK), 0)
    pos = (rel + row) & (slen - 1)

    def conv(part):
        lanes = slice(int(offs[4]) + part * GDN_QK, int(offs[4]) + (part + 1) * GDN_QK)
        cw = cw_ref[:, part * GDN_QK:(part + 1) * GDN_QK]
        y = _dot(h, w_ref[:, lanes])
        yh = _dot(hh, w_ref[:, lanes])
        ym1 = jnp.where(pos == 0, 0.0, jnp.where(row == 0, yh[SUBLANE - 1:SUBLANE, :], pltpu.roll(y, 1, 0)))
        yp1 = jnp.where(pos == slen - 1, 0.0,
                        jnp.where(row == tm - 1, yh[SUBLANE:SUBLANE + 1, :], pltpu.roll(y, tm - 1, 0)))
        return _silu(ym1 * cw[0:1, :] + y * cw[1:2, :] + yp1 * cw[2:3, :])

    def l2norm_heads(y, o_ref, scale):
        for hd in range(GDN_HEADS):
            v = y[:, hd * LANE:(hd + 1) * LANE]
            o_ref[:, hd * LANE:(hd + 1) * LANE] = v * lax.rsqrt(jnp.sum(v * v, axis=-1, keepdims=True) + EPS) * scale

    l2norm_heads(conv(0), cq_ref, GDN_DK ** -0.5)
    rq_ref[...] = _dot(h, w_ref[:, col(0)])
    rk_ref[...] = _dot(h, w_ref[:, col(1)])
    l2norm_heads(conv(1), ck_ref, 1.0)
    rv_ref[...] = _dot(h, w_ref[:, col(2)])
    rg_ref[...] = _dot(h, w_ref[:, col(3)])
    cv_ref[...] = conv(2)
    gg_ref[...] = _dot(h, w_ref[:, col(5)])
    g = _dot(h, w_ref[:, col(6)])
    lane = lax.broadcasted_iota(jnp.int32, g.shape, 1)
    gb = jnp.where(lane < 2 * GDN_HEADS, -jnp.exp(alog_ref[...]) * _softplus(g + dt_ref[...]), jax.nn.sigmoid(g))
    gb_ref[...] = gb
    gbt = gb.T
    for c in range(tm // CHUNK):
        gt_ref[c] = gbt[0:4 * GDN_HEADS, c * CHUNK:(c + 1) * CHUNK]


def _even_proj(lay, x, mods, nw, w, conv_w, a_log, dt_bias):
    tm = lay.tm
    assert lay.seq & (lay.seq - 1) == 0 and lay.dec_seq & (lay.dec_seq - 1) == 0 and tm % LANE == 0
    nblk8 = lay.n // SUBLANE
    r = tm // SUBLANE
    width = GDN_V
    assert all(wd == width for wd in EVEN_WIDTHS[:4] + EVEN_WIDTHS[5:6]) and GDN_QK == width

    def pad_row(v):
        v = v.reshape(1, -1).astype(F32)
        return jnp.pad(v, ((0, 0), (0, LANE - v.shape[1])))

    tok_out = [(width, F32)] * 8 + [(LANE, F32)]
    return pl.pallas_call(
        functools.partial(_even_proj_kernel, n_p=lay.n_p, seq=lay.seq, dec_seq=lay.dec_seq),
        grid=(lay.tiles,),
        in_specs=[
            lay.tok(D_MODEL),
            pl.BlockSpec((SUBLANE, D_MODEL), lambda i: (jnp.maximum(i * r - 1, 0), 0)),
            pl.BlockSpec((SUBLANE, D_MODEL), lambda i: (jnp.minimum((i + 1) * r, nblk8 - 1), 0)),
            lay.mod(), _resident((1, D_MODEL)), _resident(w.shape), _resident(conv_w.shape),
            _resident((1, LANE)), _resident((1, LANE)),
        ],
        out_specs=[lay.tok(wd) for wd, _ in tok_out]
                  + [pl.BlockSpec((tm // CHUNK, 4 * GDN_HEADS, CHUNK), lambda i: (i, 0, 0))],
        out_shape=[jax.ShapeDtypeStruct((lay.n, wd), dt) for wd, dt in tok_out]
                  + [jax.ShapeDtypeStruct((lay.n // CHUNK, 4 * GDN_HEADS, CHUNK), F32)],
        compiler_params=_params("parallel"),
        name="even_proj",
    )(x, x, x, mods, nw.reshape(1, D_MODEL), w, conv_w, pad_row(a_log), pad_row(dt_bias))


def _even_weight(w):
    d = w.shape[0]
    o = 0
    rq = w[:, o:o + RET_QK]; o += RET_QK
    rk = w[:, o:o + RET_QK]; o += RET_QK
    rest = w[:, o:o + 2 * RET_V + 2 * GDN_QK + 2 * GDN_V]; o += 2 * RET_V + 2 * GDN_QK + 2 * GDN_V
    gab = w[:, o:]

    def pad_heads(m):
        m = m.reshape(d, RET_HEADS, RET_DK)
        return jnp.pad(m, ((0, 0), (0, 0), (0, HEAD_PAD - RET_DK))).reshape(d, RET_HEADS * HEAD_PAD)

    gab = jnp.pad(gab, ((0, 0), (0, LANE - gab.shape[1])))
    return jnp.concatenate([pad_heads(rq), pad_heads(rk * RET_DK ** -0.5), rest, gab], axis=1).astype(BF16)


N_HEADS = 4
STACK = N_HEADS * CHUNK


def _prep_kernel(logit_ref, rq_ref, rk_ref, rv_ref, cq_ref, ck_ref, cv_ref, gb_ref, gt_ref,
                 rin_ref, rqd_ref, rkd_ref, rvb_ref, gu_ref, gwq_ref, gat_ref, gkd_ref, gcd_ref, *, chunks):
    wr = lax.broadcasted_iota(jnp.int32, (CHUNK, STACK), 0)
    wcol = lax.broadcasted_iota(jnp.int32, (CHUNK, STACK), 1)
    wc = wcol & (CHUNK - 1)
    whead = wcol >> 6
    r4 = lax.broadcasted_iota(jnp.int32, (STACK, STACK), 0)
    c4 = lax.broadcasted_iota(jnp.int32, (STACK, STACK), 1)
    same_head = (r4 >> 6) == (c4 >> 6)
    rc = lax.broadcasted_iota(jnp.int32, (CHUNK, CHUNK), 0)
    cc = lax.broadcasted_iota(jnp.int32, (CHUNK, CHUNK), 1)
    rpos = (lax.broadcasted_iota(jnp.int32, (STACK, LANE), 0) & (CHUNK - 1)).astype(F32)
    eye_w = jnp.where(wr == wc, 1.0, 0.0)

    def rows(x, h):
        return x[h * CHUNK:(h + 1) * CHUNK]

    def by_head(parts):
        out = parts[N_HEADS - 1]
        for h in range(N_HEADS - 2, -1, -1):
            out = jnp.where(whead == h, parts[h], out)
        return out

    def block_diag(w):
        return tuple(jnp.where(same_head, jnp.concatenate([x] * N_HEADS, axis=0), jnp.zeros((STACK, STACK), BF16))
                     for x in _split2(w))

    def wide_times(lhs, bh, bl):
        parts = [_split2(a) for a in lhs]
        hh = _dot(jnp.concatenate([x for hi_lo in parts for x in hi_lo], axis=0), bh)
        hl = _dot(jnp.concatenate([hi for hi, _ in parts], axis=0), bl)
        return [hh[2 * i * CHUNK:(2 * i + 1) * CHUNK] + hh[(2 * i + 1) * CHUNK:(2 * i + 2) * CHUNK]
                + hl[i * CHUNK:(i + 1) * CHUNK] for i in range(len(lhs))]

    ret_const = []
    for d in range(2):
        lg4 = jnp.concatenate(
            [-_softplus(-jnp.full((CHUNK, STACK), logit_ref[d * N_HEADS + h], F32)) for h in range(N_HEADS)], axis=0)
        tri4 = (r4 >= c4) if d == 0 else (r4 <= c4)
        dist4 = ((r4 - c4) if d == 0 else (c4 - r4)).astype(F32)
        dec4 = jnp.exp(jnp.where(same_head, jnp.where(tri4, lg4 * dist4, -jnp.inf), -jnp.inf))
        spos = rpos if d == 0 else (CHUNK - 1.0) - rpos
        lgc = lg4[:, :LANE]
        ret_const.append((dec4, jnp.exp(lgc * (spos + 1.0)), jnp.exp(lgc * ((CHUNK - 1.0) - spos))))

    setup = []
    for n in range(chunks):
        tok = slice(n * CHUNK, (n + 1) * CHUNK)

        def stack(ref):
            return jnp.concatenate([ref[tok, h * LANE:(h + 1) * LANE] for h in range(N_HEADS)], axis=0)

        q4, k4, v4 = stack(rq_ref), stack(rk_ref), stack(rv_ref)
        q4b, k4b, v4b = q4.astype(BF16), k4.astype(BF16), v4.astype(BF16)
        rvb_ref[n] = v4b
        qk_ret = _dot_nt(q4b, k4b)
        gq4, gk4, gv4 = stack(cq_ref), stack(ck_ref), stack(cv_ref)
        gq4b, gk4b = gq4.astype(BF16), gk4.astype(BF16)
        gb = gb_ref[tok, :]
        gt = gt_ref[n]
        gh, gl = _split2(gb)
        gll = (gb - gh.astype(F32) - gl.astype(F32)).astype(BF16)
        th, tl = _split2(gt)
        tll = (gt - th.astype(F32) - tl.astype(F32)).astype(BF16)

        intra = _dot(jnp.concatenate([(qk_ret * ret_const[d][0]).astype(BF16) for d in range(2)], axis=0), v4b)
        for d in range(2):
            _, qdec, kdec = ret_const[d]
            rin_ref[n, d] = intra[d * STACK:(d + 1) * STACK]
            rqd_ref[n, d] = (q4 * qdec).astype(BF16)
            rkd_ref[n, d] = (k4 * kdec).astype(BF16)
        gqk = [_dot_nt(rows(gq4b, h), rows(gk4b, h)) for h in range(N_HEADS)]

        for d in range(2):
            fwd = d == 0
            low = jnp.where((rc >= cc) if fwd else (rc <= cc), 1.0, 0.0).astype(BF16)
            upp_w = jnp.where((wr <= wc) if fwd else (wr >= wc), 1.0, 0.0).astype(BF16)
            tri_w = (wr >= wc) if fwd else (wr <= wc)
            tri_c = (rc >= cc) if fwd else (rc <= cc)
            gc_col = _dot(low, gh) + _dot(low, gl) + _dot(low, gll)
            t3 = _dot(jnp.concatenate([th, tl, tll], axis=0), upp_w)
            gc_row = t3[:16] + t3[16:32] + t3[32:]
            last_row = CHUNK - 1 if fwd else 0
            idx = [d * N_HEADS + h for h in range(N_HEADS)]
            gcol = [gc_col[:, i:i + 1] for i in idx]
            glast = [gc_col[last_row:last_row + 1, i:i + 1] for i in idx]
            beta = [gb[:, 2 * N_HEADS + i:2 * N_HEADS + i + 1] for i in idx]
            col4 = lambda parts: jnp.concatenate([jnp.broadcast_to(p, (CHUNK, 1)) for p in parts], axis=0)
            gcol4, glast4, beta4 = col4(gcol), col4(glast), col4(beta)
            eg4 = jnp.exp(gcol4)
            kb4 = gk4 * beta4
            gcol_w = by_head([jnp.broadcast_to(g, (CHUNK, STACK)) for g in gcol])
            grow_w = by_head([jnp.broadcast_to(gc_row[i:i + 1, :], (CHUNK, STACK)) for i in idx])
            dmat_w = jnp.exp(jnp.where(tri_w, gcol_w - grow_w, -jnp.inf))
            kk = _dot_nt(kb4.astype(BF16), gk4b)
            kk_w = by_head([rows(kk, h) for h in range(N_HEADS)])
            p = jnp.where(wr == wc, 0.0, -(kk_w * dmat_w))
            setup.append((n, d, p, gcol, glast, gcol4, glast4, beta4, eg4, kb4, gc_row, idx, tri_c,
                          gq4, gk4, gv4, gqk))

    ps = [s[2] for s in setup]
    tinvs = [eye_w + p for p in ps]
    ps = [wide_times([p], *block_diag(p))[0] for p in ps]
    for _ in range(4):
        new = [wide_times([p, t], *block_diag(p)) for p, t in zip(ps, tinvs)]
        ps = [n[0] for n in new]
        tinvs = [t + n[1] for t, n in zip(tinvs, new)]
    tinvs = [t + wide_times([t], *block_diag(p))[0] for p, t in zip(ps, tinvs)]
    t_bds = [block_diag(t) for t in tinvs]
    for (n, d, _, gcol, glast, gcol4, glast4, beta4, eg4, kb4, gc_row, idx, tri_c,
         gq4, gk4, gv4, gqk), (th_bd, tl_bd) in zip(setup, t_bds):
        rh, rl = _split2(jnp.concatenate([gv4 * beta4, kb4 * eg4], axis=1))
        both = _dot(jnp.concatenate([th_bd, tl_bd], axis=0), rh)
        uw = both[:STACK] + both[STACK:] + _dot(th_bd, rl)
        gu_ref[n, d] = uw[:, :GDN_DV]
        w4 = uw[:, GDN_DV:].astype(BF16)
        qd4 = (gq4 * eg4).astype(BF16)
        gkd_ref[n, d] = (gk4 * jnp.exp(glast4 - gcol4)).astype(BF16)
        for h in range(N_HEADS):
            gwq_ref[n, d, h] = jnp.concatenate([rows(w4, h), rows(qd4, h)], axis=0)
            dm = jnp.exp(jnp.where(tri_c, gcol[h] - gc_row[idx[h]:idx[h] + 1, :CHUNK], -jnp.inf))
            gat_ref[n, d, h] = (gqk[h] * dm).astype(BF16)
        gcd_ref[n, d] = jnp.concatenate(
            [jnp.broadcast_to(jnp.exp(g), (1, LANE)) for g in glast] + [jnp.zeros((SUBLANE - N_HEADS, LANE), F32)],
            axis=0)


def _scan_kernel(fblk, bblk, first, last, s0idx, soidx, logit_ref, *refs):
    del fblk, bblk, s0idx, soidx
    views = (refs[0:9], refs[9:18])
    s0r_ref, s0g_ref = refs[18], refs[19]
    o_refs = ((refs[20], refs[22]), (refs[21], refs[23]))
    sro_ref, sgo_ref = refs[24], refs[25]
    sr_ref, sg_ref = refs[26], refs[27]
    t = pl.program_id(0)

    @pl.when(first[t] == 1)
    def _():
        sr_ref[...] = s0r_ref[...]
        sg_ref[...] = s0g_ref[...]

    @pl.when(first[t] == 2)
    def _():
        sr_ref[...] = jnp.zeros(sr_ref.shape, F32)
        sg_ref[...] = jnp.zeros(sg_ref.shape, F32)

    chains = [(d, h) for d in range(2) for h in range(N_HEADS)]
    s_gdn = {c: sg_ref[c[0], c[1]] for c in chains}
    s_ret = {c: sr_ref[c[0], c[1]] for c in chains}
    ret_decay = {(d, h): jnp.exp(-_softplus(-jnp.full((1, LANE), logit_ref[d * N_HEADS + h], F32)) * float(CHUNK))
                 for d, h in chains}
    n_sub = views[0][0].shape[0]
    for sub in range(n_sub):
        at = (sub, n_sub - 1 - sub)

        def rows(d, k, h):
            return views[d][k][at[d], h * CHUNK:(h + 1) * CHUNK, :]

        ws = {(d, h): _dot(views[d][5][at[d], h], s_gdn[d, h].astype(BF16)) for d, h in chains}
        inter = {(d, h): _dot(rows(d, 1, h), s_ret[d, h].astype(BF16)) for d, h in chains}
        kv = {(d, h): _dot_tn(rows(d, 2, h), rows(d, 3, h)) for d, h in chains}
        v_new = {(d, h): (rows(d, 4, h) - ws[d, h][:CHUNK]).astype(BF16) for d, h in chains}
        intra = {(d, h): _dot(views[d][6][at[d], h], v_new[d, h]) for d, h in chains}
        kvn = {(d, h): _dot_tn(rows(d, 7, h), v_new[d, h]) for d, h in chains}
        for d, h in chains:
            tok = slice(at[d] * CHUNK, (at[d] + 1) * CHUNK)
            lanes = slice(h * LANE, (h + 1) * LANE)
            o_refs[d][0][tok, lanes] = rows(d, 0, h) + inter[d, h]
            o_refs[d][1][tok, lanes] = ws[d, h][CHUNK:] + intra[d, h]
            s_ret[d, h] = s_ret[d, h] * ret_decay[d, h] + kv[d, h]
            s_gdn[d, h] = s_gdn[d, h] * views[d][8][at[d], h:h + 1, :] + kvn[d, h]
    for d, h in chains:
        sr_ref[d, h] = s_ret[d, h]
        sg_ref[d, h] = s_gdn[d, h]

    @pl.when(last[t] == 1)
    def _():
        sro_ref[...] = sr_ref[...]
        sgo_ref[...] = sg_ref[...]


def _scan_tables(lay):
    fblk, bblk, first, last, s0idx, soidx = [], [], [], [], [], []
    blk = SCAN_CHUNKS * CHUNK
    assert lay.seq % blk == 0 and lay.dec_seq % blk == 0
    cp, cs = lay.seq // blk, lay.dec_seq // blk
    for b in range(lay.dec_batch):
        base = lay.n_p // blk + b * cs
        for s in range(cs):
            fblk.append(base + s); bblk.append(base + cs - 1 - s)
            first.append(1 if s == 0 else 0); last.append(0); s0idx.append(b); soidx.append(0)
    for b in range(lay.batch):
        base = b * cp
        for s in range(cp):
            fblk.append(base + s); bblk.append(base + cp - 1 - s)
            first.append(2 if s == 0 else 0); last.append(1 if s == cp - 1 else 0)
            s0idx.append(lay.dec_batch - 1); soidx.append(b)
    return [jnp.asarray(np.asarray(a, np.int32)) for a in (fblk, bblk, first, last, s0idx, soidx)]


def _scan_prep(lay, rq, rk, rv, cq, ck, cv, gb, gt, logit):
    n_chunks = lay.n // CHUNK
    nc = PREP_CHUNKS
    assert n_chunks % nc == 0
    width = N_HEADS * LANE

    def tok(wd):
        return pl.BlockSpec((nc * CHUNK, wd), lambda i: (i, 0))

    def per_dir(shape, dtype):
        spec = pl.BlockSpec((nc, 2) + shape, lambda i: (i,) + (0,) * (1 + len(shape)))
        return spec, jax.ShapeDtypeStruct((n_chunks, 2) + shape, dtype)

    outs = [
        per_dir((STACK, LANE), F32),
        per_dir((STACK, LANE), BF16),
        per_dir((STACK, LANE), BF16),
        (pl.BlockSpec((nc, STACK, LANE), lambda i: (i, 0, 0)), jax.ShapeDtypeStruct((n_chunks, STACK, LANE), BF16)),
        per_dir((STACK, LANE), F32),
        per_dir((N_HEADS, 2 * CHUNK, LANE), BF16),
        per_dir((N_HEADS, CHUNK, CHUNK), BF16),
        per_dir((STACK, LANE), BF16),
        per_dir((SUBLANE, LANE), F32),
    ]
    return pl.pallas_call(
        functools.partial(_prep_kernel, chunks=nc),
        grid=(n_chunks // nc,),
        in_specs=[pl.BlockSpec(memory_space=pltpu.SMEM)] + [tok(width)] * 6
                 + [tok(LANE), pl.BlockSpec((nc, 4 * GDN_HEADS, CHUNK), lambda i: (i, 0, 0))],
        out_specs=[o[0] for o in outs],
        out_shape=[o[1] for o in outs],
        compiler_params=_params("parallel"),
        name="scan_prep",
    )(logit.reshape(-1).astype(F32), rq, rk, rv, cq, ck, cv, gb, gt)


def _even_scan(lay, prep, logit, s0_ret, s0_gdn):
    tables = _scan_tables(lay)
    steps = int(tables[0].shape[0])
    width = N_HEADS * LANE

    def view(which):
        def blk(arr, with_dir=True):
            shape = arr.shape[2:] if with_dir else arr.shape[1:]
            if with_dir:
                return pl.BlockSpec((SCAN_CHUNKS, None) + shape,
                                    lambda t, *tb: (tb[which][t], which) + (0,) * len(shape))
            return pl.BlockSpec((SCAN_CHUNKS,) + shape, lambda t, *tb: (tb[which][t],) + (0,) * len(shape))
        return [blk(a, with_dir=(i != 3)) for i, a in enumerate(prep)]

    state_shape = (2, N_HEADS, LANE, LANE)
    s0_spec = pl.BlockSpec((None,) + state_shape, lambda t, *tb: (tb[4][t], 0, 0, 0, 0))
    so_spec = pl.BlockSpec((None,) + state_shape, lambda t, *tb: (tb[5][t], 0, 0, 0, 0))
    out_f = pl.BlockSpec((SCAN_CHUNKS * CHUNK, width), lambda t, *tb: (tb[0][t], 0))
    out_b = pl.BlockSpec((SCAN_CHUNKS * CHUNK, width), lambda t, *tb: (tb[1][t], 0))
    o_sds = jax.ShapeDtypeStruct((lay.n, width), F32)
    so_sds = jax.ShapeDtypeStruct((lay.batch,) + state_shape, F32)
    args = list(prep)
    return pl.pallas_call(
        _scan_kernel,
        grid_spec=pltpu.PrefetchScalarGridSpec(
            num_scalar_prefetch=6,
            grid=(steps,),
            in_specs=[pl.BlockSpec(memory_space=pltpu.SMEM)] + view(0) + view(1) + [s0_spec, s0_spec],
            out_specs=[out_f, out_b, out_f, out_b, so_spec, so_spec],
            scratch_shapes=[pltpu.VMEM(state_shape, F32), pltpu.VMEM(state_shape, F32)],
        ),
        out_shape=[o_sds, o_sds, o_sds, o_sds, so_sds, so_sds],
        compiler_params=_params("arbitrary"),
        name="even_scan",
    )(*tables, logit.reshape(-1).astype(F32), *args, *args, s0_ret, s0_gdn)


def _odd_proj_kernel(x_ref, mod_ref, nw_ref, w_ref, qnw_ref, knw_ref, cos_ref, sin_ref,
                     q_ref, kd_ref, vt_ref, ks_ref, vs_ref, *, p_tiles):
    h = _adaln(x_ref[...], nw_ref[...], mod_ref[3:4, :], mod_ref[4:5, :]).astype(BF16)
    tm = h.shape[0]
    lane = lax.broadcasted_iota(jnp.int32, (tm, LANE), 1)
    lo = lane < ATT_HD
    first16 = (lane & (2 * ROPE_PAIRS - 1)) < ROPE_PAIRS
    cos, sin = cos_ref[...], sin_ref[...]

    def norm_rope(xs, w2):
        sq = xs * xs
        ms_lo = jnp.sum(jnp.where(lo, sq, 0.0), axis=-1, keepdims=True) * (1.0 / ATT_HD)
        ms_hi = jnp.sum(jnp.where(lo, 0.0, sq), axis=-1, keepdims=True) * (1.0 / ATT_HD)
        xn = xs * jnp.where(lo, lax.rsqrt(ms_lo + EPS), lax.rsqrt(ms_hi + EPS)) * w2
        partner = jnp.where(first16, pltpu.roll(xn, LANE - ROPE_PAIRS, 1), pltpu.roll(xn, ROPE_PAIRS, 1))
        return xn * cos + partner * sin

    def both_halves(xs, j):
        swapped = pltpu.roll(xs, ATT_HD, 1)
        return jnp.where(lo, xs, swapped) if j % 2 == 0 else jnp.where(lo, swapped, xs)

    for t in range(ATT_Q // MXU_TILE):
        y = _dot(h, w_ref[:, t * MXU_TILE:(t + 1) * MXU_TILE])
        for p in (2 * t, 2 * t + 1):
            qs = norm_rope(y[:, (p % 2) * LANE:(p % 2 + 1) * LANE], qnw_ref[...])
            q_ref[:, p * LANE:(p + 1) * LANE] = (qs * Q_SCALE).astype(BF16)
    is_prompt = pl.program_id(0) < p_tiles
    assert ATT_KV == MXU_TILE
    yk = _dot(h, w_ref[:, ATT_Q:ATT_Q + ATT_KV])
    yv = _dot(h, w_ref[:, ATT_Q + ATT_KV:ATT_Q + 2 * ATT_KV])
    for p in range(ATT_KV // LANE):
        ks = norm_rope(yk[:, p * LANE:(p + 1) * LANE], knw_ref[...])
        vs = yv[:, p * LANE:(p + 1) * LANE]
        vst = vs.T.astype(BF16)
        for j in (2 * p, 2 * p + 1):
            kd_ref[j] = both_halves(ks, j).astype(BF16)
            for c in range(tm // KEY_CHUNK):
                vt_ref[j, c] = vst[(j % 2) * ATT_HD:(j % 2 + 1) * ATT_HD, c * KEY_CHUNK:(c + 1) * KEY_CHUNK]

        @pl.when(is_prompt)
        def _():
            ks_ref[:, p * LANE:(p + 1) * LANE] = ks
            vs_ref[:, p * LANE:(p + 1) * LANE] = vs


def _rope_tables(lay):
    t = jnp.arange(lay.dec_seq)
    inv = ROPE_THETA ** (-jnp.arange(ROPE_PAIRS, dtype=F32) / ROPE_PAIRS)
    ar = (t // GRID_W).astype(F32)[:, None] * inv
    ac = (t % GRID_W).astype(F32)[:, None] * inv
    cos = jnp.concatenate([jnp.cos(ar), jnp.cos(ar), jnp.cos(ac), jnp.cos(ac)], axis=1)
    sin = jnp.concatenate([-jnp.sin(ar), jnp.sin(ar), -jnp.sin(ac), jnp.sin(ac)], axis=1)
    cos = jnp.concatenate([jnp.tile(cos, (1, 2)), jnp.ones((lay.tm, LANE), F32)], axis=0)
    sin = jnp.concatenate([jnp.tile(sin, (1, 2)), jnp.zeros((lay.tm, LANE), F32)], axis=0)
    return cos, sin


def _odd_proj(lay, x, mods, nw, w, qnw, knw):
    cos, sin = _rope_tables(lay)

    def table_block(i):
        return jnp.where(i < lay.p_tiles, lay.tiles_per_seq, lax.rem(jnp.maximum(i - lay.p_tiles, 0), lay.tiles_per_seq))

    tab = pl.BlockSpec((lay.tm, LANE), lambda i: (table_block(i), 0))
    std = pl.BlockSpec((lay.tm, ATT_KV), lambda i: (jnp.minimum(i, lay.p_tiles - 1), 0))
    tm = lay.tm
    return pl.pallas_call(
        functools.partial(_odd_proj_kernel, p_tiles=lay.p_tiles),
        grid=(lay.tiles,),
        in_specs=[lay.tok(D_MODEL), lay.mod(), _resident((1, D_MODEL)), _resident(w.shape),
                  _resident((1, LANE)), _resident((1, LANE)), tab, tab],
        out_specs=[
            lay.tok(ATT_Q),
            pl.BlockSpec((ATT_KV_HEADS, tm, LANE), lambda i: (0, i, 0)),
            pl.BlockSpec((ATT_KV_HEADS, tm // KEY_CHUNK, ATT_HD, KEY_CHUNK), lambda i: (0, i, 0, 0)),
            std, std,
        ],
        out_shape=[
            jax.ShapeDtypeStruct((lay.n, ATT_Q), BF16),
            jax.ShapeDtypeStruct((ATT_KV_HEADS, lay.n, LANE), BF16),
            jax.ShapeDtypeStruct((ATT_KV_HEADS, lay.n // KEY_CHUNK, ATT_HD, KEY_CHUNK), BF16),
            jax.ShapeDtypeStruct((lay.n_p, ATT_KV), F32),
            jax.ShapeDtypeStruct((lay.n_p, ATT_KV), F32),
        ],
        compiler_params=_params("arbitrary"),
        name="odd_proj",
    )(x, mods, nw.reshape(1, D_MODEL), w, jnp.tile(qnw.reshape(1, ATT_HD), (1, 2)),
      jnp.tile(knw.reshape(1, ATT_HD), (1, 2)), cos, sin)


def _attn_kernel(*refs, cached):
    if cached:
        zero_ref, q_ref, kn_ref, vn_ref, kc_ref, vc_ref, _, o_ref, *s_scr = refs
    else:
        zero_ref, q_ref, kn_ref, vn_ref, o_ref, *s_scr = refs
    z = zero_ref[0]
    tq = MXU_TILE
    n_sub = q_ref.shape[0] // tq
    n_kv = kn_ref.shape[0]
    gw = ATT_GROUP * ATT_HD
    ck = KEY_CHUNK
    n_c = kc_ref.shape[1] // ck if cached else 0
    n_n = kn_ref.shape[1] // ck
    lo = lax.broadcasted_iota(jnp.int32, (tq, LANE), 1) < ATT_HD
    qms, kv_of = [], []
    for s in range(n_sub):
        for j in range(n_kv):
            for g in range(ATT_GROUP):
                q2 = q_ref[s * tq:(s + 1) * tq, j * gw + (g // 2) * LANE:j * gw + (g // 2 + 1) * LANE]
                qms.append(jnp.where(lo, q2, jnp.zeros_like(q2)) if g % 2 == 0
                           else jnp.where(lo, jnp.zeros_like(q2), q2))
                kv_of.append(j)
    n_units = len(qms)

    def keys(j, c):
        return kc_ref[j, c * ck:(c + 1) * ck, :] if c < n_c else kn_ref[j, (c - n_c) * ck:(c - n_c + 1) * ck, :]

    def values_t(j, c):
        return vc_ref[j, c] if c < n_c else vn_ref[j, c - n_c]

    ones_rows = jnp.where(lax.broadcasted_iota(jnp.int32, (2 * SUBLANE, ck), 0) == 0, 1.0, 0.0).astype(BF16)
    m_fin = [None] * n_units
    outs = [None] * n_units
    for ph in range(n_units + 1):
        ga = ph if ph < n_units else None
        gb = ph - 1 if ph >= 1 else None
        ma = jnp.full((1, tq), -jnp.inf, F32)
        acc = jnp.zeros((ATT_HD + 2 * SUBLANE, tq), F32)
        for c in range(n_c + n_n):
            if ga is not None:
                st = _dot_nt(keys(kv_of[ga], c), qms[ga])
                s_scr[ga % 2][z + c] = st
                ma = jnp.maximum(ma, jnp.max(st, axis=0, keepdims=True))
            if gb is not None:
                p = jnp.exp2(s_scr[gb % 2][z + c] - m_fin[gb]).astype(BF16)
                acc = acc + _dot(jnp.concatenate([values_t(kv_of[gb], c), ones_rows], axis=0), p)
        if ga is not None:
            m_fin[ga] = ma
        if gb is not None:
            outs[gb] = acc[:ATT_HD] / acc[ATT_HD:ATT_HD + 1]
    per_sub = n_kv * ATT_GROUP
    for s in range(n_sub):
        o_ref[s * tq:(s + 1) * tq, :] = jnp.concatenate(outs[s * per_sub:(s + 1) * per_sub], axis=0).T.astype(BF16)


def _attention(lay, q, kd, vt, kc, vc):
    gw = ATT_GROUP * ATT_HD
    o_sds = jax.ShapeDtypeStruct((lay.n, ATT_Q), BF16)
    ck = KEY_CHUNK
    assert lay.seq % ck == 0 and lay.dec_seq % ck == 0 and lay.past % ck == 0
    assert lay.seq % MXU_TILE == 0 and lay.dec_seq % MXU_TILE == 0 and ATT_Q_TILE % MXU_TILE == 0

    zero = jnp.zeros((1,), jnp.int32)
    zero_spec = pl.BlockSpec(memory_space=pltpu.SMEM)

    tq = min(ATT_Q_TILE, lay.seq)
    nq = lay.seq // tq
    nck = lay.seq // ck
    out = pl.pallas_call(
        functools.partial(_attn_kernel, cached=False),
        grid=(lay.batch, nq),
        in_specs=[
            zero_spec,
            pl.BlockSpec((tq, ATT_Q), lambda b, i: (b * nq + i, 0)),
            pl.BlockSpec((ATT_KV_HEADS, lay.seq, LANE), lambda b, i: (0, b, 0)),
            pl.BlockSpec((ATT_KV_HEADS, nck, ATT_HD, ck), lambda b, i: (0, b, 0, 0)),
        ],
        out_specs=pl.BlockSpec((tq, ATT_Q), lambda b, i: (b * nq + i, 0)),
        out_shape=o_sds,
        scratch_shapes=[pltpu.VMEM((nck, ck, MXU_TILE), F32)] * 2,
        compiler_params=_params("parallel", "parallel"),
        name="attn_context",
    )(zero, q, kd, vt)

    tq = min(ATT_Q_TILE, lay.dec_seq)
    nq = lay.dec_seq // tq
    nck = lay.dec_seq // ck
    qbase = lay.n_p // tq
    kbase = lay.n_p // lay.dec_seq
    assert lay.n_p % lay.dec_seq == 0
    return pl.pallas_call(
        functools.partial(_attn_kernel, cached=True),
        grid=(lay.dec_batch, ATT_KV_HEADS, nq),
        in_specs=[
            zero_spec,
            pl.BlockSpec((tq, gw), lambda b, j, i: (qbase + b * nq + i, j)),
            pl.BlockSpec((1, lay.dec_seq, LANE), lambda b, j, i: (j, kbase + b, 0)),
            pl.BlockSpec((1, nck, ATT_HD, ck), lambda b, j, i: (j, kbase + b, 0, 0)),
            pl.BlockSpec((None, 1, lay.past, LANE), lambda b, j, i: (b, j, 0, 0)),
            pl.BlockSpec((None, 1, lay.past // ck, ATT_HD, ck), lambda b, j, i: (b, j, 0, 0, 0)),
            pl.BlockSpec(memory_space=pl.ANY),
        ],
        out_specs=pl.BlockSpec((tq, gw), lambda b, j, i: (qbase + b * nq + i, j)),
        out_shape=o_sds,
        scratch_shapes=[pltpu.VMEM((lay.past // ck + nck, ck, MXU_TILE), F32)] * 2,
        input_output_aliases={6: 0},
        compiler_params=_params("parallel", "parallel", "parallel"),
        name="attn_latent",
    )(zero, q, kd, vt, kc, vc, out)


def _pad_state(s):
    return jnp.pad(s, [(0, 0)] * (s.ndim - 2) + [(0, LANE - s.shape[-2]), (0, 0)])


def _cache_keys(c):
    return jnp.tile(c.transpose(0, 2, 1, 3), (1, 1, 1, 2)).astype(BF16)


def _cache_values(c):
    b, t, kvh, hd = c.shape
    return c.reshape(b, t // KEY_CHUNK, KEY_CHUNK, kvh, hd).transpose(0, 3, 1, 4, 2).astype(BF16)


def kernel(x_prompt, x_sample, state_ret, state_gdn, cache_k, cache_v, c, c_ctx,
           mod_w, mod_b, norm_w, ffn_w_in, ffn_w_out, even_w_in, even_w_out,
           ret_decay_logit, ret_norm_w, gdn_conv_w, gdn_A_log, gdn_dt_bias, gdn_norm_w,
           odd_w_in, odd_w_out, q_norm_w, k_norm_w, final_norm_w):
    batch, seq, d = x_prompt.shape
    dec_batch, dec_seq, _ = x_sample.shape
    depth = mod_w.shape[0]
    lay = _Layout(batch, seq, dec_batch, dec_seq, cache_k.shape[2])
    lay_ffn = _Layout(batch, seq, dec_batch, dec_seq, cache_k.shape[2], tile=FFN_TILE)

    x = (x_prompt.reshape(lay.n_p, d), x_sample.reshape(lay.n_s, d))
    n_cond = -(-(1 + dec_batch) // (2 * SUBLANE)) * (2 * SUBLANE)
    cond = jnp.zeros((n_cond, d), F32).at[0].set(c_ctx).at[1:1 + dec_batch].set(c)
    mods = _modulation(cond, mod_w, mod_b)
    ffn_in, ffn_out = ffn_w_in.astype(BF16), ffn_w_out.astype(BF16)

    new_ret, new_gdn, new_k, new_v = [], [], [], []
    for l in range(depth):
        m = mods[l]
        last = l == depth - 1
        x = _ffn(lay_ffn, x, m, norm_w[l, 0], ffn_in, ffn_out, (l, 0), 0)
        if l % 2 == 0:
            e = l // 2
            rq, rk, rv, rg, gg, cq, ck, cv, gb, gt = _even_proj(
                lay, x, m, norm_w[l, 1], _even_weight(even_w_in[e]), gdn_conv_w[e], gdn_A_log[e], gdn_dt_bias[e])
            prep = _scan_prep(lay, rq, rk, rv, cq, ck, cv, gb, gt, ret_decay_logit[e])
            orf, orb, ogf, ogb, sr, sg = _even_scan(
                lay, prep, ret_decay_logit[e], _pad_state(state_ret[:, e]), state_gdn[:, e])
            new_ret.append(sr[:, :, :, :RET_DK, :])
            new_gdn.append(sg)
            mixer = "even"
            mixer_ins = (orf, orb, ogf, ogb, rg, gg, ret_norm_w[e].reshape(1, LANE), gdn_norm_w[e].reshape(1, LANE),
                         even_w_out[e].astype(BF16))
        else:
            o = l // 2
            q, kd, vt, ks, vs = _odd_proj(lay, x, m, norm_w[l, 1], odd_w_in[o].astype(BF16), q_norm_w[o], k_norm_w[o])
            a = _attention(lay, q, kd, vt, _cache_keys(cache_k[:, o]), _cache_values(cache_v[:, o]))
            new_k.append(ks.reshape(batch, seq, ATT_KV_HEADS, ATT_HD))
            new_v.append(vs.reshape(batch, seq, ATT_KV_HEADS, ATT_HD))
            mixer = "odd"
            mixer_ins = (a, odd_w_out[o].astype(BF16))
        x = _ffn(lay, x, m, norm_w[l, 2], ffn_in, ffn_out, (l, 1), 2,
                 final_w=final_norm_w if last else None, mixer=mixer, mixer_ins=mixer_ins)

    y_prompt = x[0].reshape(batch, seq, d)
    y_sample = x[1].reshape(dec_batch, dec_seq, d)
    return (y_prompt, y_sample, jnp.stack(new_ret, axis=1), jnp.stack(new_gdn, axis=1),
            jnp.stack(new_k, axis=1), jnp.stack(new_v, axis=1))
```

```python
import functools

import numpy as np
import jax
import jax.numpy as jnp
from jax import lax
from jax.experimental import pallas as pl
from jax.experimental.pallas import tpu as pltpu

F32 = jnp.float32
BF16 = jnp.bfloat16

D_MODEL = 1024
GRID_W = 64
RET_HEADS = 4
RET_DK = 64
RET_DV = 128
GDN_HEADS = 4
GDN_DK = 128
GDN_DV = 128
CHUNK = 64
ATT_HEADS = 16
ATT_KV_HEADS = 4
ATT_HD = 64
ATT_GROUP = ATT_HEADS // ATT_KV_HEADS
ROPE_THETA = 10000.0
ROPE_PAIRS = ATT_HD // 4
FFN_HIDDEN = 2816
N_MOD = 9
EPS = 1e-6

RET_QK = RET_HEADS * RET_DK
RET_V = RET_HEADS * RET_DV
GDN_QK = GDN_HEADS * GDN_DK
GDN_V = GDN_HEADS * GDN_DV
ATT_Q = ATT_HEADS * ATT_HD
ATT_KV = ATT_KV_HEADS * ATT_HD

LANE = 128
SUBLANE = 8
V7X_VMEM_BYTES = 64 * 1024 * 1024
VMEM_COMPILER_RESERVE = 8 * 1024 * 1024
VMEM_LIMIT = V7X_VMEM_BYTES - VMEM_COMPILER_RESERVE

CHUNK_SHIFT = CHUNK.bit_length() - 1
assert 1 << CHUNK_SHIFT == CHUNK
MOD_COL_TILES = 8
TOKEN_TILE = 512
FFN_TILE = 1024
ATT_Q_TILE = 1024
KEY_CHUNK = 256
Q_SCALE = ATT_HD ** -0.5 * float(np.log2(np.e))
MXU_TILE = 256
FFN_CHUNKS = ((0, 6 * MXU_TILE), (6 * MXU_TILE, FFN_HIDDEN))
HEAD_PAD = LANE
PREP_CHUNKS = 4
SCAN_CHUNKS = 4


def _params(*sem):
    return pltpu.CompilerParams(dimension_semantics=sem, vmem_limit_bytes=VMEM_LIMIT)


def _resident(shape):
    nd = len(shape)
    return pl.BlockSpec(shape, lambda *_: (0,) * nd, pipeline_mode=pl.Buffered(1))


def _resident_at(arr, lead):
    shape = arr.shape[len(lead):]
    return pl.BlockSpec((None,) * len(lead) + shape, lambda *_: tuple(lead) + (0,) * len(shape),
                        pipeline_mode=pl.Buffered(1))


def _dot(a, b):
    return jnp.dot(a, b, preferred_element_type=F32)


def _dot_nt(a, b):
    return lax.dot_general(a, b, (((1,), (1,)), ((), ())), preferred_element_type=F32)


def _dot_tn(a, b):
    return lax.dot_general(a, b, (((0,), (0,)), ((), ())), preferred_element_type=F32)


def _split2(a):
    hi = a.astype(BF16)
    lo = (a - hi.astype(F32)).astype(BF16)
    return hi, lo


def _silu(x):
    return x * jax.nn.sigmoid(x)


def _softplus(x):
    return jnp.maximum(x, 0.0) + jnp.log(1.0 + jnp.exp(-jnp.abs(x)))


def _adaln(x, nw, shift, scale):
    ms = jnp.mean(x * x, axis=-1, keepdims=True)
    return (x * lax.rsqrt(ms + EPS)) * nw * (1.0 + scale) + shift


def _head_rmsnorm(x, w, n_heads):
    parts = []
    for h in range(n_heads):
        xs = x[:, h * LANE:(h + 1) * LANE]
        ms = jnp.mean(xs * xs, axis=-1, keepdims=True)
        parts.append(xs * lax.rsqrt(ms + EPS) * w)
    return jnp.concatenate(parts, axis=1)


def _mod_kernel(cond_ref, w_ref, b_ref, o_ref):
    c = cond_ref[...]
    o_ref[0] = _dot(_silu(c).astype(BF16), w_ref[0].astype(BF16)) + b_ref[0]


def _modulation(cond, mod_w, mod_b):
    depth, d, n = mod_w.shape
    r = cond.shape[0]
    tn = n // MOD_COL_TILES
    out = pl.pallas_call(
        _mod_kernel,
        grid=(depth, n // tn),
        in_specs=[
            pl.BlockSpec((r, d), lambda l, j: (0, 0)),
            pl.BlockSpec((1, d, tn), lambda l, j: (l, 0, j)),
            pl.BlockSpec((1, 1, tn), lambda l, j: (l, 0, j)),
        ],
        out_specs=pl.BlockSpec((1, r, tn), lambda l, j: (l, 0, j)),
        out_shape=jax.ShapeDtypeStruct((depth, r, n), F32),
        compiler_params=_params("parallel", "parallel"),
        name="modulation",
    )(cond, mod_w, mod_b.reshape(depth, 1, n))
    return out.reshape(depth, r, N_MOD, d)


class _Layout:
    def __init__(self, batch, seq, dec_batch, dec_seq, past, tile=TOKEN_TILE):
        self.batch, self.seq, self.dec_batch, self.dec_seq, self.past = batch, seq, dec_batch, dec_seq, past
        self.n_p = batch * seq
        self.n_s = dec_batch * dec_seq
        self.n = self.n_p + self.n_s
        self.tm = min(tile, self.n_p, dec_seq)
        assert self.n_p % self.tm == 0 and dec_seq % self.tm == 0
        self.tiles = self.n // self.tm
        self.p_tiles = self.n_p // self.tm
        self.tiles_per_seq = dec_seq // self.tm

    def group(self, i):
        return jnp.where(i < self.p_tiles, 0, 1 + (i - self.p_tiles) // self.tiles_per_seq)

    def tok(self, width, tm=None):
        tm = tm or self.tm
        return pl.BlockSpec((tm, width), lambda i: (i, 0))

    def mod(self):
        return pl.BlockSpec((None, N_MOD, D_MODEL), lambda i: (self.group(i), 0, 0))


def _ffn_kernel(*refs, j, first, final, p_tiles, mixer):
    refs = list(refs)
    x_refs = [refs.pop(0) for _ in range(2 if first else 1)]
    n_mix = {None: 0, "odd": 2, "even": 9}[mixer]
    mix_refs = [refs.pop(0) for _ in range(n_mix)]
    mod_ref, nw_ref, win_ref, wout_ref = (refs.pop(0) for _ in range(4))
    fw_ref = refs.pop(0) if final else None
    o_refs = refs
    is_prompt = pl.program_id(0) < p_tiles
    x = jnp.where(is_prompt, x_refs[0][...], x_refs[1][...]) if first else x_refs[0][...]
    if mixer == "odd":
        a_ref, wo_ref = mix_refs
        x = x + mod_ref[5:6, :] * _dot(a_ref[...], wo_ref[...])
    elif mixer == "even":
        rf_ref, rb_ref, gf_ref, gb_ref, rg_ref, gg_ref, rnw_ref, gnw_ref, wo_ref = mix_refs
        ret = _head_rmsnorm(rf_ref[...] + rb_ref[...], rnw_ref[...], RET_HEADS) * _silu(rg_ref[...])
        gdn = _head_rmsnorm(gf_ref[...] + gb_ref[...], gnw_ref[...], GDN_HEADS) * _silu(gg_ref[...])
        x = x + mod_ref[5:6, :] * _dot(jnp.concatenate([ret, gdn], axis=1).astype(BF16), wo_ref[...])
    h = _adaln(x, nw_ref[...], mod_ref[3 * j:3 * j + 1, :], mod_ref[3 * j + 1:3 * j + 2, :]).astype(BF16)
    acc = None
    for lo, hi in FFN_CHUNKS:
        a = _dot(h, win_ref[:, lo:hi])
        b = _dot(h, win_ref[:, FFN_HIDDEN + lo:FFN_HIDDEN + hi])
        y = _dot((_silu(a) * b).astype(BF16), wout_ref[lo:hi, :])
        acc = y if acc is None else acc + y
    out = x + (0.5 * mod_ref[3 * j + 2:3 * j + 3, :]) * acc
    if not final:
        o_refs[0][...] = out
        return
    ms = jnp.mean(out * out, axis=-1, keepdims=True)
    out = out * lax.rsqrt(ms + EPS) * fw_ref[...]

    @pl.when(is_prompt)
    def _():
        o_refs[0][...] = out

    @pl.when(jnp.logical_not(is_prompt))
    def _():
        o_refs[1][...] = out


def _ffn(lay, xs, mods, nw, w_in, w_out, which, j, final_w=None, mixer=None, mixer_ins=()):
    first = isinstance(xs, tuple)
    final = final_w is not None
    prompt_rows = pl.BlockSpec((lay.tm, D_MODEL), lambda i: (jnp.minimum(i, lay.p_tiles - 1), 0))
    latent_rows = pl.BlockSpec((lay.tm, D_MODEL), lambda i: (jnp.maximum(i - lay.p_tiles, 0), 0))
    ins = list(xs) if first else [xs]
    specs = [prompt_rows, latent_rows] if first else [lay.tok(D_MODEL)]
    for arr in mixer_ins:
        ins.append(arr)
        specs.append(lay.tok(arr.shape[1]) if arr.shape[0] == lay.n else _resident(arr.shape))
    ins += [mods, nw.reshape(1, D_MODEL), w_in, w_out]
    specs += [lay.mod(), _resident((1, D_MODEL)), _resident_at(w_in, which), _resident_at(w_out, which)]
    if final:
        ins.append(final_w.reshape(1, D_MODEL))
        specs.append(_resident((1, D_MODEL)))
        out_specs = [prompt_rows, latent_rows]
        out_shape = [jax.ShapeDtypeStruct((lay.n_p, D_MODEL), F32), jax.ShapeDtypeStruct((lay.n_s, D_MODEL), F32)]
    else:
        out_specs = lay.tok(D_MODEL)
        out_shape = jax.ShapeDtypeStruct((lay.n, D_MODEL), F32)
    return pl.pallas_call(
        functools.partial(_ffn_kernel, j=j, first=first, final=final, p_tiles=lay.p_tiles, mixer=mixer),
        grid=(lay.tiles,),
        in_specs=specs,
        out_specs=out_specs,
        out_shape=out_shape,
        compiler_params=_params("arbitrary"),
        name=("ffn_final" if final else ("ffn_first" if first else "ffn")) + ("_" + mixer if mixer else ""),
    )(*ins)


EVEN_WIDTHS = (RET_HEADS * HEAD_PAD, RET_HEADS * HEAD_PAD, RET_V, RET_V, 2 * GDN_QK + GDN_V, GDN_V, LANE)


def _even_proj_kernel(x_ref, prev_ref, next_ref, mod_ref, nw_ref, w_ref, cw_ref, alog_ref, dt_ref,
                      rq_ref, rk_ref, rv_ref, rg_ref, gg_ref, cq_ref, ck_ref, cv_ref, gb_ref, gt_ref,
                      *, n_p, seq, dec_seq):
    tm = x_ref.shape[0]
    shift, scale = mod_ref[3:4, :], mod_ref[4:5, :]
    h = _adaln(x_ref[...], nw_ref[...], shift, scale).astype(BF16)
    halo = jnp.concatenate([prev_ref[...], next_ref[...]], axis=0)
    hh = _adaln(halo, nw_ref[...], shift, scale).astype(BF16)
    offs = np.cumsum((0,) + EVEN_WIDTHS)
    col = lambda i: slice(int(offs[i]), int(offs[i + 1]))

    start = pl.program_id(0) * tm
    in_prompt = start < n_p
    rel = jnp.where(in_prompt, start, start - n_p)
    slen = jnp.where(in_prompt, seq, dec_seq)
    row = lax.broadcasted_iota(jnp.int32, (tm, GDN_QK), 0)
    pos = (rel + row) & (slen - 1)

    def conv(part):
        lanes = slice(int(offs[4]) + part * GDN_QK, int(offs[4]) + (part + 1) * GDN_QK)
        cw = cw_ref[:, part * GDN_QK:(part + 1) * GDN_QK]
        y = _dot(h, w_ref[:, lanes])
        yh = _dot(hh, w_ref[:, lanes])
        ym1 = jnp.where(pos == 0, 0.0, jnp.where(row == 0, yh[SUBLANE - 1:SUBLANE, :], pltpu.roll(y, 1, 0)))
        yp1 = jnp.where(pos == slen - 1, 0.0,
                        jnp.where(row == tm - 1, yh[SUBLANE:SUBLANE + 1, :], pltpu.roll(y, tm - 1, 0)))
        return _silu(ym1 * cw[0:1, :] + y * cw[1:2, :] + yp1 * cw[2:3, :])

    def l2norm_heads(y, o_ref, scale):
        for hd in range(GDN_HEADS):
            v = y[:, hd * LANE:(hd + 1) * LANE]
            o_ref[:, hd * LANE:(hd + 1) * LANE] = v * lax.rsqrt(jnp.sum(v * v, axis=-1, keepdims=True) + EPS) * scale

    l2norm_heads(conv(0), cq_ref, GDN_DK ** -0.5)
    rq_ref[...] = _dot(h, w_ref[:, col(0)])
    rk_ref[...] = _dot(h, w_ref[:, col(1)])
    l2norm_heads(conv(1), ck_ref, 1.0)
    rv_ref[...] = _dot(h, w_ref[:, col(2)])
    rg_ref[...] = _dot(h, w_ref[:, col(3)])
    cv_ref[...] = conv(2)
    gg_ref[...] = _dot(h, w_ref[:, col(5)])
    g = _dot(h, w_ref[:, col(6)])
    lane = lax.broadcasted_iota(jnp.int32, g.shape, 1)
    gb = jnp.where(lane < 2 * GDN_HEADS, -jnp.exp(alog_ref[...]) * _softplus(g + dt_ref[...]), jax.nn.sigmoid(g))
    gb_ref[...] = gb
    gbt = gb.T
    for c in range(tm // CHUNK):
        gt_ref[c] = gbt[0:4 * GDN_HEADS, c * CHUNK:(c + 1) * CHUNK]


def _even_proj(lay, x, mods, nw, w, conv_w, a_log, dt_bias):
    tm = lay.tm
    assert lay.seq & (lay.seq - 1) == 0 and lay.dec_seq & (lay.dec_seq - 1) == 0 and tm % LANE == 0
    nblk8 = lay.n // SUBLANE
    r = tm // SUBLANE
    width = GDN_V
    assert all(wd == width for wd in EVEN_WIDTHS[:4] + EVEN_WIDTHS[5:6]) and GDN_QK == width

    def pad_row(v):
        v = v.reshape(1, -1).astype(F32)
        return jnp.pad(v, ((0, 0), (0, LANE - v.shape[1])))

    tok_out = [(width, F32)] * 8 + [(LANE, F32)]
    return pl.pallas_call(
        functools.partial(_even_proj_kernel, n_p=lay.n_p, seq=lay.seq, dec_seq=lay.dec_seq),
        grid=(lay.tiles,),
        in_specs=[
            lay.tok(D_MODEL),
            pl.BlockSpec((SUBLANE, D_MODEL), lambda i: (jnp.maximum(i * r - 1, 0), 0)),
            pl.BlockSpec((SUBLANE, D_MODEL), lambda i: (jnp.minimum((i + 1) * r, nblk8 - 1), 0)),
            lay.mod(), _resident((1, D_MODEL)), _resident(w.shape), _resident(conv_w.shape),
            _resident((1, LANE)), _resident((1, LANE)),
        ],
        out_specs=[lay.tok(wd) for wd, _ in tok_out]
                  + [pl.BlockSpec((tm // CHUNK, 4 * GDN_HEADS, CHUNK), lambda i: (i, 0, 0))],
        out_shape=[jax.ShapeDtypeStruct((lay.n, wd), dt) for wd, dt in tok_out]
                  + [jax.ShapeDtypeStruct((lay.n // CHUNK, 4 * GDN_HEADS, CHUNK), F32)],
        compiler_params=_params("parallel"),
        name="even_proj",
    )(x, x, x, mods, nw.reshape(1, D_MODEL), w, conv_w, pad_row(a_log), pad_row(dt_bias))


def _even_weight(w):
    d = w.shape[0]
    o = 0
    rq = w[:, o:o + RET_QK]; o += RET_QK
    rk = w[:, o:o + RET_QK]; o += RET_QK
    rest = w[:, o:o + 2 * RET_V + 2 * GDN_QK + 2 * GDN_V]; o += 2 * RET_V + 2 * GDN_QK + 2 * GDN_V
    gab = w[:, o:]

    def pad_heads(m):
        m = m.reshape(d, RET_HEADS, RET_DK)
        return jnp.pad(m, ((0, 0), (0, 0), (0, HEAD_PAD - RET_DK))).reshape(d, RET_HEADS * HEAD_PAD)

    gab = jnp.pad(gab, ((0, 0), (0, LANE - gab.shape[1])))
    return jnp.concatenate([pad_heads(rq), pad_heads(rk * RET_DK ** -0.5), rest, gab], axis=1).astype(BF16)


N_HEADS = 4
STACK = N_HEADS * CHUNK


def _prep_kernel(logit_ref, rq_ref, rk_ref, rv_ref, cq_ref, ck_ref, cv_ref, gb_ref, gt_ref,
                 rin_ref, rqd_ref, rkd_ref, rvb_ref, gu_ref, gwq_ref, gat_ref, gkd_ref, gcd_ref, *, chunks):
    wr = lax.broadcasted_iota(jnp.int32, (CHUNK, STACK), 0)
    wcol = lax.broadcasted_iota(jnp.int32, (CHUNK, STACK), 1)
    wc = wcol & (CHUNK - 1)
    whead = wcol >> CHUNK_SHIFT
    r4 = lax.broadcasted_iota(jnp.int32, (STACK, STACK), 0)
    c4 = lax.broadcasted_iota(jnp.int32, (STACK, STACK), 1)
    same_head = (r4 >> CHUNK_SHIFT) == (c4 >> CHUNK_SHIFT)
    rc = lax.broadcasted_iota(jnp.int32, (CHUNK, CHUNK), 0)
    cc = lax.broadcasted_iota(jnp.int32, (CHUNK, CHUNK), 1)
    rpos = (lax.broadcasted_iota(jnp.int32, (STACK, LANE), 0) & (CHUNK - 1)).astype(F32)
    eye_w = jnp.where(wr == wc, 1.0, 0.0)

    def rows(x, h):
        return x[h * CHUNK:(h + 1) * CHUNK]

    def by_head(parts):
        out = parts[N_HEADS - 1]
        for h in range(N_HEADS - 2, -1, -1):
            out = jnp.where(whead == h, parts[h], out)
        return out

    def block_diag(w):
        return tuple(jnp.where(same_head, jnp.concatenate([x] * N_HEADS, axis=0), jnp.zeros((STACK, STACK), BF16))
                     for x in _split2(w))

    def wide_times(lhs, bh, bl):
        parts = [_split2(a) for a in lhs]
        hh = _dot(jnp.concatenate([x for hi_lo in parts for x in hi_lo], axis=0), bh)
        hl = _dot(jnp.concatenate([hi for hi, _ in parts], axis=0), bl)
        return [hh[2 * i * CHUNK:(2 * i + 1) * CHUNK] + hh[(2 * i + 1) * CHUNK:(2 * i + 2) * CHUNK]
                + hl[i * CHUNK:(i + 1) * CHUNK] for i in range(len(lhs))]

    ret_const = []
    for d in range(2):
        lg4 = jnp.concatenate(
            [-_softplus(-jnp.full((CHUNK, STACK), logit_ref[d * N_HEADS + h], F32)) for h in range(N_HEADS)], axis=0)
        tri4 = (r4 >= c4) if d == 0 else (r4 <= c4)
        dist4 = ((r4 - c4) if d == 0 else (c4 - r4)).astype(F32)
        dec4 = jnp.exp(jnp.where(same_head, jnp.where(tri4, lg4 * dist4, -jnp.inf), -jnp.inf))
        spos = rpos if d == 0 else (CHUNK - 1.0) - rpos
        lgc = lg4[:, :LANE]
        ret_const.append((dec4, jnp.exp(lgc * (spos + 1.0)), jnp.exp(lgc * ((CHUNK - 1.0) - spos))))

    setup = []
    for n in range(chunks):
        tok = slice(n * CHUNK, (n + 1) * CHUNK)

        def stack(ref):
            return jnp.concatenate([ref[tok, h * LANE:(h + 1) * LANE] for h in range(N_HEADS)], axis=0)

        q4, k4, v4 = stack(rq_ref), stack(rk_ref), stack(rv_ref)
        q4b, k4b, v4b = q4.astype(BF16), k4.astype(BF16), v4.astype(BF16)
        rvb_ref[n] = v4b
        qk_ret = _dot_nt(q4b, k4b)
        gq4, gk4, gv4 = stack(cq_ref), stack(ck_ref), stack(cv_ref)
        gq4b, gk4b = gq4.astype(BF16), gk4.astype(BF16)
        gb = gb_ref[tok, :]
        gt = gt_ref[n]
        gh, gl = _split2(gb)
        gll = (gb - gh.astype(F32) - gl.astype(F32)).astype(BF16)
        th, tl = _split2(gt)
        tll = (gt - th.astype(F32) - tl.astype(F32)).astype(BF16)

        intra = _dot(jnp.concatenate([(qk_ret * ret_const[d][0]).astype(BF16) for d in range(2)], axis=0), v4b)
        for d in range(2):
            _, qdec, kdec = ret_const[d]
            rin_ref[n, d] = intra[d * STACK:(d + 1) * STACK]
            rqd_ref[n, d] = (q4 * qdec).astype(BF16)
            rkd_ref[n, d] = (k4 * kdec).astype(BF16)
        gqk = [_dot_nt(rows(gq4b, h), rows(gk4b, h)) for h in range(N_HEADS)]

        for d in range(2):
            fwd = d == 0
            low = jnp.where((rc >= cc) if fwd else (rc <= cc), 1.0, 0.0).astype(BF16)
            upp_w = jnp.where((wr <= wc) if fwd else (wr >= wc), 1.0, 0.0).astype(BF16)
            tri_w = (wr >= wc) if fwd else (wr <= wc)
            tri_c = (rc >= cc) if fwd else (rc <= cc)
            gc_col = _dot(low, gh) + _dot(low, gl) + _dot(low, gll)
            t3 = _dot(jnp.concatenate([th, tl, tll], axis=0), upp_w)
            gc_row = t3[:16] + t3[16:32] + t3[32:]
            last_row = CHUNK - 1 if fwd else 0
            idx = [d * N_HEADS + h for h in range(N_HEADS)]
            gcol = [gc_col[:, i:i + 1] for i in idx]
            glast = [gc_col[last_row:last_row + 1, i:i + 1] for i in idx]
            beta = [gb[:, 2 * N_HEADS + i:2 * N_HEADS + i + 1] for i in idx]
            col4 = lambda parts: jnp.concatenate([jnp.broadcast_to(p, (CHUNK, 1)) for p in parts], axis=0)
            gcol4, glast4, beta4 = col4(gcol), col4(glast), col4(beta)
            eg4 = jnp.exp(gcol4)
            kb4 = gk4 * beta4
            gcol_w = by_head([jnp.broadcast_to(g, (CHUNK, STACK)) for g in gcol])
            grow_w = by_head([jnp.broadcast_to(gc_row[i:i + 1, :], (CHUNK, STACK)) for i in idx])
            dmat_w = jnp.exp(jnp.where(tri_w, gcol_w - grow_w, -jnp.inf))
            kk = _dot_nt(kb4.astype(BF16), gk4b)
            kk_w = by_head([rows(kk, h) for h in range(N_HEADS)])
            p = jnp.where(wr == wc, 0.0, -(kk_w * dmat_w))
            setup.append((n, d, p, gcol, glast, gcol4, glast4, beta4, eg4, kb4, gc_row, idx, tri_c,
                          gq4, gk4, gv4, gqk))

    ps = [s[2] for s in setup]
    tinvs = [eye_w + p for p in ps]
    ps = [wide_times([p], *block_diag(p))[0] for p in ps]
    for _ in range(4):
        new = [wide_times([p, t], *block_diag(p)) for p, t in zip(ps, tinvs)]
        ps = [n[0] for n in new]
        tinvs = [t + n[1] for t, n in zip(tinvs, new)]
    tinvs = [t + wide_times([t], *block_diag(p))[0] for p, t in zip(ps, tinvs)]
    t_bds = [block_diag(t) for t in tinvs]
    for (n, d, _, gcol, glast, gcol4, glast4, beta4, eg4, kb4, gc_row, idx, tri_c,
         gq4, gk4, gv4, gqk), (th_bd, tl_bd) in zip(setup, t_bds):
        rh, rl = _split2(jnp.concatenate([gv4 * beta4, kb4 * eg4], axis=1))
        both = _dot(jnp.concatenate([th_bd, tl_bd], axis=0), rh)
        uw = both[:STACK] + both[STACK:] + _dot(th_bd, rl)
        gu_ref[n, d] = uw[:, :GDN_DV]
        w4 = uw[:, GDN_DV:].astype(BF16)
        qd4 = (gq4 * eg4).astype(BF16)
        gkd_ref[n, d] = (gk4 * jnp.exp(glast4 - gcol4)).astype(BF16)
        for h in range(N_HEADS):
            gwq_ref[n, d, h] = jnp.concatenate([rows(w4, h), rows(qd4, h)], axis=0)
            dm = jnp.exp(jnp.where(tri_c, gcol[h] - gc_row[idx[h]:idx[h] + 1, :CHUNK], -jnp.inf))
            gat_ref[n, d, h] = (gqk[h] * dm).astype(BF16)
        gcd_ref[n, d] = jnp.concatenate(
            [jnp.broadcast_to(jnp.exp(g), (1, LANE)) for g in glast] + [jnp.zeros((SUBLANE - N_HEADS, LANE), F32)],
            axis=0)


def _scan_kernel(fblk, bblk, first, last, s0idx, soidx, logit_ref, *refs):
    del fblk, bblk, s0idx, soidx
    views = (refs[0:9], refs[9:18])
    s0r_ref, s0g_ref = refs[18], refs[19]
    o_refs = ((refs[20], refs[22]), (refs[21], refs[23]))
    sro_ref, sgo_ref = refs[24], refs[25]
    sr_ref, sg_ref = refs[26], refs[27]
    t = pl.program_id(0)

    @pl.when(first[t] == 1)
    def _():
        sr_ref[...] = s0r_ref[...]
        sg_ref[...] = s0g_ref[...]

    @pl.when(first[t] == 2)
    def _():
        sr_ref[...] = jnp.zeros(sr_ref.shape, F32)
        sg_ref[...] = jnp.zeros(sg_ref.shape, F32)

    chains = [(d, h) for d in range(2) for h in range(N_HEADS)]
    s_gdn = {c: sg_ref[c[0], c[1]] for c in chains}
    s_ret = {c: sr_ref[c[0], c[1]] for c in chains}
    ret_decay = {(d, h): jnp.exp(-_softplus(-jnp.full((1, LANE), logit_ref[d * N_HEADS + h], F32)) * float(CHUNK))
                 for d, h in chains}
    n_sub = views[0][0].shape[0]
    for sub in range(n_sub):
        at = (sub, n_sub - 1 - sub)

        def rows(d, k, h):
            return views[d][k][at[d], h * CHUNK:(h + 1) * CHUNK, :]

        ws = {(d, h): _dot(views[d][5][at[d], h], s_gdn[d, h].astype(BF16)) for d, h in chains}
        inter = {(d, h): _dot(rows(d, 1, h), s_ret[d, h].astype(BF16)) for d, h in chains}
        kv = {(d, h): _dot_tn(rows(d, 2, h), rows(d, 3, h)) for d, h in chains}
        v_new = {(d, h): (rows(d, 4, h) - ws[d, h][:CHUNK]).astype(BF16) for d, h in chains}
        intra = {(d, h): _dot(views[d][6][at[d], h], v_new[d, h]) for d, h in chains}
        kvn = {(d, h): _dot_tn(rows(d, 7, h), v_new[d, h]) for d, h in chains}
        for d, h in chains:
            tok = slice(at[d] * CHUNK, (at[d] + 1) * CHUNK)
            lanes = slice(h * LANE, (h + 1) * LANE)
            o_refs[d][0][tok, lanes] = rows(d, 0, h) + inter[d, h]
            o_refs[d][1][tok, lanes] = ws[d, h][CHUNK:] + intra[d, h]
            s_ret[d, h] = s_ret[d, h] * ret_decay[d, h] + kv[d, h]
            s_gdn[d, h] = s_gdn[d, h] * views[d][8][at[d], h:h + 1, :] + kvn[d, h]
    for d, h in chains:
        sr_ref[d, h] = s_ret[d, h]
        sg_ref[d, h] = s_gdn[d, h]

    @pl.when(last[t] == 1)
    def _():
        sro_ref[...] = sr_ref[...]
        sgo_ref[...] = sg_ref[...]


def _scan_tables(lay):
    fblk, bblk, first, last, s0idx, soidx = [], [], [], [], [], []
    blk = SCAN_CHUNKS * CHUNK
    assert lay.seq % blk == 0 and lay.dec_seq % blk == 0
    cp, cs = lay.seq // blk, lay.dec_seq // blk
    for b in range(lay.dec_batch):
        base = lay.n_p // blk + b * cs
        for s in range(cs):
            fblk.append(base + s); bblk.append(base + cs - 1 - s)
            first.append(1 if s == 0 else 0); last.append(0); s0idx.append(b); soidx.append(0)
    for b in range(lay.batch):
        base = b * cp
        for s in range(cp):
            fblk.append(base + s); bblk.append(base + cp - 1 - s)
            first.append(2 if s == 0 else 0); last.append(1 if s == cp - 1 else 0)
            s0idx.append(lay.dec_batch - 1); soidx.append(b)
    return [jnp.asarray(np.asarray(a, np.int32)) for a in (fblk, bblk, first, last, s0idx, soidx)]


def _scan_prep(lay, rq, rk, rv, cq, ck, cv, gb, gt, logit):
    n_chunks = lay.n // CHUNK
    nc = PREP_CHUNKS
    assert n_chunks % nc == 0
    width = N_HEADS * LANE

    def tok(wd):
        return pl.BlockSpec((nc * CHUNK, wd), lambda i: (i, 0))

    def per_dir(shape, dtype):
        spec = pl.BlockSpec((nc, 2) + shape, lambda i: (i,) + (0,) * (1 + len(shape)))
        return spec, jax.ShapeDtypeStruct((n_chunks, 2) + shape, dtype)

    outs = [
        per_dir((STACK, LANE), F32),
        per_dir((STACK, LANE), BF16),
        per_dir((STACK, LANE), BF16),
        (pl.BlockSpec((nc, STACK, LANE), lambda i: (i, 0, 0)), jax.ShapeDtypeStruct((n_chunks, STACK, LANE), BF16)),
        per_dir((STACK, LANE), F32),
        per_dir((N_HEADS, 2 * CHUNK, LANE), BF16),
        per_dir((N_HEADS, CHUNK, CHUNK), BF16),
        per_dir((STACK, LANE), BF16),
        per_dir((SUBLANE, LANE), F32),
    ]
    return pl.pallas_call(
        functools.partial(_prep_kernel, chunks=nc),
        grid=(n_chunks // nc,),
        in_specs=[pl.BlockSpec(memory_space=pltpu.SMEM)] + [tok(width)] * 6
                 + [tok(LANE), pl.BlockSpec((nc, 4 * GDN_HEADS, CHUNK), lambda i: (i, 0, 0))],
        out_specs=[o[0] for o in outs],
        out_shape=[o[1] for o in outs],
        compiler_params=_params("parallel"),
        name="scan_prep",
    )(logit.reshape(-1).astype(F32), rq, rk, rv, cq, ck, cv, gb, gt)


def _even_scan(lay, prep, logit, s0_ret, s0_gdn):
    tables = _scan_tables(lay)
    steps = int(tables[0].shape[0])
    width = N_HEADS * LANE

    def view(which):
        def blk(arr, with_dir=True):
            shape = arr.shape[2:] if with_dir else arr.shape[1:]
            if with_dir:
                return pl.BlockSpec((SCAN_CHUNKS, None) + shape,
                                    lambda t, *tb: (tb[which][t], which) + (0,) * len(shape))
            return pl.BlockSpec((SCAN_CHUNKS,) + shape, lambda t, *tb: (tb[which][t],) + (0,) * len(shape))
        return [blk(a, with_dir=(i != 3)) for i, a in enumerate(prep)]

    state_shape = (2, N_HEADS, LANE, LANE)
    s0_spec = pl.BlockSpec((None,) + state_shape, lambda t, *tb: (tb[4][t], 0, 0, 0, 0))
    so_spec = pl.BlockSpec((None,) + state_shape, lambda t, *tb: (tb[5][t], 0, 0, 0, 0))
    out_f = pl.BlockSpec((SCAN_CHUNKS * CHUNK, width), lambda t, *tb: (tb[0][t], 0))
    out_b = pl.BlockSpec((SCAN_CHUNKS * CHUNK, width), lambda t, *tb: (tb[1][t], 0))
    o_sds = jax.ShapeDtypeStruct((lay.n, width), F32)
    so_sds = jax.ShapeDtypeStruct((lay.batch,) + state_shape, F32)
    args = list(prep)
    return pl.pallas_call(
        _scan_kernel,
        grid_spec=pltpu.PrefetchScalarGridSpec(
            num_scalar_prefetch=6,
            grid=(steps,),
            in_specs=[pl.BlockSpec(memory_space=pltpu.SMEM)] + view(0) + view(1) + [s0_spec, s0_spec],
            out_specs=[out_f, out_b, out_f, out_b, so_spec, so_spec],
            scratch_shapes=[pltpu.VMEM(state_shape, F32), pltpu.VMEM(state_shape, F32)],
        ),
        out_shape=[o_sds, o_sds, o_sds, o_sds, so_sds, so_sds],
        compiler_params=_params("arbitrary"),
        name="even_scan",
    )(*tables, logit.reshape(-1).astype(F32), *args, *args, s0_ret, s0_gdn)


def _odd_proj_kernel(x_ref, mod_ref, nw_ref, w_ref, qnw_ref, knw_ref, cos_ref, sin_ref,
                     q_ref, kd_ref, vt_ref, ks_ref, vs_ref, *, p_tiles):
    h = _adaln(x_ref[...], nw_ref[...], mod_ref[3:4, :], mod_ref[4:5, :]).astype(BF16)
    tm = h.shape[0]
    lane = lax.broadcasted_iota(jnp.int32, (tm, LANE), 1)
    lo = lane < ATT_HD
    first16 = (lane & (2 * ROPE_PAIRS - 1)) < ROPE_PAIRS
    cos, sin = cos_ref[...], sin_ref[...]

    def norm_rope(xs, w2):
        sq = xs * xs
        ms_lo = jnp.sum(jnp.where(lo, sq, 0.0), axis=-1, keepdims=True) * (1.0 / ATT_HD)
        ms_hi = jnp.sum(jnp.where(lo, 0.0, sq), axis=-1, keepdims=True) * (1.0 / ATT_HD)
        xn = xs * jnp.where(lo, lax.rsqrt(ms_lo + EPS), lax.rsqrt(ms_hi + EPS)) * w2
        partner = jnp.where(first16, pltpu.roll(xn, LANE - ROPE_PAIRS, 1), pltpu.roll(xn, ROPE_PAIRS, 1))
        return xn * cos + partner * sin

    def both_halves(xs, j):
        swapped = pltpu.roll(xs, ATT_HD, 1)
        return jnp.where(lo, xs, swapped) if j % 2 == 0 else jnp.where(lo, swapped, xs)

    for t in range(ATT_Q // MXU_TILE):
        y = _dot(h, w_ref[:, t * MXU_TILE:(t + 1) * MXU_TILE])
        for p in (2 * t, 2 * t + 1):
            qs = norm_rope(y[:, (p % 2) * LANE:(p % 2 + 1) * LANE], qnw_ref[...])
            q_ref[:, p * LANE:(p + 1) * LANE] = (qs * Q_SCALE).astype(BF16)
    is_prompt = pl.program_id(0) < p_tiles
    assert ATT_KV == MXU_TILE
    yk = _dot(h, w_ref[:, ATT_Q:ATT_Q + ATT_KV])
    yv = _dot(h, w_ref[:, ATT_Q + ATT_KV:ATT_Q + 2 * ATT_KV])
    for p in range(ATT_KV // LANE):
        ks = norm_rope(yk[:, p * LANE:(p + 1) * LANE], knw_ref[...])
        vs = yv[:, p * LANE:(p + 1) * LANE]
        vst = vs.T.astype(BF16)
        for j in (2 * p, 2 * p + 1):
            kd_ref[j] = both_halves(ks, j).astype(BF16)
            for c in range(tm // KEY_CHUNK):
                vt_ref[j, c] = vst[(j % 2) * ATT_HD:(j % 2 + 1) * ATT_HD, c * KEY_CHUNK:(c + 1) * KEY_CHUNK]

        @pl.when(is_prompt)
        def _():
            ks_ref[:, p * LANE:(p + 1) * LANE] = ks
            vs_ref[:, p * LANE:(p + 1) * LANE] = vs


def _rope_tables(lay):
    t = jnp.arange(lay.dec_seq)
    inv = ROPE_THETA ** (-jnp.arange(ROPE_PAIRS, dtype=F32) / ROPE_PAIRS)
    ar = (t // GRID_W).astype(F32)[:, None] * inv
    ac = (t % GRID_W).astype(F32)[:, None] * inv
    cos = jnp.concatenate([jnp.cos(ar), jnp.cos(ar), jnp.cos(ac), jnp.cos(ac)], axis=1)
    sin = jnp.concatenate([-jnp.sin(ar), jnp.sin(ar), -jnp.sin(ac), jnp.sin(ac)], axis=1)
    cos = jnp.concatenate([jnp.tile(cos, (1, 2)), jnp.ones((lay.tm, LANE), F32)], axis=0)
    sin = jnp.concatenate([jnp.tile(sin, (1, 2)), jnp.zeros((lay.tm, LANE), F32)], axis=0)
    return cos, sin


def _odd_proj(lay, x, mods, nw, w, qnw, knw):
    cos, sin = _rope_tables(lay)

    def table_block(i):
        return jnp.where(i < lay.p_tiles, lay.tiles_per_seq, lax.rem(jnp.maximum(i - lay.p_tiles, 0), lay.tiles_per_seq))

    tab = pl.BlockSpec((lay.tm, LANE), lambda i: (table_block(i), 0))
    std = pl.BlockSpec((lay.tm, ATT_KV), lambda i: (jnp.minimum(i, lay.p_tiles - 1), 0))
    tm = lay.tm
    return pl.pallas_call(
        functools.partial(_odd_proj_kernel, p_tiles=lay.p_tiles),
        grid=(lay.tiles,),
        in_specs=[lay.tok(D_MODEL), lay.mod(), _resident((1, D_MODEL)), _resident(w.shape),
                  _resident((1, LANE)), _resident((1, LANE)), tab, tab],
        out_specs=[
            lay.tok(ATT_Q),
            pl.BlockSpec((ATT_KV_HEADS, tm, LANE), lambda i: (0, i, 0)),
            pl.BlockSpec((ATT_KV_HEADS, tm // KEY_CHUNK, ATT_HD, KEY_CHUNK), lambda i: (0, i, 0, 0)),
            std, std,
        ],
        out_shape=[
            jax.ShapeDtypeStruct((lay.n, ATT_Q), BF16),
            jax.ShapeDtypeStruct((ATT_KV_HEADS, lay.n, LANE), BF16),
            jax.ShapeDtypeStruct((ATT_KV_HEADS, lay.n // KEY_CHUNK, ATT_HD, KEY_CHUNK), BF16),
            jax.ShapeDtypeStruct((lay.n_p, ATT_KV), F32),
            jax.ShapeDtypeStruct((lay.n_p, ATT_KV), F32),
        ],
        compiler_params=_params("arbitrary"),
        name="odd_proj",
    )(x, mods, nw.reshape(1, D_MODEL), w, jnp.tile(qnw.reshape(1, ATT_HD), (1, 2)),
      jnp.tile(knw.reshape(1, ATT_HD), (1, 2)), cos, sin)


def _attn_kernel(*refs, cached):
    if cached:
        zero_ref, q_ref, kn_ref, vn_ref, kc_ref, vc_ref, _, o_ref, *s_scr = refs
    else:
        zero_ref, q_ref, kn_ref, vn_ref, o_ref, *s_scr = refs
    z = zero_ref[0]
    tq = MXU_TILE
    n_sub = q_ref.shape[0] // tq
    n_kv = kn_ref.shape[0]
    gw = ATT_GROUP * ATT_HD
    ck = KEY_CHUNK
    n_c = kc_ref.shape[1] // ck if cached else 0
    n_n = kn_ref.shape[1] // ck
    lo = lax.broadcasted_iota(jnp.int32, (tq, LANE), 1) < ATT_HD
    qms, kv_of = [], []
    for s in range(n_sub):
        for j in range(n_kv):
            for g in range(ATT_GROUP):
                q2 = q_ref[s * tq:(s + 1) * tq, j * gw + (g // 2) * LANE:j * gw + (g // 2 + 1) * LANE]
                qms.append(jnp.where(lo, q2, jnp.zeros_like(q2)) if g % 2 == 0
                           else jnp.where(lo, jnp.zeros_like(q2), q2))
                kv_of.append(j)
    n_units = len(qms)

    def keys(j, c):
        return kc_ref[j, c * ck:(c + 1) * ck, :] if c < n_c else kn_ref[j, (c - n_c) * ck:(c - n_c + 1) * ck, :]

    def values_t(j, c):
        return vc_ref[j, c] if c < n_c else vn_ref[j, c - n_c]

    ones_rows = jnp.where(lax.broadcasted_iota(jnp.int32, (2 * SUBLANE, ck), 0) == 0, 1.0, 0.0).astype(BF16)
    m_fin = [None] * n_units
    outs = [None] * n_units
    for ph in range(n_units + 1):
        ga = ph if ph < n_units else None
        gb = ph - 1 if ph >= 1 else None
        ma = jnp.full((1, tq), -jnp.inf, F32)
        acc = jnp.zeros((ATT_HD + 2 * SUBLANE, tq), F32)
        for c in range(n_c + n_n):
            if ga is not None:
                st = _dot_nt(keys(kv_of[ga], c), qms[ga])
                s_scr[ga % 2][z + c] = st
                ma = jnp.maximum(ma, jnp.max(st, axis=0, keepdims=True))
            if gb is not None:
                p = jnp.exp2(s_scr[gb % 2][z + c] - m_fin[gb]).astype(BF16)
                acc = acc + _dot(jnp.concatenate([values_t(kv_of[gb], c), ones_rows], axis=0), p)
        if ga is not None:
            m_fin[ga] = ma
        if gb is not None:
            outs[gb] = acc[:ATT_HD] / acc[ATT_HD:ATT_HD + 1]
    per_sub = n_kv * ATT_GROUP
    for s in range(n_sub):
        o_ref[s * tq:(s + 1) * tq, :] = jnp.concatenate(outs[s * per_sub:(s + 1) * per_sub], axis=0).T.astype(BF16)


def _attention(lay, q, kd, vt, kc, vc):
    gw = ATT_GROUP * ATT_HD
    o_sds = jax.ShapeDtypeStruct((lay.n, ATT_Q), BF16)
    ck = KEY_CHUNK
    assert lay.seq % ck == 0 and lay.dec_seq % ck == 0 and lay.past % ck == 0
    assert lay.seq % MXU_TILE == 0 and lay.dec_seq % MXU_TILE == 0 and ATT_Q_TILE % MXU_TILE == 0

    zero = jnp.zeros((1,), jnp.int32)
    zero_spec = pl.BlockSpec(memory_space=pltpu.SMEM)

    tq = min(ATT_Q_TILE, lay.seq)
    nq = lay.seq // tq
    nck = lay.seq // ck
    out = pl.pallas_call(
        functools.partial(_attn_kernel, cached=False),
        grid=(lay.batch, nq),
        in_specs=[
            zero_spec,
            pl.BlockSpec((tq, ATT_Q), lambda b, i: (b * nq + i, 0)),
            pl.BlockSpec((ATT_KV_HEADS, lay.seq, LANE), lambda b, i: (0, b, 0)),
            pl.BlockSpec((ATT_KV_HEADS, nck, ATT_HD, ck), lambda b, i: (0, b, 0, 0)),
        ],
        out_specs=pl.BlockSpec((tq, ATT_Q), lambda b, i: (b * nq + i, 0)),
        out_shape=o_sds,
        scratch_shapes=[pltpu.VMEM((nck, ck, MXU_TILE), F32)] * 2,
        compiler_params=_params("parallel", "parallel"),
        name="attn_context",
    )(zero, q, kd, vt)

    tq = min(ATT_Q_TILE, lay.dec_seq)
    nq = lay.dec_seq // tq
    nck = lay.dec_seq // ck
    qbase = lay.n_p // tq
    kbase = lay.n_p // lay.dec_seq
    assert lay.n_p % lay.dec_seq == 0
    return pl.pallas_call(
        functools.partial(_attn_kernel, cached=True),
        grid=(lay.dec_batch, ATT_KV_HEADS, nq),
        in_specs=[
            zero_spec,
            pl.BlockSpec((tq, gw), lambda b, j, i: (qbase + b * nq + i, j)),
            pl.BlockSpec((1, lay.dec_seq, LANE), lambda b, j, i: (j, kbase + b, 0)),
            pl.BlockSpec((1, nck, ATT_HD, ck), lambda b, j, i: (j, kbase + b, 0, 0)),
            pl.BlockSpec((None, 1, lay.past, LANE), lambda b, j, i: (b, j, 0, 0)),
            pl.BlockSpec((None, 1, lay.past // ck, ATT_HD, ck), lambda b, j, i: (b, j, 0, 0, 0)),
            pl.BlockSpec(memory_space=pl.ANY),
        ],
        out_specs=pl.BlockSpec((tq, gw), lambda b, j, i: (qbase + b * nq + i, j)),
        out_shape=o_sds,
        scratch_shapes=[pltpu.VMEM((lay.past // ck + nck, ck, MXU_TILE), F32)] * 2,
        input_output_aliases={6: 0},
        compiler_params=_params("parallel", "parallel", "parallel"),
        name="attn_latent",
    )(zero, q, kd, vt, kc, vc, out)


def _pad_state(s):
    return jnp.pad(s, [(0, 0)] * (s.ndim - 2) + [(0, LANE - s.shape[-2]), (0, 0)])


def _cache_keys(c):
    return jnp.tile(c.transpose(0, 2, 1, 3), (1, 1, 1, 2)).astype(BF16)


def _cache_values(c):
    b, t, kvh, hd = c.shape
    return c.reshape(b, t // KEY_CHUNK, KEY_CHUNK, kvh, hd).transpose(0, 3, 1, 4, 2).astype(BF16)


def kernel(x_prompt, x_sample, state_ret, state_gdn, cache_k, cache_v, c, c_ctx,
           mod_w, mod_b, norm_w, ffn_w_in, ffn_w_out, even_w_in, even_w_out,
           ret_decay_logit, ret_norm_w, gdn_conv_w, gdn_A_log, gdn_dt_bias, gdn_norm_w,
           odd_w_in, odd_w_out, q_norm_w, k_norm_w, final_norm_w):
    batch, seq, d = x_prompt.shape
    dec_batch, dec_seq, _ = x_sample.shape
    depth = mod_w.shape[0]
    lay = _Layout(batch, seq, dec_batch, dec_seq, cache_k.shape[2])
    lay_ffn = _Layout(batch, seq, dec_batch, dec_seq, cache_k.shape[2], tile=FFN_TILE)

    x = (x_prompt.reshape(lay.n_p, d), x_sample.reshape(lay.n_s, d))
    n_cond = -(-(1 + dec_batch) // (2 * SUBLANE)) * (2 * SUBLANE)
    cond = jnp.zeros((n_cond, d), F32).at[0].set(c_ctx).at[1:1 + dec_batch].set(c)
    mods = _modulation(cond, mod_w, mod_b)
    ffn_in, ffn_out = ffn_w_in.astype(BF16), ffn_w_out.astype(BF16)

    new_ret, new_gdn, new_k, new_v = [], [], [], []
    for l in range(depth):
        m = mods[l]
        last = l == depth - 1
        x = _ffn(lay_ffn, x, m, norm_w[l, 0], ffn_in, ffn_out, (l, 0), 0)
        if l % 2 == 0:
            e = l // 2
            rq, rk, rv, rg, gg, cq, ck, cv, gb, gt = _even_proj(
                lay, x, m, norm_w[l, 1], _even_weight(even_w_in[e]), gdn_conv_w[e], gdn_A_log[e], gdn_dt_bias[e])
            prep = _scan_prep(lay, rq, rk, rv, cq, ck, cv, gb, gt, ret_decay_logit[e])
            orf, orb, ogf, ogb, sr, sg = _even_scan(
                lay, prep, ret_decay_logit[e], _pad_state(state_ret[:, e]), state_gdn[:, e])
            new_ret.append(sr[:, :, :, :RET_DK, :])
            new_gdn.append(sg)
            mixer = "even"
            mixer_ins = (orf, orb, ogf, ogb, rg, gg, ret_norm_w[e].reshape(1, LANE), gdn_norm_w[e].reshape(1, LANE),
                         even_w_out[e].astype(BF16))
        else:
            o = l // 2
            q, kd, vt, ks, vs = _odd_proj(lay, x, m, norm_w[l, 1], odd_w_in[o].astype(BF16), q_norm_w[o], k_norm_w[o])
            a = _attention(lay, q, kd, vt, _cache_keys(cache_k[:, o]), _cache_values(cache_v[:, o]))
            new_k.append(ks.reshape(batch, seq, ATT_KV_HEADS, ATT_HD))
            new_v.append(vs.reshape(batch, seq, ATT_KV_HEADS, ATT_HD))
            mixer = "odd"
            mixer_ins = (a, odd_w_out[o].astype(BF16))
        x = _ffn(lay, x, m, norm_w[l, 2], ffn_in, ffn_out, (l, 1), 2,
                 final_w=final_norm_w if last else None, mixer=mixer, mixer_ins=mixer_ins)

    y_prompt = x[0].reshape(batch, seq, d)
    y_sample = x[1].reshape(dec_batch, dec_seq, d)
    return (y_prompt, y_sample, jnp.stack(new_ret, axis=1), jnp.stack(new_gdn, axis=1),
            jnp.stack(new_k, axis=1), jnp.stack(new_v, axis=1))
```

```python
import functools

import numpy as np
import jax
import jax.numpy as jnp
from jax import lax
from jax.experimental import pallas as pl
from jax.experimental.pallas import tpu as pltpu

F32 = jnp.float32
BF16 = jnp.bfloat16

D_MODEL = 1024
GRID_W = 64
RET_HEADS = 4
RET_DK = 64
RET_DV = 128
GDN_HEADS = 4
GDN_DK = 128
GDN_DV = 128
CHUNK = 64
ATT_HEADS = 16
ATT_KV_HEADS = 4
ATT_HD = 64
ATT_GROUP = ATT_HEADS // ATT_KV_HEADS
ROPE_THETA = 10000.0
ROPE_PAIRS = ATT_HD // 4
FFN_HIDDEN = 2816
N_MOD = 9
EPS = 1e-6

RET_QK = RET_HEADS * RET_DK
RET_V = RET_HEADS * RET_DV
GDN_QK = GDN_HEADS * GDN_DK
GDN_V = GDN_HEADS * GDN_DV
ATT_Q = ATT_HEADS * ATT_HD
ATT_KV = ATT_KV_HEADS * ATT_HD

LANE = 128
SUBLANE = 8
V7X_VMEM_BYTES = 64 * 1024 * 1024
VMEM_COMPILER_RESERVE = 8 * 1024 * 1024
VMEM_LIMIT = V7X_VMEM_BYTES - VMEM_COMPILER_RESERVE

CHUNK_SHIFT = CHUNK.bit_length() - 1
assert 1 << CHUNK_SHIFT == CHUNK
MOD_COL_TILES = 8
TOKEN_TILE = 512
FFN_TILE = 1024
ATT_Q_TILE = 1024
KEY_CHUNK = 256
Q_SCALE = ATT_HD ** -0.5 * float(np.log2(np.e))
MXU_TILE = 256
FFN_CHUNKS = ((0, 6 * MXU_TILE), (6 * MXU_TILE, FFN_HIDDEN))
HEAD_PAD = LANE
PREP_CHUNKS = 4
SCAN_CHUNKS = 4

def _params(*sem):
    return pltpu.CompilerParams(dimension_semantics=sem, vmem_limit_bytes=VMEM_LIMIT)


def _resident(shape):
    nd = len(shape)
    return pl.BlockSpec(shape, lambda *_: (0,) * nd, pipeline_mode=pl.Buffered(1))


def _resident_at(arr, lead):
    shape = arr.shape[len(lead):]
    return pl.BlockSpec((None,) * len(lead) + shape, lambda *_: tuple(lead) + (0,) * len(shape),
                        pipeline_mode=pl.Buffered(1))


def _dot(a, b):
    return jnp.dot(a, b, preferred_element_type=F32)


def _dot_nt(a, b):
    return lax.dot_general(a, b, (((1,), (1,)), ((), ())), preferred_element_type=F32)


def _dot_tn(a, b):
    return lax.dot_general(a, b, (((0,), (0,)), ((), ())), preferred_element_type=F32)


def _split2(a):
    hi = a.astype(BF16)
    lo = (a - hi.astype(F32)).astype(BF16)
    return hi, lo


def _silu(x):
    return x * jax.nn.sigmoid(x)


def _softplus(x):
    return jnp.maximum(x, 0.0) + jnp.log(1.0 + jnp.exp(-jnp.abs(x)))


def _adaln(x, nw, shift, scale):
    ms = jnp.mean(x * x, axis=-1, keepdims=True)
    return (x * lax.rsqrt(ms + EPS)) * nw * (1.0 + scale) + shift


def _head_rmsnorm(x, w, n_heads):
    parts = []
    for h in range(n_heads):
        xs = x[:, h * LANE:(h + 1) * LANE]
        ms = jnp.mean(xs * xs, axis=-1, keepdims=True)
        parts.append(xs * lax.rsqrt(ms + EPS) * w)
    return jnp.concatenate(parts, axis=1)


def _mod_kernel(cond_ref, w_ref, b_ref, o_ref):
    c = cond_ref[...]
    o_ref[0] = _dot(_silu(c).astype(BF16), w_ref[0].astype(BF16)) + b_ref[0]


def _modulation(cond, mod_w, mod_b):
    depth, d, n = mod_w.shape
    r = cond.shape[0]
    tn = n // MOD_COL_TILES
    out = pl.pallas_call(
        _mod_kernel,
        grid=(depth, n // tn),
        in_specs=[
            pl.BlockSpec((r, d), lambda l, j: (0, 0)),
            pl.BlockSpec((1, d, tn), lambda l, j: (l, 0, j)),
            pl.BlockSpec((1, 1, tn), lambda l, j: (l, 0, j)),
        ],
        out_specs=pl.BlockSpec((1, r, tn), lambda l, j: (l, 0, j)),
        out_shape=jax.ShapeDtypeStruct((depth, r, n), F32),
        compiler_params=_params("parallel", "parallel"),
        name="modulation",
    )(cond, mod_w, mod_b.reshape(depth, 1, n))
    return out.reshape(depth, r, N_MOD, d)


class _Layout:
    def __init__(self, batch, seq, dec_batch, dec_seq, past, tile=TOKEN_TILE):
        self.batch, self.seq, self.dec_batch, self.dec_seq, self.past = batch, seq, dec_batch, dec_seq, past
        self.n_p = batch * seq
        self.n_s = dec_batch * dec_seq
        self.n = self.n_p + self.n_s
        self.tm = min(tile, self.n_p, dec_seq)
        assert self.n_p % self.tm == 0 and dec_seq % self.tm == 0
        self.tiles = self.n // self.tm
        self.p_tiles = self.n_p // self.tm
        self.tiles_per_seq = dec_seq // self.tm

    def group(self, i):
        return jnp.where(i < self.p_tiles, 0, 1 + (i - self.p_tiles) // self.tiles_per_seq)

    def tok(self, width, tm=None):
        tm = tm or self.tm
        return pl.BlockSpec((tm, width), lambda i: (i, 0))

    def mod(self):
        return pl.BlockSpec((None, N_MOD, D_MODEL), lambda i: (self.group(i), 0, 0))


def _ffn_kernel(*refs, j, first, final, p_tiles, mixer):
    refs = list(refs)
    x_refs = [refs.pop(0) for _ in range(2 if first else 1)]
    n_mix = {None: 0, "odd": 2, "even": 11}[mixer]
    mix_refs = [refs.pop(0) for _ in range(n_mix)]
    mod_ref, nw_ref, win_ref, wout_ref = (refs.pop(0) for _ in range(4))
    fw_ref = refs.pop(0) if final else None
    o_refs = refs
    is_prompt = pl.program_id(0) < p_tiles
    x = jnp.where(is_prompt, x_refs[0][...], x_refs[1][...]) if first else x_refs[0][...]
    if mixer == "odd":
        a_ref, wo_ref = mix_refs
        x = x + mod_ref[5:6, :] * _dot(a_ref[...], wo_ref[...])
    elif mixer == "even":
        rf_ref, rb_ref, gf_ref, gb_ref, rg_ref, gg_ref, rif_ref, rib_ref, rnw_ref, gnw_ref, wo_ref = mix_refs
        ret_o = (rf_ref[...] + rif_ref[...]) + (rb_ref[...] + rib_ref[...])
        ret = _head_rmsnorm(ret_o, rnw_ref[...], RET_HEADS) * _silu(rg_ref[...])
        gdn = _head_rmsnorm(gf_ref[...] + gb_ref[...], gnw_ref[...], GDN_HEADS) * _silu(gg_ref[...])
        x = x + mod_ref[5:6, :] * _dot(jnp.concatenate([ret, gdn], axis=1).astype(BF16), wo_ref[...])
    h = _adaln(x, nw_ref[...], mod_ref[3 * j:3 * j + 1, :], mod_ref[3 * j + 1:3 * j + 2, :]).astype(BF16)
    acc = None
    for lo, hi in FFN_CHUNKS:
        a = _dot(h, win_ref[:, lo:hi])
        b = _dot(h, win_ref[:, FFN_HIDDEN + lo:FFN_HIDDEN + hi])
        y = _dot((_silu(a) * b).astype(BF16), wout_ref[lo:hi, :])
        acc = y if acc is None else acc + y
    out = x + (0.5 * mod_ref[3 * j + 2:3 * j + 3, :]) * acc
    if not final:
        o_refs[0][...] = out
        return
    ms = jnp.mean(out * out, axis=-1, keepdims=True)
    out = out * lax.rsqrt(ms + EPS) * fw_ref[...]

    @pl.when(is_prompt)
    def _():
        o_refs[0][...] = out

    @pl.when(jnp.logical_not(is_prompt))
    def _():
        o_refs[1][...] = out


def _ffn(lay, xs, mods, nw, w_in, w_out, which, j, final_w=None, mixer=None, mixer_ins=()):
    first = isinstance(xs, tuple)
    final = final_w is not None
    prompt_rows = pl.BlockSpec((lay.tm, D_MODEL), lambda i: (jnp.minimum(i, lay.p_tiles - 1), 0))
    latent_rows = pl.BlockSpec((lay.tm, D_MODEL), lambda i: (jnp.maximum(i - lay.p_tiles, 0), 0))
    ins = list(xs) if first else [xs]
    specs = [prompt_rows, latent_rows] if first else [lay.tok(D_MODEL)]
    for arr in mixer_ins:
        if isinstance(arr, tuple):
            arr, lead = arr
            ins.append(arr)
            specs.append(pl.BlockSpec((None, lay.tm, arr.shape[2]), lambda i, lead=lead: (lead, i, 0)))
            continue
        ins.append(arr)
        specs.append(lay.tok(arr.shape[1]) if arr.shape[0] == lay.n else _resident(arr.shape))
    ins += [mods, nw.reshape(1, D_MODEL), w_in, w_out]
    specs += [lay.mod(), _resident((1, D_MODEL)), _resident_at(w_in, which), _resident_at(w_out, which)]
    if final:
        ins.append(final_w.reshape(1, D_MODEL))
        specs.append(_resident((1, D_MODEL)))
        out_specs = [prompt_rows, latent_rows]
        out_shape = [jax.ShapeDtypeStruct((lay.n_p, D_MODEL), F32), jax.ShapeDtypeStruct((lay.n_s, D_MODEL), F32)]
    else:
        out_specs = lay.tok(D_MODEL)
        out_shape = jax.ShapeDtypeStruct((lay.n, D_MODEL), F32)
    return pl.pallas_call(
        functools.partial(_ffn_kernel, j=j, first=first, final=final, p_tiles=lay.p_tiles, mixer=mixer),
        grid=(lay.tiles,),
        in_specs=specs,
        out_specs=out_specs,
        out_shape=out_shape,
        compiler_params=_params("arbitrary"),
        name=("ffn_final" if final else ("ffn_first" if first else "ffn")) + ("_" + mixer if mixer else ""),
    )(*ins)


EVEN_WIDTHS = (RET_HEADS * HEAD_PAD, RET_HEADS * HEAD_PAD, RET_V, RET_V, 2 * GDN_QK + GDN_V, GDN_V, LANE)


def _even_proj_kernel(x_ref, prev_ref, next_ref, mod_ref, nw_ref, w_ref, cw_ref, alog_ref, dt_ref,
                      rq_ref, rk_ref, rv_ref, rg_ref, gg_ref, cq_ref, ck_ref, cv_ref, gb_ref, gt_ref,
                      *, n_p, seq, dec_seq):
    tm = x_ref.shape[0]
    shift, scale = mod_ref[3:4, :], mod_ref[4:5, :]
    h = _adaln(x_ref[...], nw_ref[...], shift, scale).astype(BF16)
    halo = jnp.concatenate([prev_ref[...], next_ref[...]], axis=0)
    hh = _adaln(halo, nw_ref[...], shift, scale).astype(BF16)
    offs = np.cumsum((0,) + EVEN_WIDTHS)
    col = lambda i: slice(int(offs[i]), int(offs[i + 1]))

    start = pl.program_id(0) * tm
    in_prompt = start < n_p
    rel = jnp.where(in_prompt, start, start - n_p)
    slen = jnp.where(in_prompt, seq, dec_seq)
    row = lax.broadcasted_iota(jnp.int32, (tm, GDN_QK), 0)
    pos = (rel + row) & (slen - 1)

    def conv(part):
        lanes = slice(int(offs[4]) + part * GDN_QK, int(offs[4]) + (part + 1) * GDN_QK)
        cw = cw_ref[:, part * GDN_QK:(part + 1) * GDN_QK]
        y = _dot(h, w_ref[:, lanes])
        yh = _dot(hh, w_ref[:, lanes])
        ym1 = jnp.where(pos == 0, 0.0, jnp.where(row == 0, yh[SUBLANE - 1:SUBLANE, :], pltpu.roll(y, 1, 0)))
        yp1 = jnp.where(pos == slen - 1, 0.0,
                        jnp.where(row == tm - 1, yh[SUBLANE:SUBLANE + 1, :], pltpu.roll(y, tm - 1, 0)))
        return _silu(ym1 * cw[0:1, :] + y * cw[1:2, :] + yp1 * cw[2:3, :])

    def l2norm_heads(y, o_ref, scale):
        for hd in range(GDN_HEADS):
            v = y[:, hd * LANE:(hd + 1) * LANE]
            o_ref[:, hd * LANE:(hd + 1) * LANE] = v * lax.rsqrt(jnp.sum(v * v, axis=-1, keepdims=True) + EPS) * scale

    l2norm_heads(conv(0), cq_ref, GDN_DK ** -0.5)
    rq_ref[...] = _dot(h, w_ref[:, col(0)])
    rk_ref[...] = _dot(h, w_ref[:, col(1)])
    l2norm_heads(conv(1), ck_ref, 1.0)
    rv_ref[...] = _dot(h, w_ref[:, col(2)])
    rg_ref[...] = _dot(h, w_ref[:, col(3)])
    cv_ref[...] = conv(2)
    gg_ref[...] = _dot(h, w_ref[:, col(5)])
    g = _dot(h, w_ref[:, col(6)])
    lane = lax.broadcasted_iota(jnp.int32, g.shape, 1)
    gb = jnp.where(lane < 2 * GDN_HEADS, -jnp.exp(alog_ref[...]) * _softplus(g + dt_ref[...]), jax.nn.sigmoid(g))
    gb_ref[...] = gb
    gbt = gb.T
    for c in range(tm // CHUNK):
        gt_ref[c] = gbt[0:4 * GDN_HEADS, c * CHUNK:(c + 1) * CHUNK]


def _even_proj(lay, x, mods, nw, w, conv_w, a_log, dt_bias):
    tm = lay.tm
    assert lay.seq & (lay.seq - 1) == 0 and lay.dec_seq & (lay.dec_seq - 1) == 0 and tm % LANE == 0
    nblk8 = lay.n // SUBLANE
    r = tm // SUBLANE
    width = GDN_V
    assert all(wd == width for wd in EVEN_WIDTHS[:4] + EVEN_WIDTHS[5:6]) and GDN_QK == width

    def pad_row(v):
        v = v.reshape(1, -1).astype(F32)
        return jnp.pad(v, ((0, 0), (0, LANE - v.shape[1])))

    tok_out = [(width, F32)] * 8 + [(LANE, F32)]
    return pl.pallas_call(
        functools.partial(_even_proj_kernel, n_p=lay.n_p, seq=lay.seq, dec_seq=lay.dec_seq),
        grid=(lay.tiles,),
        in_specs=[
            lay.tok(D_MODEL),
            pl.BlockSpec((SUBLANE, D_MODEL), lambda i: (jnp.maximum(i * r - 1, 0), 0)),
            pl.BlockSpec((SUBLANE, D_MODEL), lambda i: (jnp.minimum((i + 1) * r, nblk8 - 1), 0)),
            lay.mod(), _resident((1, D_MODEL)), _resident(w.shape), _resident(conv_w.shape),
            _resident((1, LANE)), _resident((1, LANE)),
        ],
        out_specs=[lay.tok(wd) for wd, _ in tok_out]
                  + [pl.BlockSpec((tm // CHUNK, 4 * GDN_HEADS, CHUNK), lambda i: (i, 0, 0))],
        out_shape=[jax.ShapeDtypeStruct((lay.n, wd), dt) for wd, dt in tok_out]
                  + [jax.ShapeDtypeStruct((lay.n // CHUNK, 4 * GDN_HEADS, CHUNK), F32)],
        compiler_params=_params("parallel"),
        name="even_proj",
    )(x, x, x, mods, nw.reshape(1, D_MODEL), w, conv_w, pad_row(a_log), pad_row(dt_bias))


def _even_weight(w):
    d = w.shape[0]
    o = 0
    rq = w[:, o:o + RET_QK]; o += RET_QK
    rk = w[:, o:o + RET_QK]; o += RET_QK
    rest = w[:, o:o + 2 * RET_V + 2 * GDN_QK + 2 * GDN_V]; o += 2 * RET_V + 2 * GDN_QK + 2 * GDN_V
    gab = w[:, o:]

    def pad_heads(m):
        m = m.reshape(d, RET_HEADS, RET_DK)
        return jnp.pad(m, ((0, 0), (0, 0), (0, HEAD_PAD - RET_DK))).reshape(d, RET_HEADS * HEAD_PAD)

    gab = jnp.pad(gab, ((0, 0), (0, LANE - gab.shape[1])))
    return jnp.concatenate([pad_heads(rq), pad_heads(rk * RET_DK ** -0.5), rest, gab], axis=1).astype(BF16)


N_HEADS = 4
STACK = N_HEADS * CHUNK


def _prep_kernel(logit_ref, rq_ref, rk_ref, rv_ref, cq_ref, ck_ref, cv_ref, gb_ref, gt_ref,
                 rin_ref, rqd_ref, rkd_ref, rvb_ref, gu_ref, gwq_ref, gat_ref, gkd_ref, gcd_ref, *, chunks):
    wr = lax.broadcasted_iota(jnp.int32, (CHUNK, STACK), 0)
    wcol = lax.broadcasted_iota(jnp.int32, (CHUNK, STACK), 1)
    wc = wcol & (CHUNK - 1)
    whead = wcol >> CHUNK_SHIFT
    r4 = lax.broadcasted_iota(jnp.int32, (STACK, STACK), 0)
    c4 = lax.broadcasted_iota(jnp.int32, (STACK, STACK), 1)
    same_head = (r4 >> CHUNK_SHIFT) == (c4 >> CHUNK_SHIFT)
    rc = lax.broadcasted_iota(jnp.int32, (CHUNK, CHUNK), 0)
    cc = lax.broadcasted_iota(jnp.int32, (CHUNK, CHUNK), 1)
    rpos = (lax.broadcasted_iota(jnp.int32, (STACK, LANE), 0) & (CHUNK - 1)).astype(F32)
    eye_w = jnp.where(wr == wc, 1.0, 0.0)

    def rows(x, h):
        return x[h * CHUNK:(h + 1) * CHUNK]

    def by_head(parts):
        out = parts[N_HEADS - 1]
        for h in range(N_HEADS - 2, -1, -1):
            out = jnp.where(whead == h, parts[h], out)
        return out

    def block_diag(w):
        return tuple(jnp.where(same_head, jnp.concatenate([x] * N_HEADS, axis=0), jnp.zeros((STACK, STACK), BF16))
                     for x in _split2(w))

    def wide_times(lhs, bh, bl):
        parts = [_split2(a) for a in lhs]
        hh = _dot(jnp.concatenate([x for hi_lo in parts for x in hi_lo], axis=0), bh)
        hl = _dot(jnp.concatenate([hi for hi, _ in parts], axis=0), bl)
        return [hh[2 * i * CHUNK:(2 * i + 1) * CHUNK] + hh[(2 * i + 1) * CHUNK:(2 * i + 2) * CHUNK]
                + hl[i * CHUNK:(i + 1) * CHUNK] for i in range(len(lhs))]

    ret_const = []
    for d in range(2):
        lg4 = jnp.concatenate(
            [-_softplus(-jnp.full((CHUNK, STACK), logit_ref[d * N_HEADS + h], F32)) for h in range(N_HEADS)], axis=0)
        tri4 = (r4 >= c4) if d == 0 else (r4 <= c4)
        dist4 = ((r4 - c4) if d == 0 else (c4 - r4)).astype(F32)
        dec4 = jnp.exp(jnp.where(same_head, jnp.where(tri4, lg4 * dist4, -jnp.inf), -jnp.inf))
        spos = rpos if d == 0 else (CHUNK - 1.0) - rpos
        lgc = lg4[:, :LANE]
        ret_const.append((dec4, jnp.exp(lgc * (spos + 1.0)), jnp.exp(lgc * ((CHUNK - 1.0) - spos))))

    setup = []
    for n in range(chunks):
        tok = slice(n * CHUNK, (n + 1) * CHUNK)

        def stack(ref):
            return jnp.concatenate([ref[tok, h * LANE:(h + 1) * LANE] for h in range(N_HEADS)], axis=0)

        q4, k4, v4 = stack(rq_ref), stack(rk_ref), stack(rv_ref)
        q4b, k4b, v4b = q4.astype(BF16), k4.astype(BF16), v4.astype(BF16)
        rvb_ref[n] = v4b
        qk_ret = _dot_nt(q4b, k4b)
        gq4, gk4, gv4 = stack(cq_ref), stack(ck_ref), stack(cv_ref)
        gq4b, gk4b = gq4.astype(BF16), gk4.astype(BF16)
        gb = gb_ref[tok, :]
        gt = gt_ref[n]
        gh, gl = _split2(gb)
        gll = (gb - gh.astype(F32) - gl.astype(F32)).astype(BF16)
        th, tl = _split2(gt)
        tll = (gt - th.astype(F32) - tl.astype(F32)).astype(BF16)

        intra = _dot(jnp.concatenate([(qk_ret * ret_const[d][0]).astype(BF16) for d in range(2)], axis=0), v4b)
        for d in range(2):
            _, qdec, kdec = ret_const[d]
            for h in range(N_HEADS):
                rin_ref[d, n * CHUNK:(n + 1) * CHUNK, h * LANE:(h + 1) * LANE] = (
                    intra[d * STACK + h * CHUNK:d * STACK + (h + 1) * CHUNK])
            rqd_ref[n, d] = (q4 * qdec).astype(BF16)
            rkd_ref[n, d] = (k4 * kdec).astype(BF16)
        gqk = [_dot_nt(rows(gq4b, h), rows(gk4b, h)) for h in range(N_HEADS)]

        for d in range(2):
            fwd = d == 0
            low = jnp.where((rc >= cc) if fwd else (rc <= cc), 1.0, 0.0).astype(BF16)
            upp_w = jnp.where((wr <= wc) if fwd else (wr >= wc), 1.0, 0.0).astype(BF16)
            tri_w = (wr >= wc) if fwd else (wr <= wc)
            tri_c = (rc >= cc) if fwd else (rc <= cc)
            gc_col = _dot(low, gh) + _dot(low, gl) + _dot(low, gll)
            t3 = _dot(jnp.concatenate([th, tl, tll], axis=0), upp_w)
            gc_row = t3[:16] + t3[16:32] + t3[32:]
            last_row = CHUNK - 1 if fwd else 0
            idx = [d * N_HEADS + h for h in range(N_HEADS)]
            gcol = [gc_col[:, i:i + 1] for i in idx]
            glast = [gc_col[last_row:last_row + 1, i:i + 1] for i in idx]
            beta = [gb[:, 2 * N_HEADS + i:2 * N_HEADS + i + 1] for i in idx]
            col4 = lambda parts: jnp.concatenate([jnp.broadcast_to(p, (CHUNK, 1)) for p in parts], axis=0)
            gcol4, glast4, beta4 = col4(gcol), col4(glast), col4(beta)
            eg4 = jnp.exp(gcol4)
            kb4 = gk4 * beta4
            gcol_w = by_head([jnp.broadcast_to(g, (CHUNK, STACK)) for g in gcol])
            grow_w = by_head([jnp.broadcast_to(gc_row[i:i + 1, :], (CHUNK, STACK)) for i in idx])
            dmat_w = jnp.exp(jnp.where(tri_w, gcol_w - grow_w, -jnp.inf))
            kk = _dot_nt(kb4.astype(BF16), gk4b)
            kk_w = by_head([rows(kk, h) for h in range(N_HEADS)])
            p = jnp.where(wr == wc, 0.0, -(kk_w * dmat_w))
            setup.append((n, d, p, gcol, glast, gcol4, glast4, beta4, eg4, kb4, gc_row, idx, tri_c,
                          gq4, gk4, gv4, gqk))

    ps = [s[2] for s in setup]
    tinvs = [eye_w + p for p in ps]
    ps = [wide_times([p], *block_diag(p))[0] for p in ps]
    for _ in range(4):
        new = [wide_times([p, t], *block_diag(p)) for p, t in zip(ps, tinvs)]
        ps = [n[0] for n in new]
        tinvs = [t + n[1] for t, n in zip(tinvs, new)]
    tinvs = [t + wide_times([t], *block_diag(p))[0] for p, t in zip(ps, tinvs)]
    t_bds = [block_diag(t) for t in tinvs]
    for (n, d, _, gcol, glast, gcol4, glast4, beta4, eg4, kb4, gc_row, idx, tri_c,
         gq4, gk4, gv4, gqk), (th_bd, tl_bd) in zip(setup, t_bds):
        rh, rl = _split2(jnp.concatenate([gv4 * beta4, kb4 * eg4], axis=1))
        both = _dot(jnp.concatenate([th_bd, tl_bd], axis=0), rh)
        uw = both[:STACK] + both[STACK:] + _dot(th_bd, rl)
        gu_ref[n, d] = uw[:, :GDN_DV]
        w4 = uw[:, GDN_DV:].astype(BF16)
        qd4 = (gq4 * eg4).astype(BF16)
        gkd_ref[n, d] = (gk4 * jnp.exp(glast4 - gcol4)).astype(BF16)
        for h in range(N_HEADS):
            gwq_ref[n, d, h] = jnp.concatenate([rows(w4, h), rows(qd4, h)], axis=0)
            dm = jnp.exp(jnp.where(tri_c, gcol[h] - gc_row[idx[h]:idx[h] + 1, :CHUNK], -jnp.inf))
            gat_ref[n, d, h] = (gqk[h] * dm).astype(BF16)
        gcd_ref[n, d] = jnp.concatenate(
            [jnp.broadcast_to(jnp.exp(g), (1, LANE)) for g in glast] + [jnp.zeros((SUBLANE - N_HEADS, LANE), F32)],
            axis=0)


def _scan_kernel(fblk, bblk, first, last, s0idx, soidx, logit_ref, *refs):
    del fblk, bblk, s0idx, soidx
    views = (refs[0:8], refs[8:16])
    s0r_ref, s0g_ref = refs[16], refs[17]
    o_refs = ((refs[18], refs[20]), (refs[19], refs[21]))
    sro_ref, sgo_ref = refs[22], refs[23]
    sr_ref, sg_ref = refs[24], refs[25]
    t = pl.program_id(0)

    @pl.when(first[t] == 1)
    def _():
        sr_ref[...] = s0r_ref[...]
        sg_ref[...] = s0g_ref[...]

    @pl.when(first[t] == 2)
    def _():
        sr_ref[...] = jnp.zeros(sr_ref.shape, F32)
        sg_ref[...] = jnp.zeros(sg_ref.shape, F32)

    chains = [(d, h) for d in range(2) for h in range(N_HEADS)]
    s_gdn = {c: sg_ref[c[0], c[1]] for c in chains}
    s_ret = {c: sr_ref[c[0], c[1]] for c in chains}
    ret_decay = {(d, h): jnp.exp(-_softplus(-jnp.full((1, LANE), logit_ref[d * N_HEADS + h], F32)) * float(CHUNK))
                 for d, h in chains}
    n_sub = views[0][0].shape[0]
    for sub in range(n_sub):
        at = (sub, n_sub - 1 - sub)

        def rows(d, k, h):
            return views[d][k][at[d], h * CHUNK:(h + 1) * CHUNK, :]

        ws = {(d, h): _dot(views[d][4][at[d], h], s_gdn[d, h].astype(BF16)) for d, h in chains}
        inter = {(d, h): _dot(rows(d, 0, h), s_ret[d, h].astype(BF16)) for d, h in chains}
        kv = {(d, h): _dot_tn(rows(d, 1, h), rows(d, 2, h)) for d, h in chains}
        v_new = {(d, h): (rows(d, 3, h) - ws[d, h][:CHUNK]).astype(BF16) for d, h in chains}
        intra = {(d, h): _dot(views[d][5][at[d], h], v_new[d, h]) for d, h in chains}
        kvn = {(d, h): _dot_tn(rows(d, 6, h), v_new[d, h]) for d, h in chains}
        for d, h in chains:
            tok = slice(at[d] * CHUNK, (at[d] + 1) * CHUNK)
            lanes = slice(h * LANE, (h + 1) * LANE)
            o_refs[d][0][tok, lanes] = inter[d, h]
            o_refs[d][1][tok, lanes] = ws[d, h][CHUNK:] + intra[d, h]
            s_ret[d, h] = s_ret[d, h] * ret_decay[d, h] + kv[d, h]
            s_gdn[d, h] = s_gdn[d, h] * views[d][7][at[d], h:h + 1, :] + kvn[d, h]
    for d, h in chains:
        sr_ref[d, h] = s_ret[d, h]
        sg_ref[d, h] = s_gdn[d, h]

    @pl.when(last[t] == 1)
    def _():
        sro_ref[...] = sr_ref[...]
        sgo_ref[...] = sg_ref[...]


def _scan_tables(lay):
    fblk, bblk, first, last, s0idx, soidx = [], [], [], [], [], []
    blk = SCAN_CHUNKS * CHUNK
    assert lay.seq % blk == 0 and lay.dec_seq % blk == 0
    cp, cs = lay.seq // blk, lay.dec_seq // blk
    for b in range(lay.dec_batch):
        base = lay.n_p // blk + b * cs
        for s in range(cs):
            fblk.append(base + s); bblk.append(base + cs - 1 - s)
            first.append(1 if s == 0 else 0); last.append(0); s0idx.append(b); soidx.append(0)
    for b in range(lay.batch):
        base = b * cp
        for s in range(cp):
            fblk.append(base + s); bblk.append(base + cp - 1 - s)
            first.append(2 if s == 0 else 0); last.append(1 if s == cp - 1 else 0)
            s0idx.append(lay.dec_batch - 1); soidx.append(b)
    return [jnp.asarray(np.asarray(a, np.int32)) for a in (fblk, bblk, first, last, s0idx, soidx)]


def _scan_prep(lay, rq, rk, rv, cq, ck, cv, gb, gt, logit):
    n_chunks = lay.n // CHUNK
    nc = PREP_CHUNKS
    assert n_chunks % nc == 0
    width = N_HEADS * LANE

    def tok(wd):
        return pl.BlockSpec((nc * CHUNK, wd), lambda i: (i, 0))

    def per_dir(shape, dtype):
        spec = pl.BlockSpec((nc, 2) + shape, lambda i: (i,) + (0,) * (1 + len(shape)))
        return spec, jax.ShapeDtypeStruct((n_chunks, 2) + shape, dtype)

    outs = [
        (pl.BlockSpec((2, nc * CHUNK, width), lambda i: (0, i, 0)),
         jax.ShapeDtypeStruct((2, lay.n, width), F32)),
        per_dir((STACK, LANE), BF16),
        per_dir((STACK, LANE), BF16),
        (pl.BlockSpec((nc, STACK, LANE), lambda i: (i, 0, 0)), jax.ShapeDtypeStruct((n_chunks, STACK, LANE), BF16)),
        per_dir((STACK, LANE), F32),
        per_dir((N_HEADS, 2 * CHUNK, LANE), BF16),
        per_dir((N_HEADS, CHUNK, CHUNK), BF16),
        per_dir((STACK, LANE), BF16),
        per_dir((SUBLANE, LANE), F32),
    ]
    return pl.pallas_call(
        functools.partial(_prep_kernel, chunks=nc),
        grid=(n_chunks // nc,),
        in_specs=[pl.BlockSpec(memory_space=pltpu.SMEM)] + [tok(width)] * 6
                 + [tok(LANE), pl.BlockSpec((nc, 4 * GDN_HEADS, CHUNK), lambda i: (i, 0, 0))],
        out_specs=[o[0] for o in outs],
        out_shape=[o[1] for o in outs],
        compiler_params=_params("parallel"),
        name="scan_prep",
    )(logit.reshape(-1).astype(F32), rq, rk, rv, cq, ck, cv, gb, gt)


def _even_scan(lay, prep, logit, s0_ret, s0_gdn):
    tables = _scan_tables(lay)
    steps = int(tables[0].shape[0])
    width = N_HEADS * LANE

    def view(which):
        def blk(arr, with_dir=True):
            shape = arr.shape[2:] if with_dir else arr.shape[1:]
            if with_dir:
                return pl.BlockSpec((SCAN_CHUNKS, None) + shape,
                                    lambda t, *tb: (tb[which][t], which) + (0,) * len(shape))
            return pl.BlockSpec((SCAN_CHUNKS,) + shape, lambda t, *tb: (tb[which][t],) + (0,) * len(shape))
        return [blk(a, with_dir=(i != 2)) for i, a in enumerate(prep)]

    state_shape = (2, N_HEADS, LANE, LANE)
    s0_spec = pl.BlockSpec((None,) + state_shape, lambda t, *tb: (tb[4][t], 0, 0, 0, 0))
    so_spec = pl.BlockSpec((None,) + state_shape, lambda t, *tb: (tb[5][t], 0, 0, 0, 0))
    out_f = pl.BlockSpec((SCAN_CHUNKS * CHUNK, width), lambda t, *tb: (tb[0][t], 0))
    out_b = pl.BlockSpec((SCAN_CHUNKS * CHUNK, width), lambda t, *tb: (tb[1][t], 0))
    o_sds = jax.ShapeDtypeStruct((lay.n, width), F32)
    so_sds = jax.ShapeDtypeStruct((lay.batch,) + state_shape, F32)
    args = list(prep)
    return pl.pallas_call(
        _scan_kernel,
        grid_spec=pltpu.PrefetchScalarGridSpec(
            num_scalar_prefetch=6,
            grid=(steps,),
            in_specs=[pl.BlockSpec(memory_space=pltpu.SMEM)] + view(0) + view(1) + [s0_spec, s0_spec],
            out_specs=[out_f, out_b, out_f, out_b, so_spec, so_spec],
            scratch_shapes=[pltpu.VMEM(state_shape, F32), pltpu.VMEM(state_shape, F32)],
        ),
        out_shape=[o_sds, o_sds, o_sds, o_sds, so_sds, so_sds],
        compiler_params=_params("arbitrary"),
        name="even_scan",
    )(*tables, logit.reshape(-1).astype(F32), *args, *args, s0_ret, s0_gdn)


def _odd_proj_kernel(x_ref, mod_ref, nw_ref, w_ref, qnw_ref, knw_ref, cos_ref, sin_ref,
                     q_ref, kd_ref, vt_ref, ks_ref, vs_ref, *, p_tiles):
    h = _adaln(x_ref[...], nw_ref[...], mod_ref[3:4, :], mod_ref[4:5, :]).astype(BF16)
    tm = h.shape[0]
    lane = lax.broadcasted_iota(jnp.int32, (tm, LANE), 1)
    lo = lane < ATT_HD
    first16 = (lane & (2 * ROPE_PAIRS - 1)) < ROPE_PAIRS
    cos, sin = cos_ref[...], sin_ref[...]

    def norm_rope(xs, w2):
        sq = xs * xs
        ms_lo = jnp.sum(jnp.where(lo, sq, 0.0), axis=-1, keepdims=True) * (1.0 / ATT_HD)
        ms_hi = jnp.sum(jnp.where(lo, 0.0, sq), axis=-1, keepdims=True) * (1.0 / ATT_HD)
        xn = xs * jnp.where(lo, lax.rsqrt(ms_lo + EPS), lax.rsqrt(ms_hi + EPS)) * w2
        partner = jnp.where(first16, pltpu.roll(xn, LANE - ROPE_PAIRS, 1), pltpu.roll(xn, ROPE_PAIRS, 1))
        return xn * cos + partner * sin

    def both_halves(xs, j):
        swapped = pltpu.roll(xs, ATT_HD, 1)
        return jnp.where(lo, xs, swapped) if j % 2 == 0 else jnp.where(lo, swapped, xs)

    for t in range(ATT_Q // MXU_TILE):
        y = _dot(h, w_ref[:, t * MXU_TILE:(t + 1) * MXU_TILE])
        for p in (2 * t, 2 * t + 1):
            qs = norm_rope(y[:, (p % 2) * LANE:(p % 2 + 1) * LANE], qnw_ref[...])
            q_ref[:, p * LANE:(p + 1) * LANE] = (qs * Q_SCALE).astype(BF16)
    is_prompt = pl.program_id(0) < p_tiles
    assert ATT_KV == MXU_TILE
    yk = _dot(h, w_ref[:, ATT_Q:ATT_Q + ATT_KV])
    yv = _dot(h, w_ref[:, ATT_Q + ATT_KV:ATT_Q + 2 * ATT_KV])
    for p in range(ATT_KV // LANE):
        ks = norm_rope(yk[:, p * LANE:(p + 1) * LANE], knw_ref[...])
        vs = yv[:, p * LANE:(p + 1) * LANE]
        vst = vs.T.astype(BF16)
        for j in (2 * p, 2 * p + 1):
            kd_ref[j] = both_halves(ks, j).astype(BF16)
            for c in range(tm // KEY_CHUNK):
                vt_ref[j, c] = vst[(j % 2) * ATT_HD:(j % 2 + 1) * ATT_HD, c * KEY_CHUNK:(c + 1) * KEY_CHUNK]

        @pl.when(is_prompt)
        def _():
            ks_ref[:, p * LANE:(p + 1) * LANE] = ks
            vs_ref[:, p * LANE:(p + 1) * LANE] = vs


def _rope_tables(lay):
    t = jnp.arange(lay.dec_seq)
    inv = ROPE_THETA ** (-jnp.arange(ROPE_PAIRS, dtype=F32) / ROPE_PAIRS)
    ar = (t // GRID_W).astype(F32)[:, None] * inv
    ac = (t % GRID_W).astype(F32)[:, None] * inv
    cos = jnp.concatenate([jnp.cos(ar), jnp.cos(ar), jnp.cos(ac), jnp.cos(ac)], axis=1)
    sin = jnp.concatenate([-jnp.sin(ar), jnp.sin(ar), -jnp.sin(ac), jnp.sin(ac)], axis=1)
    cos = jnp.concatenate([jnp.tile(cos, (1, 2)), jnp.ones((lay.tm, LANE), F32)], axis=0)
    sin = jnp.concatenate([jnp.tile(sin, (1, 2)), jnp.zeros((lay.tm, LANE), F32)], axis=0)
    return cos, sin


def _odd_proj(lay, x, mods, nw, w, qnw, knw):
    cos, sin = _rope_tables(lay)

    def table_block(i):
        return jnp.where(i < lay.p_tiles, lay.tiles_per_seq, lax.rem(jnp.maximum(i - lay.p_tiles, 0), lay.tiles_per_seq))

    tab = pl.BlockSpec((lay.tm, LANE), lambda i: (table_block(i), 0))
    std = pl.BlockSpec((lay.tm, ATT_KV), lambda i: (jnp.minimum(i, lay.p_tiles - 1), 0))
    tm = lay.tm
    return pl.pallas_call(
        functools.partial(_odd_proj_kernel, p_tiles=lay.p_tiles),
        grid=(lay.tiles,),
        in_specs=[lay.tok(D_MODEL), lay.mod(), _resident((1, D_MODEL)), _resident(w.shape),
                  _resident((1, LANE)), _resident((1, LANE)), tab, tab],
        out_specs=[
            lay.tok(ATT_Q),
            pl.BlockSpec((ATT_KV_HEADS, tm, LANE), lambda i: (0, i, 0)),
            pl.BlockSpec((ATT_KV_HEADS, tm // KEY_CHUNK, ATT_HD, KEY_CHUNK), lambda i: (0, i, 0, 0)),
            std, std,
        ],
        out_shape=[
            jax.ShapeDtypeStruct((lay.n, ATT_Q), BF16),
            jax.ShapeDtypeStruct((ATT_KV_HEADS, lay.n, LANE), BF16),
            jax.ShapeDtypeStruct((ATT_KV_HEADS, lay.n // KEY_CHUNK, ATT_HD, KEY_CHUNK), BF16),
            jax.ShapeDtypeStruct((lay.n_p, ATT_KV), F32),
            jax.ShapeDtypeStruct((lay.n_p, ATT_KV), F32),
        ],
        compiler_params=_params("arbitrary"),
        name="odd_proj",
    )(x, mods, nw.reshape(1, D_MODEL), w, jnp.tile(qnw.reshape(1, ATT_HD), (1, 2)),
      jnp.tile(knw.reshape(1, ATT_HD), (1, 2)), cos, sin)


def _attn_kernel(*refs, cached):
    if cached:
        zero_ref, q_ref, kn_ref, vn_ref, kc_ref, vc_ref, _, o_ref, *s_scr = refs
    else:
        zero_ref, q_ref, kn_ref, vn_ref, o_ref, *s_scr = refs
    z = zero_ref[0]
    tq = MXU_TILE
    n_sub = q_ref.shape[0] // tq
    n_kv = kn_ref.shape[0]
    gw = ATT_GROUP * ATT_HD
    ck = KEY_CHUNK
    n_c = kc_ref.shape[1] // ck if cached else 0
    n_n = kn_ref.shape[1] // ck
    lo = lax.broadcasted_iota(jnp.int32, (tq, LANE), 1) < ATT_HD
    qms, kv_of = [], []
    for s in range(n_sub):
        for j in range(n_kv):
            for g in range(ATT_GROUP):
                q2 = q_ref[s * tq:(s + 1) * tq, j * gw + (g // 2) * LANE:j * gw + (g // 2 + 1) * LANE]
                qms.append(jnp.where(lo, q2, jnp.zeros_like(q2)) if g % 2 == 0
                           else jnp.where(lo, jnp.zeros_like(q2), q2))
                kv_of.append(j)
    n_units = len(qms)

    def keys(j, c):
        return kc_ref[j, c * ck:(c + 1) * ck, :] if c < n_c else kn_ref[j, (c - n_c) * ck:(c - n_c + 1) * ck, :]

    def values_t(j, c):
        return vc_ref[j, c] if c < n_c else vn_ref[j, c - n_c]

    ones_rows = jnp.where(lax.broadcasted_iota(jnp.int32, (2 * SUBLANE, ck), 0) == 0, 1.0, 0.0).astype(BF16)
    m_fin = [None] * n_units
    outs = [None] * n_units
    for ph in range(n_units + 1):
        ga = ph if ph < n_units else None
        gb = ph - 1 if ph >= 1 else None
        ma = jnp.full((1, tq), -jnp.inf, F32)
        acc = jnp.zeros((ATT_HD + 2 * SUBLANE, tq), F32)
        for c in range(n_c + n_n):
            if ga is not None:
                st = _dot_nt(keys(kv_of[ga], c), qms[ga])
                s_scr[ga % 2][z + c] = st
                ma = jnp.maximum(ma, jnp.max(st, axis=0, keepdims=True))
            if gb is not None:
                p = jnp.exp2(s_scr[gb % 2][z + c] - m_fin[gb]).astype(BF16)
                acc = acc + _dot(jnp.concatenate([values_t(kv_of[gb], c), ones_rows], axis=0), p)
        if ga is not None:
            m_fin[ga] = ma
        if gb is not None:
            outs[gb] = acc[:ATT_HD] / acc[ATT_HD:ATT_HD + 1]
    per_sub = n_kv * ATT_GROUP
    for s in range(n_sub):
        o_ref[s * tq:(s + 1) * tq, :] = jnp.concatenate(outs[s * per_sub:(s + 1) * per_sub], axis=0).T.astype(BF16)


def _attention(lay, q, kd, vt, kc, vc):
    gw = ATT_GROUP * ATT_HD
    o_sds = jax.ShapeDtypeStruct((lay.n, ATT_Q), BF16)
    ck = KEY_CHUNK
    assert lay.seq % ck == 0 and lay.dec_seq % ck == 0 and lay.past % ck == 0
    assert lay.seq % MXU_TILE == 0 and lay.dec_seq % MXU_TILE == 0 and ATT_Q_TILE % MXU_TILE == 0

    zero = jnp.zeros((1,), jnp.int32)
    zero_spec = pl.BlockSpec(memory_space=pltpu.SMEM)

    tq = min(ATT_Q_TILE, lay.seq)
    nq = lay.seq // tq
    nck = lay.seq // ck
    out = pl.pallas_call(
        functools.partial(_attn_kernel, cached=False),
        grid=(lay.batch, nq),
        in_specs=[
            zero_spec,
            pl.BlockSpec((tq, ATT_Q), lambda b, i: (b * nq + i, 0)),
            pl.BlockSpec((ATT_KV_HEADS, lay.seq, LANE), lambda b, i: (0, b, 0)),
            pl.BlockSpec((ATT_KV_HEADS, nck, ATT_HD, ck), lambda b, i: (0, b, 0, 0)),
        ],
        out_specs=pl.BlockSpec((tq, ATT_Q), lambda b, i: (b * nq + i, 0)),
        out_shape=o_sds,
        scratch_shapes=[pltpu.VMEM((nck, ck, MXU_TILE), F32)] * 2,
        compiler_params=_params("parallel", "parallel"),
        name="attn_context",
    )(zero, q, kd, vt)

    tq = min(ATT_Q_TILE, lay.dec_seq)
    nq = lay.dec_seq // tq
    nck = lay.dec_seq // ck
    qbase = lay.n_p // tq
    kbase = lay.n_p // lay.dec_seq
    assert lay.n_p % lay.dec_seq == 0
    return pl.pallas_call(
        functools.partial(_attn_kernel, cached=True),
        grid=(lay.dec_batch, ATT_KV_HEADS, nq),
        in_specs=[
            zero_spec,
            pl.BlockSpec((tq, gw), lambda b, j, i: (qbase + b * nq + i, j)),
            pl.BlockSpec((1, lay.dec_seq, LANE), lambda b, j, i: (j, kbase + b, 0)),
            pl.BlockSpec((1, nck, ATT_HD, ck), lambda b, j, i: (j, kbase + b, 0, 0)),
            pl.BlockSpec((None, 1, lay.past, LANE), lambda b, j, i: (b, j, 0, 0)),
            pl.BlockSpec((None, 1, lay.past // ck, ATT_HD, ck), lambda b, j, i: (b, j, 0, 0, 0)),
            pl.BlockSpec(memory_space=pl.ANY),
        ],
        out_specs=pl.BlockSpec((tq, gw), lambda b, j, i: (qbase + b * nq + i, j)),
        out_shape=o_sds,
        scratch_shapes=[pltpu.VMEM((lay.past // ck + nck, ck, MXU_TILE), F32)] * 2,
        input_output_aliases={6: 0},
        compiler_params=_params("parallel", "parallel", "parallel"),
        name="attn_latent",
    )(zero, q, kd, vt, kc, vc, out)


def _pad_state(s):
    return jnp.pad(s, [(0, 0)] * (s.ndim - 2) + [(0, LANE - s.shape[-2]), (0, 0)])


def _cache_keys(c):
    return jnp.tile(c.transpose(0, 2, 1, 3), (1, 1, 1, 2)).astype(BF16)


def _cache_values(c):
    b, t, kvh, hd = c.shape
    return c.reshape(b, t // KEY_CHUNK, KEY_CHUNK, kvh, hd).transpose(0, 3, 1, 4, 2).astype(BF16)


def kernel(x_prompt, x_sample, state_ret, state_gdn, cache_k, cache_v, c, c_ctx,
           mod_w, mod_b, norm_w, ffn_w_in, ffn_w_out, even_w_in, even_w_out,
           ret_decay_logit, ret_norm_w, gdn_conv_w, gdn_A_log, gdn_dt_bias, gdn_norm_w,
           odd_w_in, odd_w_out, q_norm_w, k_norm_w, final_norm_w):
    batch, seq, d = x_prompt.shape
    dec_batch, dec_seq, _ = x_sample.shape
    depth = mod_w.shape[0]
    lay = _Layout(batch, seq, dec_batch, dec_seq, cache_k.shape[2])
    lay_ffn = _Layout(batch, seq, dec_batch, dec_seq, cache_k.shape[2], tile=FFN_TILE)

    x = (x_prompt.reshape(lay.n_p, d), x_sample.reshape(lay.n_s, d))
    n_cond = -(-(1 + dec_batch) // (2 * SUBLANE)) * (2 * SUBLANE)
    cond = jnp.zeros((n_cond, d), F32).at[0].set(c_ctx).at[1:1 + dec_batch].set(c)
    mods = _modulation(cond, mod_w, mod_b)
    ffn_in, ffn_out = ffn_w_in.astype(BF16), ffn_w_out.astype(BF16)

    new_ret, new_gdn, new_k, new_v = [], [], [], []
    for l in range(depth):
        m = mods[l]
        last = l == depth - 1
        x = _ffn(lay_ffn, x, m, norm_w[l, 0], ffn_in, ffn_out, (l, 0), 0)
        if l % 2 == 0:
            e = l // 2
            rq, rk, rv, rg, gg, cq, ck, cv, gb, gt = _even_proj(
                lay, x, m, norm_w[l, 1], _even_weight(even_w_in[e]), gdn_conv_w[e], gdn_A_log[e], gdn_dt_bias[e])
            prep = _scan_prep(lay, rq, rk, rv, cq, ck, cv, gb, gt, ret_decay_logit[e])
            rin = prep[0]
            orf, orb, ogf, ogb, sr, sg = _even_scan(
                lay, prep[1:], ret_decay_logit[e], _pad_state(state_ret[:, e]), state_gdn[:, e])
            new_ret.append(sr[:, :, :, :RET_DK, :])
            new_gdn.append(sg)
            mixer = "even"
            mixer_ins = (orf, orb, ogf, ogb, rg, gg, (rin, 0), (rin, 1),
                         ret_norm_w[e].reshape(1, LANE), gdn_norm_w[e].reshape(1, LANE),
                         even_w_out[e].astype(BF16))
        else:
            o = l // 2
            q, kd, vt, ks, vs = _odd_proj(lay, x, m, norm_w[l, 1], odd_w_in[o].astype(BF16), q_norm_w[o], k_norm_w[o])
            a = _attention(lay, q, kd, vt, _cache_keys(cache_k[:, o]), _cache_values(cache_v[:, o]))
            new_k.append(ks.reshape(batch, seq, ATT_KV_HEADS, ATT_HD))
            new_v.append(vs.reshape(batch, seq, ATT_KV_HEADS, ATT_HD))
            mixer = "odd"
            mixer_ins = (a, odd_w_out[o].astype(BF16))
        x = _ffn(lay, x, m, norm_w[l, 2], ffn_in, ffn_out, (l, 1), 2,
                 final_w=final_norm_w if last else None, mixer=mixer, mixer_ins=mixer_ins)

    y_prompt = x[0].reshape(batch, seq, d)
    y_sample = x[1].reshape(dec_batch, dec_seq, d)
    return (y_prompt, y_sample, jnp.stack(new_ret, axis=1), jnp.stack(new_gdn, axis=1),
            jnp.stack(new_k, axis=1), jnp.stack(new_v, axis=1))
```

```python
import functools

import numpy as np
import jax
import jax.numpy as jnp
from jax import lax
from jax.experimental import pallas as pl
from jax.experimental.pallas import tpu as pltpu

F32 = jnp.float32
BF16 = jnp.bfloat16

D_MODEL = 1024
GRID_W = 64
RET_HEADS = 4
RET_DK = 64
RET_DV = 128
GDN_HEADS = 4
GDN_DK = 128
GDN_DV = 128
CHUNK = 64
ATT_HEADS = 16
ATT_KV_HEADS = 4
ATT_HD = 64
ATT_GROUP = ATT_HEADS // ATT_KV_HEADS
ROPE_THETA = 10000.0
ROPE_PAIRS = ATT_HD // 4
FFN_HIDDEN = 2816
N_MOD = 9
EPS = 1e-6

RET_QK = RET_HEADS * RET_DK
RET_V = RET_HEADS * RET_DV
GDN_QK = GDN_HEADS * GDN_DK
GDN_V = GDN_HEADS * GDN_DV
ATT_Q = ATT_HEADS * ATT_HD
ATT_KV = ATT_KV_HEADS * ATT_HD

LANE = 128
SUBLANE = 8
V7X_VMEM_BYTES = 64 * 1024 * 1024
VMEM_COMPILER_RESERVE = 8 * 1024 * 1024
VMEM_LIMIT = V7X_VMEM_BYTES - VMEM_COMPILER_RESERVE

CHUNK_SHIFT = CHUNK.bit_length() - 1
assert 1 << CHUNK_SHIFT == CHUNK
MOD_COL_TILES = 8
TOKEN_TILE = 512
FFN_TILE = 1024
ATT_Q_TILE = 1024
KEY_CHUNK = 256
Q_SCALE = ATT_HD ** -0.5 * float(np.log2(np.e))
MXU_TILE = 256
FFN_CHUNKS = ((0, 6 * MXU_TILE), (6 * MXU_TILE, FFN_HIDDEN))
HEAD_PAD = LANE
PREP_CHUNKS = 4
SCAN_CHUNKS = 4

def _params(*sem):
    return pltpu.CompilerParams(dimension_semantics=sem, vmem_limit_bytes=VMEM_LIMIT)


def _resident(shape):
    nd = len(shape)
    return pl.BlockSpec(shape, lambda *_: (0,) * nd, pipeline_mode=pl.Buffered(1))


def _resident_at(arr, lead):
    shape = arr.shape[len(lead):]
    return pl.BlockSpec((None,) * len(lead) + shape, lambda *_: tuple(lead) + (0,) * len(shape),
                        pipeline_mode=pl.Buffered(1))


def _dot(a, b):
    return jnp.dot(a, b, preferred_element_type=F32)


def _dot_nt(a, b):
    return lax.dot_general(a, b, (((1,), (1,)), ((), ())), preferred_element_type=F32)


def _dot_tn(a, b):
    return lax.dot_general(a, b, (((0,), (0,)), ((), ())), preferred_element_type=F32)


def _split2(a):
    hi = a.astype(BF16)
    lo = (a - hi.astype(F32)).astype(BF16)
    return hi, lo


def _silu(x):
    return x * jax.nn.sigmoid(x)


def _softplus(x):
    return jnp.maximum(x, 0.0) + jnp.log(1.0 + jnp.exp(-jnp.abs(x)))


def _adaln(x, nw, shift, scale):
    ms = jnp.mean(x * x, axis=-1, keepdims=True)
    return (x * lax.rsqrt(ms + EPS)) * nw * (1.0 + scale) + shift


def _head_rmsnorm(x, w, n_heads):
    parts = []
    for h in range(n_heads):
        xs = x[:, h * LANE:(h + 1) * LANE]
        ms = jnp.mean(xs * xs, axis=-1, keepdims=True)
        parts.append(xs * lax.rsqrt(ms + EPS) * w)
    return jnp.concatenate(parts, axis=1)


def _mod_kernel(cond_ref, w_ref, b_ref, o_ref):
    c = cond_ref[...]
    o_ref[0] = _dot(_silu(c).astype(BF16), w_ref[0].astype(BF16)) + b_ref[0]


def _modulation(cond, mod_w, mod_b):
    depth, d, n = mod_w.shape
    r = cond.shape[0]
    tn = n // MOD_COL_TILES
    out = pl.pallas_call(
        _mod_kernel,
        grid=(depth, n // tn),
        in_specs=[
            pl.BlockSpec((r, d), lambda l, j: (0, 0)),
            pl.BlockSpec((1, d, tn), lambda l, j: (l, 0, j)),
            pl.BlockSpec((1, 1, tn), lambda l, j: (l, 0, j)),
        ],
        out_specs=pl.BlockSpec((1, r, tn), lambda l, j: (l, 0, j)),
        out_shape=jax.ShapeDtypeStruct((depth, r, n), F32),
        compiler_params=_params("parallel", "parallel"),
        name="modulation",
    )(cond, mod_w, mod_b.reshape(depth, 1, n))
    return out.reshape(depth, r, N_MOD, d)


class _Layout:
    def __init__(self, batch, seq, dec_batch, dec_seq, past, tile=TOKEN_TILE):
        self.batch, self.seq, self.dec_batch, self.dec_seq, self.past = batch, seq, dec_batch, dec_seq, past
        self.n_p = batch * seq
        self.n_s = dec_batch * dec_seq
        self.n = self.n_p + self.n_s
        self.tm = min(tile, self.n_p, dec_seq)
        assert self.n_p % self.tm == 0 and dec_seq % self.tm == 0
        self.tiles = self.n // self.tm
        self.p_tiles = self.n_p // self.tm
        self.tiles_per_seq = dec_seq // self.tm

    def group(self, i):
        return jnp.where(i < self.p_tiles, 0, 1 + (i - self.p_tiles) // self.tiles_per_seq)

    def tok(self, width, tm=None):
        tm = tm or self.tm
        return pl.BlockSpec((tm, width), lambda i: (i, 0))

    def mod(self):
        return pl.BlockSpec((None, N_MOD, D_MODEL), lambda i: (self.group(i), 0, 0))


def _ffn_kernel(*refs, j, first, final, p_tiles, mixer):
    refs = list(refs)
    x_refs = [refs.pop(0) for _ in range(2 if first else 1)]
    n_mix = {None: 0, "odd": 2, "even": 11}[mixer]
    mix_refs = [refs.pop(0) for _ in range(n_mix)]
    mod_ref, nw_ref, win_ref, wout_ref = (refs.pop(0) for _ in range(4))
    fw_ref = refs.pop(0) if final else None
    o_refs = refs
    is_prompt = pl.program_id(0) < p_tiles
    x = jnp.where(is_prompt, x_refs[0][...], x_refs[1][...]) if first else x_refs[0][...]
    if mixer == "odd":
        a_ref, wo_ref = mix_refs
        x = x + mod_ref[5:6, :] * _dot(a_ref[...], wo_ref[...])
    elif mixer == "even":
        rf_ref, rb_ref, gf_ref, gb_ref, rg_ref, gg_ref, rif_ref, rib_ref, rnw_ref, gnw_ref, wo_ref = mix_refs
        ret_o = (rf_ref[...] + rif_ref[...]) + (rb_ref[...] + rib_ref[...])
        ret = _head_rmsnorm(ret_o, rnw_ref[...], RET_HEADS) * _silu(rg_ref[...])
        gdn = _head_rmsnorm(gf_ref[...] + gb_ref[...], gnw_ref[...], GDN_HEADS) * _silu(gg_ref[...])
        x = x + mod_ref[5:6, :] * _dot(jnp.concatenate([ret, gdn], axis=1).astype(BF16), wo_ref[...])
    h = _adaln(x, nw_ref[...], mod_ref[3 * j:3 * j + 1, :], mod_ref[3 * j + 1:3 * j + 2, :]).astype(BF16)
    acc = None
    for lo, hi in FFN_CHUNKS:
        a = _dot(h, win_ref[:, lo:hi])
        b = _dot(h, win_ref[:, FFN_HIDDEN + lo:FFN_HIDDEN + hi])
        y = _dot((_silu(a) * b).astype(BF16), wout_ref[lo:hi, :])
        acc = y if acc is None else acc + y
    out = x + (0.5 * mod_ref[3 * j + 2:3 * j + 3, :]) * acc
    if not final:
        o_refs[0][...] = out
        return
    ms = jnp.mean(out * out, axis=-1, keepdims=True)
    out = out * lax.rsqrt(ms + EPS) * fw_ref[...]

    @pl.when(is_prompt)
    def _():
        o_refs[0][...] = out

    @pl.when(jnp.logical_not(is_prompt))
    def _():
        o_refs[1][...] = out


def _ffn(lay, xs, mods, nw, w_in, w_out, which, j, final_w=None, mixer=None, mixer_ins=()):
    first = isinstance(xs, tuple)
    final = final_w is not None
    prompt_rows = pl.BlockSpec((lay.tm, D_MODEL), lambda i: (jnp.minimum(i, lay.p_tiles - 1), 0))
    latent_rows = pl.BlockSpec((lay.tm, D_MODEL), lambda i: (jnp.maximum(i - lay.p_tiles, 0), 0))
    ins = list(xs) if first else [xs]
    specs = [prompt_rows, latent_rows] if first else [lay.tok(D_MODEL)]
    for arr in mixer_ins:
        if isinstance(arr, tuple):
            arr, lead = arr
            ins.append(arr)
            specs.append(pl.BlockSpec((None, lay.tm, arr.shape[2]), lambda i, lead=lead: (lead, i, 0)))
            continue
        ins.append(arr)
        specs.append(lay.tok(arr.shape[1]) if arr.shape[0] == lay.n else _resident(arr.shape))
    ins += [mods, nw.reshape(1, D_MODEL), w_in, w_out]
    specs += [lay.mod(), _resident((1, D_MODEL)), _resident_at(w_in, which), _resident_at(w_out, which)]
    if final:
        ins.append(final_w.reshape(1, D_MODEL))
        specs.append(_resident((1, D_MODEL)))
        out_specs = [prompt_rows, latent_rows]
        out_shape = [jax.ShapeDtypeStruct((lay.n_p, D_MODEL), F32), jax.ShapeDtypeStruct((lay.n_s, D_MODEL), F32)]
    else:
        out_specs = lay.tok(D_MODEL)
        out_shape = jax.ShapeDtypeStruct((lay.n, D_MODEL), F32)
    return pl.pallas_call(
        functools.partial(_ffn_kernel, j=j, first=first, final=final, p_tiles=lay.p_tiles, mixer=mixer),
        grid=(lay.tiles,),
        in_specs=specs,
        out_specs=out_specs,
        out_shape=out_shape,
        compiler_params=pltpu.CompilerParams(
            dimension_semantics=("arbitrary",), vmem_limit_bytes=VMEM_LIMIT,
            allow_input_fusion=[a is w_in or a is w_out for a in ins]),
        name=("ffn_final" if final else ("ffn_first" if first else "ffn")) + ("_" + mixer if mixer else ""),
    )(*ins)


EVEN_WIDTHS = (RET_HEADS * HEAD_PAD, RET_HEADS * HEAD_PAD, RET_V, RET_V, 2 * GDN_QK + GDN_V, GDN_V, LANE)


def _even_proj_kernel(x_ref, prev_ref, next_ref, mod_ref, nw_ref, w_ref, cw_ref, alog_ref, dt_ref,
                      rq_ref, rk_ref, rv_ref, rg_ref, gg_ref, cq_ref, ck_ref, cv_ref, gb_ref, gt_ref,
                      *, n_p, seq, dec_seq):
    tm = x_ref.shape[0]
    shift, scale = mod_ref[3:4, :], mod_ref[4:5, :]
    h = _adaln(x_ref[...], nw_ref[...], shift, scale).astype(BF16)
    halo = jnp.concatenate([prev_ref[...], next_ref[...]], axis=0)
    hh = _adaln(halo, nw_ref[...], shift, scale).astype(BF16)
    offs = np.cumsum((0,) + EVEN_WIDTHS)
    col = lambda i: slice(int(offs[i]), int(offs[i + 1]))

    start = pl.program_id(0) * tm
    in_prompt = start < n_p
    rel = jnp.where(in_prompt, start, start - n_p)
    slen = jnp.where(in_prompt, seq, dec_seq)
    row = lax.broadcasted_iota(jnp.int32, (tm, GDN_QK), 0)
    pos = (rel + row) & (slen - 1)

    def conv(part):
        lanes = slice(int(offs[4]) + part * GDN_QK, int(offs[4]) + (part + 1) * GDN_QK)
        cw = cw_ref[:, part * GDN_QK:(part + 1) * GDN_QK]
        y = _dot(h, w_ref[:, lanes])
        yh = _dot(hh, w_ref[:, lanes])
        ym1 = jnp.where(pos == 0, 0.0, jnp.where(row == 0, yh[SUBLANE - 1:SUBLANE, :], pltpu.roll(y, 1, 0)))
        yp1 = jnp.where(pos == slen - 1, 0.0,
                        jnp.where(row == tm - 1, yh[SUBLANE:SUBLANE + 1, :], pltpu.roll(y, tm - 1, 0)))
        return _silu(ym1 * cw[0:1, :] + y * cw[1:2, :] + yp1 * cw[2:3, :])

    def l2norm_heads(y, o_ref, scale):
        for hd in range(GDN_HEADS):
            v = y[:, hd * LANE:(hd + 1) * LANE]
            o_ref[:, hd * LANE:(hd + 1) * LANE] = v * lax.rsqrt(jnp.sum(v * v, axis=-1, keepdims=True) + EPS) * scale

    l2norm_heads(conv(0), cq_ref, GDN_DK ** -0.5)
    rq_ref[...] = _dot(h, w_ref[:, col(0)])
    rk_ref[...] = _dot(h, w_ref[:, col(1)])
    l2norm_heads(conv(1), ck_ref, 1.0)
    rv_ref[...] = _dot(h, w_ref[:, col(2)])
    rg_ref[...] = _dot(h, w_ref[:, col(3)])
    cv_ref[...] = conv(2)
    gg_ref[...] = _dot(h, w_ref[:, col(5)])
    g = _dot(h, w_ref[:, col(6)])
    lane = lax.broadcasted_iota(jnp.int32, g.shape, 1)
    gb = jnp.where(lane < 2 * GDN_HEADS, -jnp.exp(alog_ref[...]) * _softplus(g + dt_ref[...]), jax.nn.sigmoid(g))
    gb_ref[...] = gb
    gbt = gb.T
    for c in range(tm // CHUNK):
        gt_ref[c] = gbt[0:4 * GDN_HEADS, c * CHUNK:(c + 1) * CHUNK]


def _even_proj(lay, x, mods, nw, w, conv_w, a_log, dt_bias):
    tm = lay.tm
    assert lay.seq & (lay.seq - 1) == 0 and lay.dec_seq & (lay.dec_seq - 1) == 0 and tm % LANE == 0
    nblk8 = lay.n // SUBLANE
    r = tm // SUBLANE
    width = GDN_V
    assert all(wd == width for wd in EVEN_WIDTHS[:4] + EVEN_WIDTHS[5:6]) and GDN_QK == width

    def pad_row(v):
        v = v.reshape(1, -1).astype(F32)
        return jnp.pad(v, ((0, 0), (0, LANE - v.shape[1])))

    tok_out = [(width, F32)] * 8 + [(LANE, F32)]
    return pl.pallas_call(
        functools.partial(_even_proj_kernel, n_p=lay.n_p, seq=lay.seq, dec_seq=lay.dec_seq),
        grid=(lay.tiles,),
        in_specs=[
            lay.tok(D_MODEL),
            pl.BlockSpec((SUBLANE, D_MODEL), lambda i: (jnp.maximum(i * r - 1, 0), 0)),
            pl.BlockSpec((SUBLANE, D_MODEL), lambda i: (jnp.minimum((i + 1) * r, nblk8 - 1), 0)),
            lay.mod(), _resident((1, D_MODEL)), _resident(w.shape), _resident(conv_w.shape),
            _resident((1, LANE)), _resident((1, LANE)),
        ],
        out_specs=[lay.tok(wd) for wd, _ in tok_out]
                  + [pl.BlockSpec((tm // CHUNK, 4 * GDN_HEADS, CHUNK), lambda i: (i, 0, 0))],
        out_shape=[jax.ShapeDtypeStruct((lay.n, wd), dt) for wd, dt in tok_out]
                  + [jax.ShapeDtypeStruct((lay.n // CHUNK, 4 * GDN_HEADS, CHUNK), F32)],
        compiler_params=_params("parallel"),
        name="even_proj",
    )(x, x, x, mods, nw.reshape(1, D_MODEL), w, conv_w, pad_row(a_log), pad_row(dt_bias))


def _even_weight(w):
    d = w.shape[0]
    o = 0
    rq = w[:, o:o + RET_QK]; o += RET_QK
    rk = w[:, o:o + RET_QK]; o += RET_QK
    rest = w[:, o:o + 2 * RET_V + 2 * GDN_QK + 2 * GDN_V]; o += 2 * RET_V + 2 * GDN_QK + 2 * GDN_V
    gab = w[:, o:]

    def pad_heads(m):
        m = m.reshape(d, RET_HEADS, RET_DK)
        return jnp.pad(m, ((0, 0), (0, 0), (0, HEAD_PAD - RET_DK))).reshape(d, RET_HEADS * HEAD_PAD)

    gab = jnp.pad(gab, ((0, 0), (0, LANE - gab.shape[1])))
    return jnp.concatenate([pad_heads(rq), pad_heads(rk * RET_DK ** -0.5), rest, gab], axis=1).astype(BF16)


N_HEADS = 4
STACK = N_HEADS * CHUNK


def _prep_kernel(logit_ref, rq_ref, rk_ref, rv_ref, cq_ref, ck_ref, cv_ref, gb_ref, gt_ref,
                 rin_ref, rqd_ref, rkd_ref, rvb_ref, gu_ref, gwq_ref, gat_ref, gkd_ref, gcd_ref, *, chunks):
    wr = lax.broadcasted_iota(jnp.int32, (CHUNK, STACK), 0)
    wcol = lax.broadcasted_iota(jnp.int32, (CHUNK, STACK), 1)
    wc = wcol & (CHUNK - 1)
    whead = wcol >> CHUNK_SHIFT
    r4 = lax.broadcasted_iota(jnp.int32, (STACK, STACK), 0)
    c4 = lax.broadcasted_iota(jnp.int32, (STACK, STACK), 1)
    same_head = (r4 >> CHUNK_SHIFT) == (c4 >> CHUNK_SHIFT)
    rc = lax.broadcasted_iota(jnp.int32, (CHUNK, CHUNK), 0)
    cc = lax.broadcasted_iota(jnp.int32, (CHUNK, CHUNK), 1)
    rpos = (lax.broadcasted_iota(jnp.int32, (STACK, LANE), 0) & (CHUNK - 1)).astype(F32)
    eye_w = jnp.where(wr == wc, 1.0, 0.0)

    def rows(x, h):
        return x[h * CHUNK:(h + 1) * CHUNK]

    def by_head(parts):
        out = parts[N_HEADS - 1]
        for h in range(N_HEADS - 2, -1, -1):
            out = jnp.where(whead == h, parts[h], out)
        return out

    def block_diag(w):
        return tuple(jnp.where(same_head, jnp.concatenate([x] * N_HEADS, axis=0), jnp.zeros((STACK, STACK), BF16))
                     for x in _split2(w))

    def wide_times(lhs, bh, bl):
        parts = [_split2(a) for a in lhs]
        hh = _dot(jnp.concatenate([x for hi_lo in parts for x in hi_lo], axis=0), bh)
        hl = _dot(jnp.concatenate([hi for hi, _ in parts], axis=0), bl)
        return [hh[2 * i * CHUNK:(2 * i + 1) * CHUNK] + hh[(2 * i + 1) * CHUNK:(2 * i + 2) * CHUNK]
                + hl[i * CHUNK:(i + 1) * CHUNK] for i in range(len(lhs))]

    ret_const = []
    for d in range(2):
        lg4 = jnp.concatenate(
            [-_softplus(-jnp.full((CHUNK, STACK), logit_ref[d * N_HEADS + h], F32)) for h in range(N_HEADS)], axis=0)
        tri4 = (r4 >= c4) if d == 0 else (r4 <= c4)
        dist4 = ((r4 - c4) if d == 0 else (c4 - r4)).astype(F32)
        dec4 = jnp.exp(jnp.where(same_head, jnp.where(tri4, lg4 * dist4, -jnp.inf), -jnp.inf))
        spos = rpos if d == 0 else (CHUNK - 1.0) - rpos
        lgc = lg4[:, :LANE]
        ret_const.append((dec4, jnp.exp(lgc * (spos + 1.0)), jnp.exp(lgc * ((CHUNK - 1.0) - spos))))

    setup = []
    for n in range(chunks):
        tok = slice(n * CHUNK, (n + 1) * CHUNK)

        def stack(ref):
            return jnp.concatenate([ref[tok, h * LANE:(h + 1) * LANE] for h in range(N_HEADS)], axis=0)

        q4, k4, v4 = stack(rq_ref), stack(rk_ref), stack(rv_ref)
        q4b, k4b, v4b = q4.astype(BF16), k4.astype(BF16), v4.astype(BF16)
        rvb_ref[n] = v4b
        qk_ret = _dot_nt(q4b, k4b)
        gq4, gk4, gv4 = stack(cq_ref), stack(ck_ref), stack(cv_ref)
        gq4b, gk4b = gq4.astype(BF16), gk4.astype(BF16)
        gb = gb_ref[tok, :]
        gt = gt_ref[n]
        gh, gl = _split2(gb)
        gll = (gb - gh.astype(F32) - gl.astype(F32)).astype(BF16)
        th, tl = _split2(gt)
        tll = (gt - th.astype(F32) - tl.astype(F32)).astype(BF16)

        intra = _dot(jnp.concatenate([(qk_ret * ret_const[d][0]).astype(BF16) for d in range(2)], axis=0), v4b)
        for d in range(2):
            _, qdec, kdec = ret_const[d]
            for h in range(N_HEADS):
                rin_ref[d, n * CHUNK:(n + 1) * CHUNK, h * LANE:(h + 1) * LANE] = (
                    intra[d * STACK + h * CHUNK:d * STACK + (h + 1) * CHUNK])
            rqd_ref[n, d] = (q4 * qdec).astype(BF16)
            rkd_ref[n, d] = (k4 * kdec).astype(BF16)
        gqk = [_dot_nt(rows(gq4b, h), rows(gk4b, h)) for h in range(N_HEADS)]

        for d in range(2):
            fwd = d == 0
            low = jnp.where((rc >= cc) if fwd else (rc <= cc), 1.0, 0.0).astype(BF16)
            upp_w = jnp.where((wr <= wc) if fwd else (wr >= wc), 1.0, 0.0).astype(BF16)
            tri_w = (wr >= wc) if fwd else (wr <= wc)
            tri_c = (rc >= cc) if fwd else (rc <= cc)
            gc_col = _dot(low, gh) + _dot(low, gl) + _dot(low, gll)
            t3 = _dot(jnp.concatenate([th, tl, tll], axis=0), upp_w)
            gc_row = t3[:16] + t3[16:32] + t3[32:]
            last_row = CHUNK - 1 if fwd else 0
            idx = [d * N_HEADS + h for h in range(N_HEADS)]
            gcol = [gc_col[:, i:i + 1] for i in idx]
            glast = [gc_col[last_row:last_row + 1, i:i + 1] for i in idx]
            beta = [gb[:, 2 * N_HEADS + i:2 * N_HEADS + i + 1] for i in idx]
            col4 = lambda parts: jnp.concatenate([jnp.broadcast_to(p, (CHUNK, 1)) for p in parts], axis=0)
            gcol4, glast4, beta4 = col4(gcol), col4(glast), col4(beta)
            eg4 = jnp.exp(gcol4)
            kb4 = gk4 * beta4
            gcol_w = by_head([jnp.broadcast_to(g, (CHUNK, STACK)) for g in gcol])
            grow_w = by_head([jnp.broadcast_to(gc_row[i:i + 1, :], (CHUNK, STACK)) for i in idx])
            dmat_w = jnp.exp(jnp.where(tri_w, gcol_w - grow_w, -jnp.inf))
            kk = _dot_nt(kb4.astype(BF16), gk4b)
            kk_w = by_head([rows(kk, h) for h in range(N_HEADS)])
            p = jnp.where(wr == wc, 0.0, -(kk_w * dmat_w))
            setup.append((n, d, p, gcol, glast, gcol4, glast4, beta4, eg4, kb4, gc_row, idx, tri_c,
                          gq4, gk4, gv4, gqk))

    ps = [s[2] for s in setup]
    tinvs = [eye_w + p for p in ps]
    ps = [wide_times([p], *block_diag(p))[0] for p in ps]
    for _ in range(4):
        new = [wide_times([p, t], *block_diag(p)) for p, t in zip(ps, tinvs)]
        ps = [n[0] for n in new]
        tinvs = [t + n[1] for t, n in zip(tinvs, new)]
    tinvs = [t + wide_times([t], *block_diag(p))[0] for p, t in zip(ps, tinvs)]
    t_bds = [block_diag(t) for t in tinvs]
    for (n, d, _, gcol, glast, gcol4, glast4, beta4, eg4, kb4, gc_row, idx, tri_c,
         gq4, gk4, gv4, gqk), (th_bd, tl_bd) in zip(setup, t_bds):
        rh, rl = _split2(jnp.concatenate([gv4 * beta4, kb4 * eg4], axis=1))
        both = _dot(jnp.concatenate([th_bd, tl_bd], axis=0), rh)
        uw = both[:STACK] + both[STACK:] + _dot(th_bd, rl)
        gu_ref[n, d] = uw[:, :GDN_DV]
        w4 = uw[:, GDN_DV:].astype(BF16)
        qd4 = (gq4 * eg4).astype(BF16)
        gkd_ref[n, d] = (gk4 * jnp.exp(glast4 - gcol4)).astype(BF16)
        for h in range(N_HEADS):
            gwq_ref[n, d, h] = jnp.concatenate([rows(w4, h), rows(qd4, h)], axis=0)
            dm = jnp.exp(jnp.where(tri_c, gcol[h] - gc_row[idx[h]:idx[h] + 1, :CHUNK], -jnp.inf))
            gat_ref[n, d, h] = (gqk[h] * dm).astype(BF16)
        gcd_ref[n, d] = jnp.concatenate(
            [jnp.broadcast_to(jnp.exp(g), (1, LANE)) for g in glast] + [jnp.zeros((SUBLANE - N_HEADS, LANE), F32)],
            axis=0)


def _scan_kernel(fblk, bblk, first, last, s0idx, soidx, logit_ref, *refs):
    del fblk, bblk, s0idx, soidx
    views = (refs[0:8], refs[8:16])
    s0r_ref, s0g_ref = refs[16], refs[17]
    o_refs = ((refs[18], refs[20]), (refs[19], refs[21]))
    sro_ref, sgo_ref = refs[22], refs[23]
    sr_ref, sg_ref = refs[24], refs[25]
    t = pl.program_id(0)

    @pl.when(first[t] == 1)
    def _():
        sr_ref[...] = s0r_ref[...]
        sg_ref[...] = s0g_ref[...]

    @pl.when(first[t] == 2)
    def _():
        sr_ref[...] = jnp.zeros(sr_ref.shape, F32)
        sg_ref[...] = jnp.zeros(sg_ref.shape, F32)

    chains = [(d, h) for d in range(2) for h in range(N_HEADS)]
    s_gdn = {c: sg_ref[c[0], c[1]] for c in chains}
    s_ret = {c: sr_ref[c[0], c[1]] for c in chains}
    ret_decay = {(d, h): jnp.exp(-_softplus(-jnp.full((1, LANE), logit_ref[d * N_HEADS + h], F32)) * float(CHUNK))
                 for d, h in chains}
    n_sub = views[0][0].shape[0]
    for sub in range(n_sub):
        at = (sub, n_sub - 1 - sub)

        def rows(d, k, h):
            return views[d][k][at[d], h * CHUNK:(h + 1) * CHUNK, :]

        ws = {(d, h): _dot(views[d][4][at[d], h], s_gdn[d, h].astype(BF16)) for d, h in chains}
        inter = {(d, h): _dot(rows(d, 0, h), s_ret[d, h].astype(BF16)) for d, h in chains}
        kv = {(d, h): _dot_tn(rows(d, 1, h), rows(d, 2, h)) for d, h in chains}
        v_new = {(d, h): (rows(d, 3, h) - ws[d, h][:CHUNK]).astype(BF16) for d, h in chains}
        intra = {(d, h): _dot(views[d][5][at[d], h], v_new[d, h]) for d, h in chains}
        kvn = {(d, h): _dot_tn(rows(d, 6, h), v_new[d, h]) for d, h in chains}
        for d, h in chains:
            tok = slice(at[d] * CHUNK, (at[d] + 1) * CHUNK)
            lanes = slice(h * LANE, (h + 1) * LANE)
            o_refs[d][0][tok, lanes] = inter[d, h]
            o_refs[d][1][tok, lanes] = ws[d, h][CHUNK:] + intra[d, h]
            s_ret[d, h] = s_ret[d, h] * ret_decay[d, h] + kv[d, h]
            s_gdn[d, h] = s_gdn[d, h] * views[d][7][at[d], h:h + 1, :] + kvn[d, h]
    for d, h in chains:
        sr_ref[d, h] = s_ret[d, h]
        sg_ref[d, h] = s_gdn[d, h]

    @pl.when(last[t] == 1)
    def _():
        sro_ref[...] = sr_ref[...]
        sgo_ref[...] = sg_ref[...]


def _scan_tables(lay):
    fblk, bblk, first, last, s0idx, soidx = [], [], [], [], [], []
    blk = SCAN_CHUNKS * CHUNK
    assert lay.seq % blk == 0 and lay.dec_seq % blk == 0
    cp, cs = lay.seq // blk, lay.dec_seq // blk
    for b in range(lay.dec_batch):
        base = lay.n_p // blk + b * cs
        for s in range(cs):
            fblk.append(base + s); bblk.append(base + cs - 1 - s)
            first.append(1 if s == 0 else 0); last.append(0); s0idx.append(b); soidx.append(0)
    for b in range(lay.batch):
        base = b * cp
        for s in range(cp):
            fblk.append(base + s); bblk.append(base + cp - 1 - s)
            first.append(2 if s == 0 else 0); last.append(1 if s == cp - 1 else 0)
            s0idx.append(lay.dec_batch - 1); soidx.append(b)
    return [jnp.asarray(np.asarray(a, np.int32)) for a in (fblk, bblk, first, last, s0idx, soidx)]


def _scan_prep(lay, rq, rk, rv, cq, ck, cv, gb, gt, logit):
    n_chunks = lay.n // CHUNK
    nc = PREP_CHUNKS
    assert n_chunks % nc == 0
    width = N_HEADS * LANE

    def tok(wd):
        return pl.BlockSpec((nc * CHUNK, wd), lambda i: (i, 0))

    def per_dir(shape, dtype):
        spec = pl.BlockSpec((nc, 2) + shape, lambda i: (i,) + (0,) * (1 + len(shape)))
        return spec, jax.ShapeDtypeStruct((n_chunks, 2) + shape, dtype)

    outs = [
        (pl.BlockSpec((2, nc * CHUNK, width), lambda i: (0, i, 0)),
         jax.ShapeDtypeStruct((2, lay.n, width), F32)),
        per_dir((STACK, LANE), BF16),
        per_dir((STACK, LANE), BF16),
        (pl.BlockSpec((nc, STACK, LANE), lambda i: (i, 0, 0)), jax.ShapeDtypeStruct((n_chunks, STACK, LANE), BF16)),
        per_dir((STACK, LANE), F32),
        per_dir((N_HEADS, 2 * CHUNK, LANE), BF16),
        per_dir((N_HEADS, CHUNK, CHUNK), BF16),
        per_dir((STACK, LANE), BF16),
        per_dir((SUBLANE, LANE), F32),
    ]
    return pl.pallas_call(
        functools.partial(_prep_kernel, chunks=nc),
        grid=(n_chunks // nc,),
        in_specs=[pl.BlockSpec(memory_space=pltpu.SMEM)] + [tok(width)] * 6
                 + [tok(LANE), pl.BlockSpec((nc, 4 * GDN_HEADS, CHUNK), lambda i: (i, 0, 0))],
        out_specs=[o[0] for o in outs],
        out_shape=[o[1] for o in outs],
        compiler_params=_params("parallel"),
        name="scan_prep",
    )(logit.reshape(-1).astype(F32), rq, rk, rv, cq, ck, cv, gb, gt)


def _even_scan(lay, prep, logit, s0_ret, s0_gdn):
    tables = _scan_tables(lay)
    steps = int(tables[0].shape[0])
    width = N_HEADS * LANE

    def view(which):
        def blk(arr, with_dir=True):
            shape = arr.shape[2:] if with_dir else arr.shape[1:]
            if with_dir:
                return pl.BlockSpec((SCAN_CHUNKS, None) + shape,
                                    lambda t, *tb: (tb[which][t], which) + (0,) * len(shape))
            return pl.BlockSpec((SCAN_CHUNKS,) + shape, lambda t, *tb: (tb[which][t],) + (0,) * len(shape))
        return [blk(a, with_dir=(i != 2)) for i, a in enumerate(prep)]

    state_shape = (2, N_HEADS, LANE, LANE)
    s0_spec = pl.BlockSpec((None,) + state_shape, lambda t, *tb: (tb[4][t], 0, 0, 0, 0))
    so_spec = pl.BlockSpec((None,) + state_shape, lambda t, *tb: (tb[5][t], 0, 0, 0, 0))
    out_f = pl.BlockSpec((SCAN_CHUNKS * CHUNK, width), lambda t, *tb: (tb[0][t], 0))
    out_b = pl.BlockSpec((SCAN_CHUNKS * CHUNK, width), lambda t, *tb: (tb[1][t], 0))
    o_sds = jax.ShapeDtypeStruct((lay.n, width), F32)
    so_sds = jax.ShapeDtypeStruct((lay.batch,) + state_shape, F32)
    args = list(prep)
    return pl.pallas_call(
        _scan_kernel,
        grid_spec=pltpu.PrefetchScalarGridSpec(
            num_scalar_prefetch=6,
            grid=(steps,),
            in_specs=[pl.BlockSpec(memory_space=pltpu.SMEM)] + view(0) + view(1) + [s0_spec, s0_spec],
            out_specs=[out_f, out_b, out_f, out_b, so_spec, so_spec],
            scratch_shapes=[pltpu.VMEM(state_shape, F32), pltpu.VMEM(state_shape, F32)],
        ),
        out_shape=[o_sds, o_sds, o_sds, o_sds, so_sds, so_sds],
        compiler_params=_params("arbitrary"),
        name="even_scan",
    )(*tables, logit.reshape(-1).astype(F32), *args, *args, s0_ret, s0_gdn)


def _odd_proj_kernel(x_ref, mod_ref, nw_ref, w_ref, qnw_ref, knw_ref, cos_ref, sin_ref,
                     q_ref, kd_ref, vt_ref, ks_ref, vs_ref, *, p_tiles):
    h = _adaln(x_ref[...], nw_ref[...], mod_ref[3:4, :], mod_ref[4:5, :]).astype(BF16)
    tm = h.shape[0]
    lane = lax.broadcasted_iota(jnp.int32, (tm, LANE), 1)
    lo = lane < ATT_HD
    first16 = (lane & (2 * ROPE_PAIRS - 1)) < ROPE_PAIRS
    cos, sin = cos_ref[...], sin_ref[...]

    def norm_rope(xs, w2):
        sq = xs * xs
        ms_lo = jnp.sum(jnp.where(lo, sq, 0.0), axis=-1, keepdims=True) * (1.0 / ATT_HD)
        ms_hi = jnp.sum(jnp.where(lo, 0.0, sq), axis=-1, keepdims=True) * (1.0 / ATT_HD)
        xn = xs * jnp.where(lo, lax.rsqrt(ms_lo + EPS), lax.rsqrt(ms_hi + EPS)) * w2
        partner = jnp.where(first16, pltpu.roll(xn, LANE - ROPE_PAIRS, 1), pltpu.roll(xn, ROPE_PAIRS, 1))
        return xn * cos + partner * sin

    def both_halves(xs, j):
        swapped = pltpu.roll(xs, ATT_HD, 1)
        return jnp.where(lo, xs, swapped) if j % 2 == 0 else jnp.where(lo, swapped, xs)

    for t in range(ATT_Q // MXU_TILE):
        y = _dot(h, w_ref[:, t * MXU_TILE:(t + 1) * MXU_TILE])
        for p in (2 * t, 2 * t + 1):
            qs = norm_rope(y[:, (p % 2) * LANE:(p % 2 + 1) * LANE], qnw_ref[...])
            q_ref[:, p * LANE:(p + 1) * LANE] = (qs * Q_SCALE).astype(BF16)
    is_prompt = pl.program_id(0) < p_tiles
    assert ATT_KV == MXU_TILE
    yk = _dot(h, w_ref[:, ATT_Q:ATT_Q + ATT_KV])
    yv = _dot(h, w_ref[:, ATT_Q + ATT_KV:ATT_Q + 2 * ATT_KV])
    for p in range(ATT_KV // LANE):
        ks = norm_rope(yk[:, p * LANE:(p + 1) * LANE], knw_ref[...])
        vs = yv[:, p * LANE:(p + 1) * LANE]
        vst = vs.T.astype(BF16)
        for j in (2 * p, 2 * p + 1):
            kd_ref[j] = both_halves(ks, j).astype(BF16)
            for c in range(tm // KEY_CHUNK):
                vt_ref[j, c] = vst[(j % 2) * ATT_HD:(j % 2 + 1) * ATT_HD, c * KEY_CHUNK:(c + 1) * KEY_CHUNK]

        @pl.when(is_prompt)
        def _():
            ks_ref[:, p * LANE:(p + 1) * LANE] = ks
            vs_ref[:, p * LANE:(p + 1) * LANE] = vs


def _rope_tables(lay):
    t = jnp.arange(lay.dec_seq)
    inv = ROPE_THETA ** (-jnp.arange(ROPE_PAIRS, dtype=F32) / ROPE_PAIRS)
    ar = (t // GRID_W).astype(F32)[:, None] * inv
    ac = (t % GRID_W).astype(F32)[:, None] * inv
    cos = jnp.concatenate([jnp.cos(ar), jnp.cos(ar), jnp.cos(ac), jnp.cos(ac)], axis=1)
    sin = jnp.concatenate([-jnp.sin(ar), jnp.sin(ar), -jnp.sin(ac), jnp.sin(ac)], axis=1)
    cos = jnp.concatenate([jnp.tile(cos, (1, 2)), jnp.ones((lay.tm, LANE), F32)], axis=0)
    sin = jnp.concatenate([jnp.tile(sin, (1, 2)), jnp.zeros((lay.tm, LANE), F32)], axis=0)
    return cos, sin


def _odd_proj(lay, x, mods, nw, w, qnw, knw):
    cos, sin = _rope_tables(lay)

    def table_block(i):
        return jnp.where(i < lay.p_tiles, lay.tiles_per_seq, lax.rem(jnp.maximum(i - lay.p_tiles, 0), lay.tiles_per_seq))

    tab = pl.BlockSpec((lay.tm, LANE), lambda i: (table_block(i), 0))
    std = pl.BlockSpec((lay.tm, ATT_KV), lambda i: (jnp.minimum(i, lay.p_tiles - 1), 0))
    tm = lay.tm
    return pl.pallas_call(
        functools.partial(_odd_proj_kernel, p_tiles=lay.p_tiles),
        grid=(lay.tiles,),
        in_specs=[lay.tok(D_MODEL), lay.mod(), _resident((1, D_MODEL)), _resident(w.shape),
                  _resident((1, LANE)), _resident((1, LANE)), tab, tab],
        out_specs=[
            lay.tok(ATT_Q),
            pl.BlockSpec((ATT_KV_HEADS, tm, LANE), lambda i: (0, i, 0)),
            pl.BlockSpec((ATT_KV_HEADS, tm // KEY_CHUNK, ATT_HD, KEY_CHUNK), lambda i: (0, i, 0, 0)),
            std, std,
        ],
        out_shape=[
            jax.ShapeDtypeStruct((lay.n, ATT_Q), BF16),
            jax.ShapeDtypeStruct((ATT_KV_HEADS, lay.n, LANE), BF16),
            jax.ShapeDtypeStruct((ATT_KV_HEADS, lay.n // KEY_CHUNK, ATT_HD, KEY_CHUNK), BF16),
            jax.ShapeDtypeStruct((lay.n_p, ATT_KV), F32),
            jax.ShapeDtypeStruct((lay.n_p, ATT_KV), F32),
        ],
        compiler_params=_params("arbitrary"),
        name="odd_proj",
    )(x, mods, nw.reshape(1, D_MODEL), w, jnp.tile(qnw.reshape(1, ATT_HD), (1, 2)),
      jnp.tile(knw.reshape(1, ATT_HD), (1, 2)), cos, sin)


def _attn_kernel(*refs, cached):
    if cached:
        zero_ref, q_ref, kn_ref, vn_ref, kc_ref, vc_ref, _, o_ref, *s_scr = refs
    else:
        zero_ref, q_ref, kn_ref, vn_ref, o_ref, *s_scr = refs
    z = zero_ref[0]
    tq = MXU_TILE
    n_sub = q_ref.shape[0] // tq
    n_kv = kn_ref.shape[0]
    gw = ATT_GROUP * ATT_HD
    ck = KEY_CHUNK
    n_c = kc_ref.shape[1] // ck if cached else 0
    n_n = kn_ref.shape[1] // ck
    lo = lax.broadcasted_iota(jnp.int32, (tq, LANE), 1) < ATT_HD
    qms, kv_of = [], []
    for s in range(n_sub):
        for j in range(n_kv):
            for g in range(ATT_GROUP):
                q2 = q_ref[s * tq:(s + 1) * tq, j * gw + (g // 2) * LANE:j * gw + (g // 2 + 1) * LANE]
                qms.append(jnp.where(lo, q2, jnp.zeros_like(q2)) if g % 2 == 0
                           else jnp.where(lo, jnp.zeros_like(q2), q2))
                kv_of.append(j)
    n_units = len(qms)

    def keys(j, c):
        return kc_ref[j, c * ck:(c + 1) * ck, :] if c < n_c else kn_ref[j, (c - n_c) * ck:(c - n_c + 1) * ck, :]

    def values_t(j, c):
        return vc_ref[j, c] if c < n_c else vn_ref[j, c - n_c]

    ones_rows = jnp.where(lax.broadcasted_iota(jnp.int32, (2 * SUBLANE, ck), 0) == 0, 1.0, 0.0).astype(BF16)
    m_fin = [None] * n_units
    outs = [None] * n_units
    for ph in range(n_units + 1):
        ga = ph if ph < n_units else None
        gb = ph - 1 if ph >= 1 else None
        ma = jnp.full((1, tq), -jnp.inf, F32)
        acc = jnp.zeros((ATT_HD + 2 * SUBLANE, tq), F32)
        for c in range(n_c + n_n):
            if ga is not None:
                st = _dot_nt(keys(kv_of[ga], c), qms[ga])
                s_scr[ga % 2][z + c] = st
                ma = jnp.maximum(ma, jnp.max(st, axis=0, keepdims=True))
            if gb is not None:
                p = jnp.exp2(s_scr[gb % 2][z + c] - m_fin[gb]).astype(BF16)
                acc = acc + _dot(jnp.concatenate([values_t(kv_of[gb], c), ones_rows], axis=0), p)
        if ga is not None:
            m_fin[ga] = ma
        if gb is not None:
            outs[gb] = acc[:ATT_HD] / acc[ATT_HD:ATT_HD + 1]
    per_sub = n_kv * ATT_GROUP
    for s in range(n_sub):
        o_ref[s * tq:(s + 1) * tq, :] = jnp.concatenate(outs[s * per_sub:(s + 1) * per_sub], axis=0).T.astype(BF16)


def _attention(lay, q, kd, vt, kc, vc):
    gw = ATT_GROUP * ATT_HD
    o_sds = jax.ShapeDtypeStruct((lay.n, ATT_Q), BF16)
    ck = KEY_CHUNK
    assert lay.seq % ck == 0 and lay.dec_seq % ck == 0 and lay.past % ck == 0
    assert lay.seq % MXU_TILE == 0 and lay.dec_seq % MXU_TILE == 0 and ATT_Q_TILE % MXU_TILE == 0

    zero = jnp.zeros((1,), jnp.int32)
    zero_spec = pl.BlockSpec(memory_space=pltpu.SMEM)

    tq = min(ATT_Q_TILE, lay.seq)
    nq = lay.seq // tq
    nck = lay.seq // ck
    out = pl.pallas_call(
        functools.partial(_attn_kernel, cached=False),
        grid=(lay.batch, nq),
        in_specs=[
            zero_spec,
            pl.BlockSpec((tq, ATT_Q), lambda b, i: (b * nq + i, 0)),
            pl.BlockSpec((ATT_KV_HEADS, lay.seq, LANE), lambda b, i: (0, b, 0)),
            pl.BlockSpec((ATT_KV_HEADS, nck, ATT_HD, ck), lambda b, i: (0, b, 0, 0)),
        ],
        out_specs=pl.BlockSpec((tq, ATT_Q), lambda b, i: (b * nq + i, 0)),
        out_shape=o_sds,
        scratch_shapes=[pltpu.VMEM((nck, ck, MXU_TILE), F32)] * 2,
        compiler_params=_params("parallel", "parallel"),
        name="attn_context",
    )(zero, q, kd, vt)

    tq = min(ATT_Q_TILE, lay.dec_seq)
    nq = lay.dec_seq // tq
    nck = lay.dec_seq // ck
    qbase = lay.n_p // tq
    kbase = lay.n_p // lay.dec_seq
    assert lay.n_p % lay.dec_seq == 0
    return pl.pallas_call(
        functools.partial(_attn_kernel, cached=True),
        grid=(lay.dec_batch, ATT_KV_HEADS, nq),
        in_specs=[
            zero_spec,
            pl.BlockSpec((tq, gw), lambda b, j, i: (qbase + b * nq + i, j)),
            pl.BlockSpec((1, lay.dec_seq, LANE), lambda b, j, i: (j, kbase + b, 0)),
            pl.BlockSpec((1, nck, ATT_HD, ck), lambda b, j, i: (j, kbase + b, 0, 0)),
            pl.BlockSpec((None, 1, lay.past, LANE), lambda b, j, i: (b, j, 0, 0)),
            pl.BlockSpec((None, 1, lay.past // ck, ATT_HD, ck), lambda b, j, i: (b, j, 0, 0, 0)),
            pl.BlockSpec(memory_space=pl.ANY),
        ],
        out_specs=pl.BlockSpec((tq, gw), lambda b, j, i: (qbase + b * nq + i, j)),
        out_shape=o_sds,
        scratch_shapes=[pltpu.VMEM((lay.past // ck + nck, ck, MXU_TILE), F32)] * 2,
        input_output_aliases={6: 0},
        compiler_params=_params("parallel", "parallel", "parallel"),
        name="attn_latent",
    )(zero, q, kd, vt, kc, vc, out)


def _pad_state(s):
    return jnp.pad(s, [(0, 0)] * (s.ndim - 2) + [(0, LANE - s.shape[-2]), (0, 0)])


def _cache_keys(c):
    return jnp.tile(c.transpose(0, 2, 1, 3), (1, 1, 1, 2)).astype(BF16)


def _cache_values(c):
    b, t, kvh, hd = c.shape
    return c.reshape(b, t // KEY_CHUNK, KEY_CHUNK, kvh, hd).transpose(0, 3, 1, 4, 2).astype(BF16)


def kernel(x_prompt, x_sample, state_ret, state_gdn, cache_k, cache_v, c, c_ctx,
           mod_w, mod_b, norm_w, ffn_w_in, ffn_w_out, even_w_in, even_w_out,
           ret_decay_logit, ret_norm_w, gdn_conv_w, gdn_A_log, gdn_dt_bias, gdn_norm_w,
           odd_w_in, odd_w_out, q_norm_w, k_norm_w, final_norm_w):
    batch, seq, d = x_prompt.shape
    dec_batch, dec_seq, _ = x_sample.shape
    depth = mod_w.shape[0]
    lay = _Layout(batch, seq, dec_batch, dec_seq, cache_k.shape[2])
    lay_ffn = _Layout(batch, seq, dec_batch, dec_seq, cache_k.shape[2], tile=FFN_TILE)

    x = (x_prompt.reshape(lay.n_p, d), x_sample.reshape(lay.n_s, d))
    n_cond = -(-(1 + dec_batch) // (2 * SUBLANE)) * (2 * SUBLANE)
    cond = jnp.zeros((n_cond, d), F32).at[0].set(c_ctx).at[1:1 + dec_batch].set(c)
    mods = _modulation(cond, mod_w, mod_b)
    ffn_in, ffn_out = ffn_w_in.astype(BF16), ffn_w_out.astype(BF16)

    new_ret, new_gdn, new_k, new_v = [], [], [], []
    for l in range(depth):
        m = mods[l]
        last = l == depth - 1
        x = _ffn(lay_ffn, x, m, norm_w[l, 0], ffn_in, ffn_out, (l, 0), 0)
        if l % 2 == 0:
            e = l // 2
            rq, rk, rv, rg, gg, cq, ck, cv, gb, gt = _even_proj(
                lay, x, m, norm_w[l, 1], _even_weight(even_w_in[e]), gdn_conv_w[e], gdn_A_log[e], gdn_dt_bias[e])
            prep = _scan_prep(lay, rq, rk, rv, cq, ck, cv, gb, gt, ret_decay_logit[e])
            rin = prep[0]
            orf, orb, ogf, ogb, sr, sg = _even_scan(
                lay, prep[1:], ret_decay_logit[e], _pad_state(state_ret[:, e]), state_gdn[:, e])
            new_ret.append(sr[:, :, :, :RET_DK, :])
            new_gdn.append(sg)
            mixer = "even"
            mixer_ins = (orf, orb, ogf, ogb, rg, gg, (rin, 0), (rin, 1),
                         ret_norm_w[e].reshape(1, LANE), gdn_norm_w[e].reshape(1, LANE),
                         even_w_out[e].astype(BF16))
        else:
            o = l // 2
            q, kd, vt, ks, vs = _odd_proj(lay, x, m, norm_w[l, 1], odd_w_in[o].astype(BF16), q_norm_w[o], k_norm_w[o])
            a = _attention(lay, q, kd, vt, _cache_keys(cache_k[:, o]), _cache_values(cache_v[:, o]))
            new_k.append(ks.reshape(batch, seq, ATT_KV_HEADS, ATT_HD))
            new_v.append(vs.reshape(batch, seq, ATT_KV_HEADS, ATT_HD))
            mixer = "odd"
            mixer_ins = (a, odd_w_out[o].astype(BF16))
        x = _ffn(lay, x, m, norm_w[l, 2], ffn_in, ffn_out, (l, 1), 2,
                 final_w=final_norm_w if last else None, mixer=mixer, mixer_ins=mixer_ins)

    y_prompt = x[0].reshape(batch, seq, d)
    y_sample = x[1].reshape(dec_batch, dec_seq, d)
    return (y_prompt, y_sample, jnp.stack(new_ret, axis=1), jnp.stack(new_gdn, axis=1),
            jnp.stack(new_k, axis=1), jnp.stack(new_v, axis=1))
```
